```python
import jax, jax.numpy as jnp
from jax import lax
import numpy as np

D_MODEL = 1024
BATCH = 4
SEQ = 4096
DEPTH = 4

GRID_W = 64
CTX_LEN = 256
HEAD_DIM = 64
EPS = 1e-6
NEG_INF = -1e30
A_HEADS = 8
A_KV_HEADS = 2
A_GROUP = A_HEADS // A_KV_HEADS
A_WINDOW = 128
A_BLOCK = 128
A_WIDTH = A_HEADS * HEAD_DIM
A_KV_WIDTH = A_KV_HEADS * HEAD_DIM
ROPE_BASE = 10000.0
B_GROUPS = 4
B_WIDTH = B_GROUPS * HEAD_DIM
CONV_W = 3
R_HEADS = 4
R_WIDTH = R_HEADS * HEAD_DIM
R_CHUNK = 128
F_GROUPS = 4
F_WIDTH = F_GROUPS * HEAD_DIM
N_BRANCH = 4

IN_NAMES = ("a_q", "a_k", "a_v", "a_z", "b_u", "b_b", "b_c", "b_z", "r_q", "r_k", "r_v", "r_z", "f_u", "f_z", "merge")
IN_SIZES = (A_WIDTH, A_KV_WIDTH, A_KV_WIDTH, A_WIDTH, B_WIDTH, B_WIDTH, B_WIDTH, B_WIDTH,
            R_WIDTH, R_WIDTH, R_WIDTH, R_WIDTH, F_WIDTH, F_WIDTH, N_BRANCH * D_MODEL)
IN_WIDTH = sum(IN_SIZES)
CTX_KV_NAMES = ("a_k", "a_v", "r_k", "r_v")

kernel_name = "hybrid_gated_branch_diffusion_trunk"


def _rmsnorm(x, g):
    xf = x.astype(jnp.float32)
    y = xf * lax.rsqrt(jnp.mean(xf * xf, axis=-1, keepdims=True) + EPS)
    return (y * g.astype(jnp.float32)).astype(x.dtype)


def _head_norm(y):
    mu = jnp.mean(y, axis=-1, keepdims=True)
    var = jnp.mean(jnp.square(y - mu), axis=-1, keepdims=True)
    return (y - mu) * lax.rsqrt(var + EPS)


def _in_proj(h, w_in, names):
    offs = np.cumsum((0,) + IN_SIZES)
    sel = [i for i, nm in enumerate(IN_NAMES) if nm in names]
    if len(sel) == len(IN_NAMES):
        w = w_in
    else:
        w = jnp.concatenate([w_in[:, int(offs[i]):int(offs[i + 1])] for i in sel], axis=1)
    p = h @ w
    sizes = [IN_SIZES[i] for i in sel]
    parts = jnp.split(p, [int(s) for s in np.cumsum(sizes)[:-1]], axis=-1)
    return {IN_NAMES[i]: part for i, part in zip(sel, parts)}


def _split_heads(t, h):
    b, n, _ = t.shape
    return t.reshape(b, n, h, HEAD_DIM)


def _axial_rope(n):
    rows = n // GRID_W
    row = jnp.broadcast_to(jnp.arange(rows)[:, None], (rows, GRID_W)).reshape(-1).astype(jnp.float32)
    col = jnp.broadcast_to(jnp.arange(GRID_W)[None, :], (rows, GRID_W)).reshape(-1).astype(jnp.float32)
    half = HEAD_DIM // 2
    inv = ROPE_BASE ** (-jnp.arange(0, half, 2, dtype=jnp.float32) / half)
    ang = jnp.stack([row[:, None] * inv, col[:, None] * inv], axis=1)
    return jnp.cos(ang), jnp.sin(ang)


def _apply_rope(t, cos, sin):
    b, n, h, d = t.shape
    t = t.reshape(b, n, h, 2, 2, d // 4)
    t1, t2 = t[..., 0, :], t[..., 1, :]
    c = cos[None, :, None].astype(t.dtype)
    s = sin[None, :, None].astype(t.dtype)
    out = jnp.stack([t1 * c - t2 * s, t2 * c + t1 * s], axis=-2)
    return out.reshape(b, n, h, d)


def _window_attention(q, k, v, kc, vc, sink):
    b, s, h, d = q.shape
    nb = s // A_BLOCK
    scale = d ** -0.5
    qb = q.reshape(b, nb, A_BLOCK, A_KV_HEADS, A_GROUP, d)
    pad = ((0, 0), (A_BLOCK, A_BLOCK), (0, 0), (0, 0))
    kp = jnp.pad(k, pad).reshape(b, nb + 2, A_BLOCK, A_KV_HEADS, d)
    vp = jnp.pad(v, pad).reshape(b, nb + 2, A_BLOCK, A_KV_HEADS, d)
    kb = jnp.concatenate([kp[:, :-2], kp[:, 1:-1], kp[:, 2:]], axis=2)
    vb = jnp.concatenate([vp[:, :-2], vp[:, 1:-1], vp[:, 2:]], axis=2)
    s_loc = jnp.einsum("bnqkgd,bnjkd->bnkgqj", qb, kb).astype(jnp.float32) * scale
    blk = jnp.arange(nb)[:, None] * A_BLOCK
    qpos = blk + jnp.arange(A_BLOCK)[None, :]
    kpos = blk - A_BLOCK + jnp.arange(3 * A_BLOCK)[None, :]
    valid = ((jnp.abs(qpos[:, :, None] - kpos[:, None, :]) <= A_WINDOW)
             & (kpos[:, None, :] >= 0) & (kpos[:, None, :] < s))
    s_loc = jnp.where(valid[None, :, None, None], s_loc, NEG_INF)
    s_ctx = jnp.einsum("bnqkgd,bjkd->bnkgqj", qb, kc).astype(jnp.float32) * scale
    sink_l = jnp.broadcast_to(sink.reshape(A_KV_HEADS, A_GROUP)[:, :, None, None].astype(jnp.float32),
                              s_loc.shape[:-1] + (1,))
    p = jax.nn.softmax(jnp.concatenate([s_loc, s_ctx, sink_l], axis=-1), axis=-1)
    nloc = 3 * A_BLOCK
    nctx = kc.shape[1]
    o = (jnp.einsum("bnkgqj,bnjkd->bnqkgd", p[..., :nloc].astype(v.dtype), vb)
         + jnp.einsum("bnkgqj,bjkd->bnqkgd", p[..., nloc:nloc + nctx].astype(v.dtype), vc))
    return o.reshape(b, s, h * d)


def _context_attention(qc, kc, vc, sink):
    b, L, h, d = qc.shape
    qg = qc.reshape(b, L, A_KV_HEADS, A_GROUP, d)
    sc = jnp.einsum("bqkgd,bjkd->bkgqj", qg, kc).astype(jnp.float32) * (d ** -0.5)
    sink_l = jnp.broadcast_to(sink.reshape(A_KV_HEADS, A_GROUP)[:, :, None, None].astype(jnp.float32),
                              sc.shape[:-1] + (1,))
    p = jax.nn.softmax(jnp.concatenate([sc, sink_l], axis=-1), axis=-1)
    o = jnp.einsum("bkgqj,bjkd->bqkgd", p[..., :L].astype(vc.dtype), vc)
    return o.reshape(b, L, h * d)


def _short_conv(u, w, bias):
    up = jnp.pad(u, ((0, 0), (1, 1), (0, 0)))
    return up[:, :-2] * w[0] + up[:, 1:-1] * w[1] + up[:, 2:] * w[2] + bias


def _retention_states(k, v, lg, r0):
    b, h, n, d = k.shape
    nc = n // R_CHUNK
    idx = jnp.arange(R_CHUNK, dtype=jnp.float32)
    kc_ = k.reshape(b, h, nc, R_CHUNK, d)
    vc_ = v.reshape(b, h, nc, R_CHUNK, d)
    zeta = jnp.exp(lg[:, None] * (R_CHUNK - 1 - idx))
    incr = jnp.einsum("bhncd,bhnce->nbhde", kc_ * zeta[None, :, None, :, None], vc_)
    g_chunk = jnp.exp(lg * R_CHUNK)[None, :, None, None]

    def step(r, u):
        return g_chunk * r + u, r

    r_fin, r_prev = lax.scan(step, r0, incr)
    return r_prev, r_fin


def _retention_outputs(q, k, v, lg, r_prev):
    b, h, n, d = q.shape
    nc = n // R_CHUNK
    idx = jnp.arange(R_CHUNK, dtype=jnp.float32)
    q_ = q.reshape(b, h, nc, R_CHUNK, d)
    k_ = k.reshape(b, h, nc, R_CHUNK, d)
    v_ = v.reshape(b, h, nc, R_CHUNK, d)
    diff = idx[:, None] - idx[None, :]
    decay = jnp.where(diff >= 0, jnp.exp(lg[:, None, None] * jnp.maximum(diff, 0.0)), 0.0)
    sc = jnp.einsum("bhncd,bhnmd->bhncm", q_, k_) * decay[None, :, None]
    inner = jnp.einsum("bhncm,bhnme->bhnce", sc, v_)
    xi = jnp.exp(lg[:, None] * (idx + 1.0))
    cross = jnp.einsum("bhncd,nbhde->bhnce", q_ * xi[None, :, None, :, None], r_prev)
    return (inner + cross).reshape(b, h, n, d)


def _bidir_retention(q, k, v, qc, kc, vc, log_gamma):
    b, h, _, d = q.shape
    r0 = jnp.zeros((b, h, d, d), jnp.float32)
    y, yc = 0.0, 0.0
    for direction in range(2):
        lg = log_gamma[direction]
        fl = (lambda t: jnp.flip(t, axis=2)) if direction == 1 else (lambda t: t)
        prev_c, state_c = _retention_states(fl(kc), fl(vc), lg, r0)
        prev_x, _ = _retention_states(fl(k), fl(v), lg, state_c)
        y = y + fl(_retention_outputs(fl(q), fl(k), fl(v), lg, prev_x))
        if qc is not None:
            yc = yc + fl(_retention_outputs(fl(qc), fl(kc), fl(vc), lg, prev_c))
    y = _head_norm(y)
    yc = _head_norm(yc) if qc is not None else None
    return y, yc


def _fourier(u):
    b, n, _ = u.shape
    ug = u.reshape(b, n, F_GROUPS, HEAD_DIM).astype(jnp.float32)
    y = jnp.fft.fft2(ug, axes=(1, 3), norm="ortho").real
    return y.reshape(b, n, F_WIDTH).astype(u.dtype)


def _merge(ys, zs, merge_logits, w_os, w_out):
    b, n, _ = merge_logits.shape
    gates = jax.nn.sigmoid(merge_logits.reshape(b, n, N_BRANCH, D_MODEL))
    total = 0.0
    for i in range(N_BRANCH):
        total = total + gates[:, :, i] * ((ys[i] * jax.nn.silu(zs[i])) @ w_os[i])
    return total @ w_out


def _to_heads_first(t):
    return jnp.transpose(t, (0, 2, 1, 3)).astype(jnp.float32)


def _from_heads_first(t, dtype):
    b, h, n, d = t.shape
    return jnp.transpose(t, (0, 2, 1, 3)).reshape(b, n, h * d).astype(dtype)


def _layer(x, xc, mod_x, mod_c, cos, sin, g_pre, g_post, w_in, sink, conv_w, conv_b,
           ret_decay, w_o_attn, w_o_conv, w_o_ret, w_o_fourier, w_out, update_ctx):
    shift_x, scale_x, gate_x = jnp.split(mod_x, 3, axis=-1)
    shift_c, scale_c, gate_c = jnp.split(mod_c, 3, axis=-1)
    hx = _rmsnorm(x, g_pre) * (1.0 + scale_x) + shift_x
    hc = _rmsnorm(xc, g_pre) * (1.0 + scale_c) + shift_c
    px = _in_proj(hx, w_in, IN_NAMES)
    pc = _in_proj(hc, w_in, IN_NAMES if update_ctx else CTX_KV_NAMES)
    w_os = (w_o_attn, w_o_conv, w_o_ret, w_o_fourier)
    k_scale = HEAD_DIM ** -0.5

    q_a = _apply_rope(_split_heads(px["a_q"], A_HEADS), cos, sin)
    k_a = _apply_rope(_split_heads(px["a_k"], A_KV_HEADS), cos, sin)
    v_a = _split_heads(px["a_v"], A_KV_HEADS)
    k_ac = _split_heads(pc["a_k"], A_KV_HEADS)
    v_ac = _split_heads(pc["a_v"], A_KV_HEADS)
    y_a = _window_attention(q_a, k_a, v_a, k_ac, v_ac, sink)

    y_b = px["b_b"] * _short_conv(px["b_c"] * px["b_u"], conv_w, conv_b)

    log_gamma = jax.nn.log_sigmoid(ret_decay.astype(jnp.float32))
    q_r = _to_heads_first(_apply_rope(_split_heads(px["r_q"], R_HEADS), cos, sin))
    k_r = _to_heads_first(_apply_rope(_split_heads(px["r_k"], R_HEADS), cos, sin)) * k_scale
    v_r = _to_heads_first(_split_heads(px["r_v"], R_HEADS))
    k_rc = _to_heads_first(_split_heads(pc["r_k"], R_HEADS)) * k_scale
    v_rc = _to_heads_first(_split_heads(pc["r_v"], R_HEADS))
    q_rc = _to_heads_first(_split_heads(pc["r_q"], R_HEADS)) if update_ctx else None
    y_r, y_rc = _bidir_retention(q_r, k_r, v_r, q_rc, k_rc, v_rc, log_gamma)
    y_r = _from_heads_first(y_r, x.dtype)

    y_f = _fourier(px["f_u"])

    y = _merge((y_a, y_b, y_r, y_f), (px["a_z"], px["b_z"], px["r_z"], px["f_z"]), px["merge"], w_os, w_out)
    x_new = x + gate_x * _rmsnorm(y, g_post)
    if not update_ctx:
        return x_new, xc

    yc_a = _context_attention(_split_heads(pc["a_q"], A_HEADS), k_ac, v_ac, sink)
    yc_b = pc["b_b"] * _short_conv(pc["b_c"] * pc["b_u"], conv_w, conv_b)
    yc_r = _from_heads_first(y_rc, xc.dtype)
    yc_f = _fourier(pc["f_u"])
    yc = _merge((yc_a, yc_b, yc_r, yc_f), (pc["a_z"], pc["b_z"], pc["r_z"], pc["f_z"]), pc["merge"], w_os, w_out)
    xc_new = xc + gate_c * _rmsnorm(yc, g_post)
    return x_new, xc_new


def setup_inputs(seed: int = 0) -> dict:
    key = jax.random.key(seed)
    ks = jax.random.split(key, 18)
    f32 = jnp.float32

    def nrm(k, shape, scale):
        return jax.random.normal(k, shape, f32) * scale

    decay_base = jnp.log(2.0 ** (5.0 + jnp.arange(R_HEADS, dtype=f32)) - 1.0)
    return {
        "x": nrm(ks[0], (BATCH, SEQ, D_MODEL), 1.0),
        "c": nrm(ks[1], (BATCH, D_MODEL), 1.0),
        "ctx": nrm(ks[2], (BATCH, CTX_LEN, D_MODEL), 1.0),
        "c_ctx": nrm(ks[3], (D_MODEL,), 1.0),
        "w_ada": nrm(ks[4], (DEPTH, D_MODEL, 3 * D_MODEL), 0.5 * D_MODEL ** -0.5),
        "b_ada": nrm(ks[5], (DEPTH, 3 * D_MODEL), 0.02),
        "norm_pre": 1.0 + nrm(ks[6], (DEPTH, D_MODEL), 0.05),
        "norm_post": 1.0 + nrm(ks[7], (DEPTH, D_MODEL), 0.05),
        "w_in": nrm(ks[8], (DEPTH, D_MODEL, IN_WIDTH), D_MODEL ** -0.5),
        "attn_sink": nrm(ks[9], (DEPTH, A_HEADS), 0.5),
        "conv_w": nrm(ks[10], (DEPTH, CONV_W, B_WIDTH), CONV_W ** -0.5),
        "conv_b": nrm(ks[11], (DEPTH, B_WIDTH), 0.02),
        "ret_decay": decay_base + nrm(ks[12], (DEPTH, 2, R_HEADS), 0.1),
        "w_o_attn": nrm(ks[13], (DEPTH, A_WIDTH, D_MODEL), A_WIDTH ** -0.5),
        "w_o_conv": nrm(ks[14], (DEPTH, B_WIDTH, D_MODEL), B_WIDTH ** -0.5),
        "w_o_ret": nrm(ks[15], (DEPTH, R_WIDTH, D_MODEL), R_WIDTH ** -0.5),
        "w_o_fourier": nrm(ks[16], (DEPTH, F_WIDTH, D_MODEL), F_WIDTH ** -0.5),
        "w_out": nrm(ks[17], (DEPTH, D_MODEL, D_MODEL), D_MODEL ** -0.5),
    }


def reference(x, c, ctx, c_ctx, w_ada, b_ada, norm_pre, norm_post, w_in, attn_sink, conv_w, conv_b,
              ret_decay, w_o_attn, w_o_conv, w_o_ret, w_o_fourier, w_out):
    n = x.shape[1]
    cos, sin = _axial_rope(n)
    silu_c = jax.nn.silu(c)
    silu_cc = jax.nn.silu(c_ctx)
    xc = ctx
    for l in range(DEPTH):
        mod_x = (silu_c @ w_ada[l] + b_ada[l])[:, None, :]
        mod_c = (silu_cc @ w_ada[l] + b_ada[l])[None, None, :]
        x, xc = _layer(x, xc, mod_x, mod_c, cos, sin, norm_pre[l], norm_post[l], w_in[l], attn_sink[l],
                       conv_w[l], conv_b[l], ret_decay[l], w_o_attn[l], w_o_conv[l], w_o_ret[l],
                       w_o_fourier[l], w_out[l], l < DEPTH - 1)
    return x
```

```python
import functools
import math

import numpy as np
import jax
import jax.numpy as jnp
from jax import lax
from jax.experimental import pallas as pl
from jax.experimental.pallas import tpu as pltpu

D_MODEL = 1024
GRID_W = 64
HEAD_DIM = 64
EPS = 1e-6
NEG_INF = -1e30
A_HEADS = 8
A_KV_HEADS = 2
A_BLOCK = 128
A_WIDTH = A_HEADS * HEAD_DIM
A_KV_WIDTH = A_KV_HEADS * HEAD_DIM
ROPE_BASE = 10000.0
B_WIDTH = 256
R_HEADS = 4
R_WIDTH = R_HEADS * HEAD_DIM
R_CHUNK = 128
F_WIDTH = 256
N_BRANCH = 4

IN_NAMES = ("a_q", "a_k", "a_v", "a_z", "b_u", "b_b", "b_c", "b_z", "r_q", "r_k", "r_v", "r_z", "f_u", "f_z", "merge")
IN_SIZES = (A_WIDTH, A_KV_WIDTH, A_KV_WIDTH, A_WIDTH, B_WIDTH, B_WIDTH, B_WIDTH, B_WIDTH,
            R_WIDTH, R_WIDTH, R_WIDTH, R_WIDTH, F_WIDTH, F_WIDTH, N_BRANCH * D_MODEL)
_OFFS = dict(zip(IN_NAMES, np.cumsum((0,) + IN_SIZES)[:-1].tolist()))
_SIZE = dict(zip(IN_NAMES, IN_SIZES))

MIX_NAMES = ("a_q", "a_k", "a_v", "b_u", "b_b", "b_c", "r_q", "r_k", "r_v", "f_u")
MIX_OFFS = dict(zip(MIX_NAMES, np.cumsum((0,) + tuple(_SIZE[n] for n in MIX_NAMES))[:-1].tolist()))
MIX_WIDTH = sum(_SIZE[n] for n in MIX_NAMES)
Z_NAMES = ("a_z", "b_z", "r_z", "f_z")
Z_SIZES = tuple(_SIZE[n] for n in Z_NAMES)
Z_OFFS = tuple(np.cumsum((0,) + Z_SIZES)[:-1].tolist())
Z_WIDTH = sum(Z_SIZES)

LANES = 128
VMEM_LIMIT = 56 * 1024 * 1024

BF16 = jnp.bfloat16
F32 = jnp.float32
HI = lax.Precision.HIGHEST


def _cparams(sem):
    return pltpu.CompilerParams(dimension_semantics=sem, vmem_limit_bytes=VMEM_LIMIT)


def _const_spec(shape):
    nd = len(shape)
    return pl.BlockSpec(shape, lambda *_: (0,) * nd)


def _sigmoid(v):
    return 1.0 / (1.0 + jnp.exp(-v))


def _ada_kernel(cv_ref, w_ref, b_ref, o_ref):
    cv = cv_ref[...]
    s = cv * _sigmoid(cv)
    o_ref[0] = jnp.dot(s, w_ref[0], preferred_element_type=F32, precision=HI) + b_ref[0]


def _ada_call(cv, w_ada, b_ada):
    depth, d, d3 = w_ada.shape
    tn = 1024
    return pl.pallas_call(
        _ada_kernel,
        grid=(depth, d3 // tn),
        in_specs=[pl.BlockSpec((8, d), lambda l, j: (0, 0)),
                  pl.BlockSpec((1, d, tn), lambda l, j: (l, 0, j)),
                  pl.BlockSpec((1, 1, tn), lambda l, j: (l, 0, j))],
        out_specs=pl.BlockSpec((1, 8, tn), lambda l, j: (l, 0, j)),
        out_shape=jax.ShapeDtypeStruct((depth, 8, d3), F32),
        compiler_params=_cparams(("arbitrary", "arbitrary")),
        name="ada_mod",
    )(cv, w_ada, b_ada.reshape(depth, 1, d3))


def _modulated_norm(x, g, mod):
    ms = jnp.mean(x * x, axis=-1, keepdims=True)
    y = x * lax.rsqrt(ms + EPS) * g
    return y * (1.0 + mod[:, D_MODEL:2 * D_MODEL]) + mod[:, 0:D_MODEL]


def _rope(t, cos, sin_signed, first_half):
    outs = []
    for j in range(t.shape[1] // LANES):
        tj = t[:, j * LANES:(j + 1) * LANES]
        partner = jnp.where(first_half, pltpu.roll(tj, LANES - 16, 1), pltpu.roll(tj, 16, 1))
        outs.append(tj * cos + partner * sin_signed)
    return outs[0] if len(outs) == 1 else jnp.concatenate(outs, axis=1)


def _inproj_kernel(x_ref, mod_ref, g_ref, w_ref, cos_ref, sin_ref, fd_ref,
                   qa_ref, ka_ref, va_ref, wc_ref, bb_ref, qr_ref, kr_ref, vr_ref, zf_ref):
    h = _modulated_norm(x_ref[...], g_ref[...], mod_ref[0]).astype(BF16)

    def proj(name):
        o = MIX_OFFS[name]
        return jnp.dot(h, w_ref[:, o:o + _SIZE[name]], preferred_element_type=F32)

    cos = cos_ref[...]
    sin = sin_ref[...]
    lane = lax.broadcasted_iota(jnp.int32, cos.shape, 1)
    first_half = (lane % 32) < 16
    k_scale = HEAD_DIM ** -0.5
    qa_ref[...] = _rope(proj("a_q"), cos, sin, first_half) * k_scale
    ka_ref[...] = _rope(proj("a_k"), cos, sin, first_half)
    va_ref[...] = proj("a_v")
    wc_ref[...] = proj("b_c") * proj("b_u")
    bb_ref[...] = proj("b_b")
    qr_ref[...] = _rope(proj("r_q"), cos, sin, first_half)
    kr_ref[...] = _rope(proj("r_k"), cos, sin, first_half) * k_scale
    vr_ref[...] = proj("r_v")
    zf_ref[...] = jnp.dot(proj("f_u").astype(BF16), fd_ref[...].astype(BF16), preferred_element_type=F32)


def _inproj_call(x2, mod3, g_pre, w_mix, cos_t, sin_t, fd, *, tm, tiles_per_group):
    rows, d = x2.shape
    nt = rows // tm
    tiles_per_seq = cos_t.shape[0] // tm
    widths = (A_WIDTH, A_KV_WIDTH, A_KV_WIDTH, B_WIDTH, B_WIDTH, R_WIDTH, R_WIDTH, R_WIDTH, 2 * F_WIDTH)
    row_spec = lambda w: pl.BlockSpec((tm, w), lambda i: (i, 0))
    return pl.pallas_call(
        _inproj_kernel,
        grid=(nt,),
        in_specs=[row_spec(d),
                  pl.BlockSpec((1, 1, 3 * d), lambda i: (i // tiles_per_group, 0, 0)),
                  _const_spec((1, d)),
                  _const_spec(w_mix.shape),
                  pl.BlockSpec((tm, LANES), lambda i: (i % tiles_per_seq, 0)),
                  pl.BlockSpec((tm, LANES), lambda i: (i % tiles_per_seq, 0)),
                  _const_spec(fd.shape)],
        out_specs=[row_spec(w) for w in widths],
        out_shape=[jax.ShapeDtypeStruct((rows, w), F32) for w in widths],
        compiler_params=_cparams(("arbitrary",)),
        name="in_proj",
    )(x2, mod3, g_pre, w_mix, cos_t, sin_t, fd)


def _attn_kernel(*refs, local):
    if local:
        sink_ref, q_ref, kp_ref, kc_ref, kn_ref, vp_ref, vc_ref, vn_ref, kx_ref, vx_ref, o_ref = refs
    else:
        sink_ref, q_ref, kx_ref, vx_ref, o_ref = refs
    blk = q_ref.shape[1]
    n = pl.program_id(1)
    last = pl.num_programs(1) - 1
    if local:
        kcat = jnp.concatenate([kp_ref[0], kc_ref[0], kn_ref[0], kx_ref[0]], axis=0)
        vcat = jnp.concatenate([vp_ref[0], vc_ref[0], vn_ref[0], vx_ref[0]], axis=0)
    else:
        kcat = kx_ref[0]
        vcat = vx_ref[0]
    nk = kcat.shape[0]
    lane_k = lax.broadcasted_iota(jnp.int32, (nk, LANES), 1)
    krol = pltpu.roll(kcat, 64, 1)
    vrol = pltpu.roll(vcat, 64, 1)
    lane_q = lax.broadcasted_iota(jnp.int32, (blk, LANES), 1)
    if local:
        row = lax.broadcasted_iota(jnp.int32, (blk, nk), 0)
        col = lax.broadcasted_iota(jnp.int32, (blk, nk), 1)
        seq_lo = blk - n * blk
        seq_hi = blk + (last - n + 1) * blk
        slack = jnp.minimum(jnp.minimum(col - row, row + 2 * blk - col),
                            jnp.minimum(col - seq_lo, seq_hi - 1 - col))
        valid = (jnp.where(col >= 3 * blk, 0, slack) >= 0)[None]
    for j in range(A_KV_HEADS):
        low = (lane_k < 64) if j == 0 else (lane_k >= 64)
        kd = jnp.where(low, kcat, krol).astype(BF16)
        vd = jnp.where(low, vcat, vrol)
        v_even = jnp.where(lane_k < 64, vd, 0.0).astype(BF16)
        v_odd = jnp.where(lane_k >= 64, vd, 0.0).astype(BF16)
        parts = []
        for pair in range(2):
            qp = q_ref[0, :, (2 * j + pair) * LANES:(2 * j + pair + 1) * LANES]
            parts.append(jnp.where(lane_q < 64, qp, 0.0))
            parts.append(jnp.where(lane_q >= 64, qp, 0.0))
        qs = jnp.concatenate(parts, axis=0).astype(BF16)
        s = lax.dot_general(qs, kd, (((1,), (1,)), ((), ())), preferred_element_type=F32)
        if local:
            s = jnp.where(valid, s.reshape(4, blk, nk), NEG_INF).reshape(4 * blk, nk)
        sink = jnp.concatenate(
            [jnp.full((blk, 1), sink_ref[4 * j + g], F32) for g in range(4)], axis=0)
        m = jnp.maximum(jnp.max(s, axis=1, keepdims=True), sink)
        p = jnp.exp(s - m)
        den = jnp.sum(p, axis=1, keepdims=True) + jnp.exp(sink - m)
        inv = 1.0 / den
        pb = p.astype(BF16)
        for pair in range(2):
            r0 = 2 * pair * blk
            o = (jnp.dot(pb[r0:r0 + blk], v_even, preferred_element_type=F32)
                 + jnp.dot(pb[r0 + blk:r0 + 2 * blk], v_odd, preferred_element_type=F32))
            scale = jnp.where(lane_q < 64, inv[r0:r0 + blk], inv[r0 + blk:r0 + 2 * blk])
            c0 = (2 * j + pair) * LANES
            o_ref[0, :, c0:c0 + LANES] = o * scale


def _attn_call(sink, q, k, v, kx, vx, *, local):
    b, sq, _ = q.shape
    blk = A_BLOCK
    nq = sq // blk
    lx = kx.shape[1]
    smem = pl.BlockSpec(memory_space=pltpu.SMEM)
    q_spec = pl.BlockSpec((1, blk, A_WIDTH), lambda bi, n: (bi, n, 0))
    x_spec = pl.BlockSpec((1, lx, A_KV_WIDTH), lambda bi, n: (bi, 0, 0))
    if local:
        prev = pl.BlockSpec((1, blk, A_KV_WIDTH), lambda bi, n: (bi, jnp.maximum(n - 1, 0), 0))
        cur = pl.BlockSpec((1, blk, A_KV_WIDTH), lambda bi, n: (bi, n, 0))
        nxt = pl.BlockSpec((1, blk, A_KV_WIDTH), lambda bi, n: (bi, jnp.minimum(n + 1, nq - 1), 0))
        in_specs = [smem, q_spec, prev, cur, nxt, prev, cur, nxt, x_spec, x_spec]
        args = (sink, q, k, k, k, v, v, v, kx, vx)
    else:
        in_specs = [smem, q_spec, x_spec, x_spec]
        args = (sink, q, kx, vx)
    return pl.pallas_call(
        functools.partial(_attn_kernel, local=local),
        grid=(b, nq),
        in_specs=in_specs,
        out_specs=pl.BlockSpec((1, blk, A_WIDTH), lambda bi, n: (bi, n, 0)),
        out_shape=jax.ShapeDtypeStruct((b, sq, A_WIDTH), F32),
        compiler_params=_cparams(("arbitrary", "arbitrary")),
        name="win_attn" if local else "ctx_attn",
    )(*args)


def _head_of(shape, dim):
    return lax.broadcasted_iota(jnp.int32, shape, dim) // HEAD_DIM


def _group_mean(t, avg):
    hi = t.astype(BF16)
    lo = (t - hi.astype(F32)).astype(BF16)
    return (jnp.dot(hi, avg, preferred_element_type=F32) + jnp.dot(lo, avg, preferred_element_type=F32))


def _ret_kernel(*refs, reverse, finalize):
    if finalize:
        lg_ref, lgl_ref, q_ref, k_ref, v_ref, r0_ref, yin_ref, y_ref, rfin_ref, r_scr = refs
    else:
        lg_ref, lgl_ref, q_ref, k_ref, v_ref, r0_ref, y_ref, rfin_ref, r_scr = refs
    c = pl.program_id(1)
    ch = q_ref.shape[1]
    w = q_ref.shape[2]

    @pl.when(c == 0)
    def _():
        r_scr[...] = r0_ref[0]

    q = q_ref[0]
    k = k_ref[0]
    v = v_ref[0]
    lgl = lgl_ref[...]
    pos = lax.broadcasted_iota(jnp.int32, (ch, 1), 0).astype(F32)
    if reverse:
        xi = jnp.exp(lgl * (ch - pos))
        zeta = jnp.exp(lgl * pos)
    else:
        xi = jnp.exp(lgl * (pos + 1.0))
        zeta = jnp.exp(lgl * (ch - 1.0 - pos))
    g_chunk = jnp.exp(lgl * float(ch))

    r = r_scr[...]
    cross = jnp.dot((q * xi).astype(BF16), r.astype(BF16), preferred_element_type=F32)

    lane_head = _head_of((ch, w), 1)
    q4 = jnp.concatenate([jnp.where(lane_head == h, q, 0.0) for h in range(R_HEADS)], axis=0).astype(BF16)
    kb = k.astype(BF16)
    sc = lax.dot_general(q4, kb, (((1,), (1,)), ((), ())), preferred_element_type=F32)
    ri = lax.broadcasted_iota(jnp.int32, (ch, ch), 0)
    ci = lax.broadcasted_iota(jnp.int32, (ch, ch), 1)
    diff = (ci - ri) if reverse else (ri - ci)
    dist = jnp.maximum(diff, 0).astype(F32)
    blocks = []
    for h in range(R_HEADS):
        decay = jnp.where(diff >= 0, jnp.exp(lg_ref[h] * dist), 0.0)
        blocks.append((sc[h * ch:(h + 1) * ch] * decay).astype(BF16))
    s4 = jnp.concatenate(blocks, axis=1)
    v4 = jnp.concatenate([jnp.where(lane_head == h, v, 0.0) for h in range(R_HEADS)], axis=0).astype(BF16)
    inner = jnp.dot(s4, v4, preferred_element_type=F32)
    y = inner + cross

    kz = (k * zeta).astype(BF16)
    ktv = lax.dot_general(kz, v.astype(BF16), (((0,), (0,)), ((), ())), preferred_element_type=F32)
    same_head = _head_of((w, w), 0) == _head_of((w, w), 1)
    r_new = g_chunk * r + jnp.where(same_head, ktv, 0.0)
    r_scr[...] = r_new

    @pl.when(c == pl.num_programs(1) - 1)
    def _():
        rfin_ref[0] = r_new

    if finalize:
        y = y + yin_ref[0]
        avg = jnp.where(same_head, 1.0 / HEAD_DIM, 0.0).astype(BF16)
        mu = _group_mean(y, avg)
        d = y - mu
        var = _group_mean(d * d, avg)
        y = d * lax.rsqrt(var + EPS)
    y_ref[0] = y


def _ret_call(lg, lgl, q, k, v, r0, y_in, *, reverse):
    b, s, w = q.shape
    ch = R_CHUNK
    nc = s // ch
    finalize = y_in is not None
    cidx = (lambda bi, c: (bi, nc - 1 - c, 0)) if reverse else (lambda bi, c: (bi, c, 0))
    chunk = pl.BlockSpec((1, ch, w), cidx)
    state = pl.BlockSpec((1, w, w), lambda bi, c: (bi, 0, 0))
    in_specs = [pl.BlockSpec(memory_space=pltpu.SMEM), pl.BlockSpec((1, w), lambda bi, c: (0, 0)),
                chunk, chunk, chunk, state]
    args = [lg, lgl, q, k, v, r0]
    if finalize:
        in_specs.append(chunk)
        args.append(y_in)
    return pl.pallas_call(
        functools.partial(_ret_kernel, reverse=reverse, finalize=finalize),
        grid=(b, nc),
        in_specs=in_specs,
        out_specs=[chunk, state],
        out_shape=[jax.ShapeDtypeStruct((b, s, w), F32), jax.ShapeDtypeStruct((b, w, w), F32)],
        scratch_shapes=[pltpu.VMEM((w, w), F32)],
        compiler_params=_cparams(("arbitrary", "arbitrary")),
        name="retention_bwd" if reverse else "retention_fwd",
    )(*args)


def _dft_cs(n, scale):
    a = 2.0 * np.pi * np.outer(np.arange(n), np.arange(n)) / n
    return np.cos(a) * scale, np.sin(a) * scale


def _fourier_rows_kernel(m_ref, z_ref, tc_ref, ts_ref, o_ref):
    a = jnp.dot(m_ref[...], z_ref[0], preferred_element_type=F32, precision=HI)
    half = a.shape[0] // 2
    a_re, a_im = a[:half], a[half:]
    tc = tc_ref[...]
    ts = ts_ref[...]
    o_ref[0, :half] = a_re * tc + a_im * ts
    o_ref[0, half:] = a_im * tc - a_re * ts


def _fourier_cols_kernel(g_ref, b_ref, o_ref):
    g = g_ref[...]
    for i in range(b_ref.shape[1]):
        o_ref[0, i] = jnp.dot(g, b_ref[0, i], preferred_element_type=F32, precision=HI)


def _fourier_latent(zf, rows):
    b, s, _ = zf.shape
    cw = GRID_W
    wide = cw * F_WIDTH
    zt = zf.reshape(b, rows, cw, 2, F_WIDTH).transpose(0, 3, 1, 2, 4).reshape(b, 2 * rows, wide)
    c_r, s_r = _dft_cs(rows, rows ** -0.5)
    m1 = jnp.asarray(np.block([[c_r, s_r], [-s_r, c_r]]), F32)
    ang = 2.0 * np.pi * np.outer(np.arange(rows), np.arange(cw)) / s
    tc = jnp.asarray(np.repeat(np.cos(ang), F_WIDTH, axis=1), F32)
    ts = jnp.asarray(np.repeat(np.sin(ang), F_WIDTH, axis=1), F32)
    tn = 2048
    bt = pl.pallas_call(
        _fourier_rows_kernel,
        grid=(wide // tn, b),
        in_specs=[_const_spec(m1.shape),
                  pl.BlockSpec((1, 2 * rows, tn), lambda j, bi: (bi, 0, j)),
                  pl.BlockSpec((rows, tn), lambda j, bi: (0, j)),
                  pl.BlockSpec((rows, tn), lambda j, bi: (0, j))],
        out_specs=pl.BlockSpec((1, 2 * rows, tn), lambda j, bi: (bi, 0, j)),
        out_shape=jax.ShapeDtypeStruct((b, 2 * rows, wide), F32),
        compiler_params=_cparams(("arbitrary", "arbitrary")),
        name="fourier_rows",
    )(m1, zt, tc, ts)
    bk = bt.reshape(b, 2, rows, cw, F_WIDTH).transpose(0, 2, 1, 3, 4).reshape(b, rows, 2 * cw, F_WIDTH)
    c_c, s_c = _dft_cs(cw, cw ** -0.5)
    g = jnp.asarray(np.concatenate([c_c, s_c], axis=1), F32)
    kt = 8
    out = pl.pallas_call(
        _fourier_cols_kernel,
        grid=(b, rows // kt),
        in_specs=[_const_spec(g.shape),
                  pl.BlockSpec((1, kt, 2 * cw, F_WIDTH), lambda bi, i: (bi, i, 0, 0))],
        out_specs=pl.BlockSpec((1, kt, cw, F_WIDTH), lambda bi, i: (bi, i, 0, 0)),
        out_shape=jax.ShapeDtypeStruct((b, rows, cw, F_WIDTH), F32),
        compiler_params=_cparams(("arbitrary", "arbitrary")),
        name="fourier_cols",
    )(g, bk)
    return out.transpose(0, 2, 1, 3).reshape(b, s, F_WIDTH)


def _fourier_dense_kernel(m_ref, z_ref, o_ref):
    z = z_ref[0]
    zz = jnp.concatenate([z[:, :F_WIDTH], z[:, F_WIDTH:]], axis=0)
    o_ref[0] = jnp.dot(m_ref[...], zz, preferred_element_type=F32, precision=HI)


def _fourier_dense(zf):
    b, n, _ = zf.shape
    c_n, s_n = _dft_cs(n, n ** -0.5)
    m = jnp.asarray(np.concatenate([c_n, s_n], axis=1), F32)
    return pl.pallas_call(
        _fourier_dense_kernel,
        grid=(b,),
        in_specs=[_const_spec(m.shape), pl.BlockSpec((1, n, 2 * F_WIDTH), lambda bi: (bi, 0, 0))],
        out_specs=pl.BlockSpec((1, n, F_WIDTH), lambda bi: (bi, 0, 0)),
        out_shape=jax.ShapeDtypeStruct((b, n, F_WIDTH), F32),
        compiler_params=_cparams(("arbitrary",)),
        name="fourier_dense",
    )(m, zf)


def _merge_kernel(x_ref, mod_ref, gpre_ref, gpost_ref, ya_ref, wc_ref, wprev_ref, wnext_ref, bb_ref,
                  yr_ref, yf_ref, cw_ref, cb_ref, wz_ref, wm_ref, wo_ref, wout_ref, o_ref, *, tiles_per_seq):
    i = pl.program_id(0)
    x = x_ref[...]
    mod = mod_ref[0]
    h = _modulated_norm(x, gpre_ref[...], mod).astype(BF16)
    tm = x.shape[0]

    wc = wc_ref[...]
    t = i % tiles_per_seq
    prev_row = jnp.where(t > 0, wprev_ref[7:8, :], 0.0)
    next_row = jnp.where(t < tiles_per_seq - 1, wnext_ref[0:1, :], 0.0)
    row = lax.broadcasted_iota(jnp.int32, wc.shape, 0)
    up = jnp.where(row == 0, prev_row, pltpu.roll(wc, 1, 0))
    dn = jnp.where(row == tm - 1, next_row, pltpu.roll(wc, tm - 1, 0))
    conv = up * cw_ref[0:1, :] + wc * cw_ref[1:2, :] + dn * cw_ref[2:3, :] + cb_ref[...]
    yb = bb_ref[...] * conv

    ys = (ya_ref[...], yb, yr_ref[...], yf_ref[...])
    total = jnp.zeros((tm, D_MODEL), F32)
    for br in range(N_BRANCH):
        zo, zw = Z_OFFS[br], Z_SIZES[br]
        z = jnp.dot(h, wz_ref[:, zo:zo + zw], preferred_element_type=F32)
        act = (ys[br] * (z * _sigmoid(z))).astype(BF16)
        proj = jnp.dot(act, wo_ref[zo:zo + zw, :], preferred_element_type=F32)
        gate = _sigmoid(jnp.dot(h, wm_ref[:, br * D_MODEL:(br + 1) * D_MODEL], preferred_element_type=F32))
        total = total + gate * proj
    y = jnp.dot(total.astype(BF16), wout_ref[...], preferred_element_type=F32)
    ms = jnp.mean(y * y, axis=-1, keepdims=True)
    yn = y * lax.rsqrt(ms + EPS) * gpost_ref[...]
    o_ref[...] = x + mod[:, 2 * D_MODEL:3 * D_MODEL] * yn


def _merge_call(x2, mod3, g_pre, g_post, ya, wc, bb, yr, yf, conv_w, conv_b, wz, wm, wo, wout,
                *, tm, tiles_per_group, tiles_per_seq):
    rows, d = x2.shape
    nt = rows // tm
    hb = tm // 8
    nhb = rows // 8
    row_spec = lambda w: pl.BlockSpec((tm, w), lambda i: (i, 0))
    return pl.pallas_call(
        functools.partial(_merge_kernel, tiles_per_seq=tiles_per_seq),
        grid=(nt,),
        in_specs=[row_spec(d),
                  pl.BlockSpec((1, 1, 3 * d), lambda i: (i // tiles_per_group, 0, 0)),
                  _const_spec((1, d)), _const_spec((1, d)),
                  row_spec(A_WIDTH),
                  row_spec(B_WIDTH),
                  pl.BlockSpec((8, B_WIDTH), lambda i: (jnp.maximum(i * hb - 1, 0), 0)),
                  pl.BlockSpec((8, B_WIDTH), lambda i: (jnp.minimum((i + 1) * hb, nhb - 1), 0)),
                  row_spec(B_WIDTH), row_spec(R_WIDTH), row_spec(F_WIDTH),
                  _const_spec(conv_w.shape), _const_spec(conv_b.shape),
                  _const_spec(wz.shape), _const_spec(wm.shape), _const_spec(wo.shape), _const_spec(wout.shape)],
        out_specs=row_spec(d),
        out_shape=jax.ShapeDtypeStruct((rows, d), F32),
        compiler_params=_cparams(("arbitrary",)),
        name="merge",
    )(x2, mod3, g_pre, g_post, ya, wc, wc, wc, bb, yr, yf, conv_w, conv_b, wz, wm, wo, wout)


def _rope_tables(n):
    rows = n // GRID_W
    row = jnp.broadcast_to(jnp.arange(rows)[:, None], (rows, GRID_W)).reshape(-1).astype(F32)
    col = jnp.broadcast_to(jnp.arange(GRID_W)[None, :], (rows, GRID_W)).reshape(-1).astype(F32)
    half = HEAD_DIM // 2
    inv = ROPE_BASE ** (-jnp.arange(0, half, 2, dtype=F32) / half)
    ang_r = row[:, None] * inv
    ang_c = col[:, None] * inv
    cos = jnp.concatenate([jnp.cos(ang_r), jnp.cos(ang_r), jnp.cos(ang_c), jnp.cos(ang_c)], axis=1)
    sin = jnp.concatenate([-jnp.sin(ang_r), jnp.sin(ang_r), -jnp.sin(ang_c), jnp.sin(ang_c)], axis=1)
    return jnp.tile(cos, (1, LANES // HEAD_DIM)), jnp.tile(sin, (1, LANES // HEAD_DIM))


def _cols(w, names):
    return jnp.concatenate([w[:, _OFFS[n]:_OFFS[n] + _SIZE[n]] for n in names], axis=1)


def kernel(x, c, ctx, c_ctx, w_ada, b_ada, norm_pre, norm_post, w_in, attn_sink, conv_w, conv_b, ret_decay,
           w_o_attn, w_o_conv, w_o_ret, w_o_fourier, w_out):
    b, s, d = x.shape
    lc = ctx.shape[1]
    depth = w_in.shape[0]
    rows_grid = s // GRID_W

    cv = jnp.zeros((8, d), F32).at[:b].set(c).at[b].set(c_ctx)
    mods = _ada_call(cv, w_ada, b_ada)

    cos_x, sin_x = _rope_tables(s)
    cos_c = jnp.ones((lc, LANES), F32)
    sin_c = jnp.zeros((lc, LANES), F32)

    c64, s64 = _dft_cs(HEAD_DIM, HEAD_DIM ** -0.5)
    eye = np.eye(F_WIDTH // HEAD_DIM)
    fd = jnp.asarray(np.concatenate([np.kron(eye, c64), -np.kron(eye, s64)], axis=1), F32)

    lg_all = jax.nn.log_sigmoid(ret_decay.astype(F32))
    zero_state = jnp.zeros((b, R_WIDTH, R_WIDTH), F32)

    tm_x = 512
    tm_m = 256
    x2 = x.reshape(b * s, d)
    xc2 = ctx.reshape(b * lc, d)
    for l in range(depth):
        update_ctx = l < depth - 1
        w_l = w_in[l]
        w_mix = _cols(w_l, MIX_NAMES).astype(BF16)
        wz = _cols(w_l, Z_NAMES).astype(BF16)
        wm = w_l[:, _OFFS["merge"]:].astype(BF16)
        wo = jnp.concatenate([w_o_attn[l], w_o_conv[l], w_o_ret[l], w_o_fourier[l]], axis=0).astype(BF16)
        wout = w_out[l].astype(BF16)
        g_pre = norm_pre[l].reshape(1, d)
        g_post = norm_post[l].reshape(1, d)
        mod_x = mods[l, :b].reshape(b, 1, 3 * d)
        mod_c = mods[l, b:b + 1].reshape(1, 1, 3 * d)
        cb = conv_b[l].reshape(1, B_WIDTH)
        lg = lg_all[l]
        lgl = jnp.repeat(lg, HEAD_DIM, axis=1)

        px = _inproj_call(x2, mod_x, g_pre, w_mix, cos_x, sin_x, fd, tm=tm_x, tiles_per_group=s // tm_x)
        pc = _inproj_call(xc2, mod_c, g_pre, w_mix, cos_c, sin_c, fd, tm=lc, tiles_per_group=b)
        qa, ka, va, wcx, bbx, qr, kr, vr, zf = [t.reshape(b, s, -1) for t in px]
        qac, kac, vac, wcc, bbc, qrc, krc, vrc, zfc = [t.reshape(b, lc, -1) for t in pc]

        ya = _attn_call(attn_sink[l], qa, ka, va, kac, vac, local=True)

        ycf, st_f = _ret_call(lg[0], lgl[0:1], qrc, krc, vrc, zero_state, None, reverse=False)
        ycr, st_b = _ret_call(lg[1], lgl[1:2], qrc, krc, vrc, zero_state, ycf, reverse=True)
        yf_, _ = _ret_call(lg[0], lgl[0:1], qr, kr, vr, st_f, None, reverse=False)
        yr, _ = _ret_call(lg[1], lgl[1:2], qr, kr, vr, st_b, yf_, reverse=True)

        yfo = _fourier_latent(zf, rows_grid)

        x2_new = _merge_call(x2, mod_x, g_pre, g_post, ya.reshape(b * s, -1), wcx.reshape(b * s, -1),
                             bbx.reshape(b * s, -1), yr.reshape(b * s, -1), yfo.reshape(b * s, -1),
                             conv_w[l], cb, wz, wm, wo, wout,
                             tm=tm_m, tiles_per_group=s // tm_m, tiles_per_seq=s // tm_m)
        if update_ctx:
            yac = _attn_call(attn_sink[l], qac, None, None, kac, vac, local=False)
            yfc = _fourier_dense(zfc)
            xc2 = _merge_call(xc2, mod_c, g_pre, g_post, yac.reshape(b * lc, -1), wcc.reshape(b * lc, -1),
                              bbc.reshape(b * lc, -1), ycr.reshape(b * lc, -1), yfc.reshape(b * lc, -1),
                              conv_w[l], cb, wz, wm, wo, wout,
                              tm=lc, tiles_per_group=b, tiles_per_seq=1)
        x2 = x2_new
    return x2.reshape(b, s, d)
```

```python
import functools
import math

import numpy as np
import jax
import jax.numpy as jnp
from jax import lax
from jax.experimental import pallas as pl
from jax.experimental.pallas import tpu as pltpu

D_MODEL = 1024
GRID_W = 64
HEAD_DIM = 64
EPS = 1e-6
NEG_INF = -1e30
A_HEADS = 8
A_KV_HEADS = 2
A_BLOCK = 128
A_WIDTH = A_HEADS * HEAD_DIM
A_KV_WIDTH = A_KV_HEADS * HEAD_DIM
ROPE_BASE = 10000.0
B_WIDTH = 256
R_HEADS = 4
R_WIDTH = R_HEADS * HEAD_DIM
R_STEP = 256
F_WIDTH = 256
N_BRANCH = 4

IN_NAMES = ("a_q", "a_k", "a_v", "a_z", "b_u", "b_b", "b_c", "b_z", "r_q", "r_k", "r_v", "r_z", "f_u", "f_z", "merge")
IN_SIZES = (A_WIDTH, A_KV_WIDTH, A_KV_WIDTH, A_WIDTH, B_WIDTH, B_WIDTH, B_WIDTH, B_WIDTH,
            R_WIDTH, R_WIDTH, R_WIDTH, R_WIDTH, F_WIDTH, F_WIDTH, N_BRANCH * D_MODEL)
_OFFS = dict(zip(IN_NAMES, np.cumsum((0,) + IN_SIZES)[:-1].tolist()))
_SIZE = dict(zip(IN_NAMES, IN_SIZES))

MIX_NAMES = ("a_q", "a_k", "a_v", "b_u", "b_b", "b_c", "r_q", "r_k", "r_v", "f_u")
MIX_OFFS = dict(zip(MIX_NAMES, np.cumsum((0,) + tuple(_SIZE[n] for n in MIX_NAMES))[:-1].tolist()))
MIX_WIDTH = sum(_SIZE[n] for n in MIX_NAMES)
Z_NAMES = ("a_z", "b_z", "r_z", "f_z")
Z_SIZES = tuple(_SIZE[n] for n in Z_NAMES)
Z_OFFS = tuple(np.cumsum((0,) + Z_SIZES)[:-1].tolist())
Z_WIDTH = sum(Z_SIZES)

LANES = 128
VMEM_LIMIT = 56 * 1024 * 1024

BF16 = jnp.bfloat16
F32 = jnp.float32
HI = lax.Precision.HIGHEST


def _cparams(sem):
    return pltpu.CompilerParams(dimension_semantics=sem, vmem_limit_bytes=VMEM_LIMIT)


def _const_spec(shape):
    nd = len(shape)
    return pl.BlockSpec(shape, lambda *_: (0,) * nd)


def _weight_spec(shape):
    nd = len(shape)
    return pl.BlockSpec(shape, lambda *_: (0,) * nd, pipeline_mode=pl.Buffered(1))


def _sigmoid(v):
    return 1.0 / (1.0 + jnp.exp(-v))


def _ada_kernel(cv_ref, w_ref, b_ref, o_ref):
    cv = cv_ref[...]
    s = cv * _sigmoid(cv)
    o_ref[0] = jnp.dot(s, w_ref[0], preferred_element_type=F32, precision=HI) + b_ref[0]


def _ada_call(cv, w_ada, b_ada):
    depth, d, d3 = w_ada.shape
    tn = 1024
    return pl.pallas_call(
        _ada_kernel,
        grid=(depth, d3 // tn),
        in_specs=[pl.BlockSpec((8, d), lambda l, j: (0, 0)),
                  pl.BlockSpec((1, d, tn), lambda l, j: (l, 0, j)),
                  pl.BlockSpec((1, 1, tn), lambda l, j: (l, 0, j))],
        out_specs=pl.BlockSpec((1, 8, tn), lambda l, j: (l, 0, j)),
        out_shape=jax.ShapeDtypeStruct((depth, 8, d3), F32),
        compiler_params=_cparams(("arbitrary", "arbitrary")),
        name="ada_mod",
    )(cv, w_ada, b_ada.reshape(depth, 1, d3))


def _modulated_norm(x, g, mod):
    ms = jnp.mean(x * x, axis=-1, keepdims=True)
    y = x * lax.rsqrt(ms + EPS) * g
    return y * (1.0 + mod[:, D_MODEL:2 * D_MODEL]) + mod[:, 0:D_MODEL]


def _rope(t, cos, sin_signed, first_half):
    outs = []
    for j in range(t.shape[1] // LANES):
        tj = t[:, j * LANES:(j + 1) * LANES]
        partner = jnp.where(first_half, pltpu.roll(tj, LANES - 16, 1), pltpu.roll(tj, 16, 1))
        outs.append(tj * cos + partner * sin_signed)
    return outs[0] if len(outs) == 1 else jnp.concatenate(outs, axis=1)


def _inproj_kernel(x_ref, mod_ref, g_ref, w_ref, cos_ref, sin_ref, fd_ref,
                   qa_ref, ka_ref, va_ref, wc_ref, bb_ref, qr_ref, kr_ref, vr_ref, zf_ref):
    h = _modulated_norm(x_ref[...], g_ref[...], mod_ref[0]).astype(BF16)

    def proj(name):
        o = MIX_OFFS[name]
        return jnp.dot(h, w_ref[:, o:o + _SIZE[name]], preferred_element_type=F32)

    cos = cos_ref[...]
    sin = sin_ref[...]
    lane = lax.broadcasted_iota(jnp.int32, cos.shape, 1)
    first_half = (lane % 32) < 16
    k_scale = HEAD_DIM ** -0.5
    qa_ref[...] = _rope(proj("a_q"), cos, sin, first_half) * k_scale
    ka_ref[...] = _rope(proj("a_k"), cos, sin, first_half)
    va_ref[...] = proj("a_v")
    wc_ref[...] = proj("b_c") * proj("b_u")
    bb_ref[...] = proj("b_b")
    qr_ref[...] = _rope(proj("r_q"), cos, sin, first_half)
    kr_ref[...] = _rope(proj("r_k"), cos, sin, first_half) * k_scale
    vr_ref[...] = proj("r_v")
    zf_ref[...] = jnp.dot(proj("f_u").astype(BF16), fd_ref[...].astype(BF16), preferred_element_type=F32)


def _inproj_call(x2, mod3, g_pre, w_mix, cos_t, sin_t, fd, *, tm, tiles_per_group):
    rows, d = x2.shape
    nt = rows // tm
    tiles_per_seq = cos_t.shape[0] // tm
    widths = (A_WIDTH, A_KV_WIDTH, A_KV_WIDTH, B_WIDTH, B_WIDTH, R_WIDTH, R_WIDTH, R_WIDTH, 2 * F_WIDTH)
    row_spec = lambda w: pl.BlockSpec((tm, w), lambda i: (i, 0))
    return pl.pallas_call(
        _inproj_kernel,
        grid=(nt,),
        in_specs=[row_spec(d),
                  pl.BlockSpec((1, 1, 3 * d), lambda i: (i // tiles_per_group, 0, 0)),
                  _const_spec((1, d)),
                  _const_spec(w_mix.shape),
                  pl.BlockSpec((tm, LANES), lambda i: (i % tiles_per_seq, 0)),
                  pl.BlockSpec((tm, LANES), lambda i: (i % tiles_per_seq, 0)),
                  _const_spec(fd.shape)],
        out_specs=[row_spec(w) for w in widths],
        out_shape=[jax.ShapeDtypeStruct((rows, w), F32) for w in widths],
        compiler_params=_cparams(("arbitrary",)),
        name="in_proj",
    )(x2, mod3, g_pre, w_mix, cos_t, sin_t, fd)


def _attn_kernel(*refs, local):
    if local:
        sink_ref, q_ref, kp_ref, kc_ref, kn_ref, vp_ref, vc_ref, vn_ref, kx_ref, vx_ref, o_ref = refs
    else:
        sink_ref, q_ref, kx_ref, vx_ref, o_ref = refs
    blk = q_ref.shape[1]
    n = pl.program_id(1)
    last = pl.num_programs(1) - 1
    if local:
        kcat = jnp.concatenate([kp_ref[0], kc_ref[0], kn_ref[0], kx_ref[0]], axis=0)
        vcat = jnp.concatenate([vp_ref[0], vc_ref[0], vn_ref[0], vx_ref[0]], axis=0)
    else:
        kcat = kx_ref[0]
        vcat = vx_ref[0]
    nk = kcat.shape[0]
    lane_k = lax.broadcasted_iota(jnp.int32, (nk, LANES), 1)
    krol = pltpu.roll(kcat, 64, 1)
    vrol = pltpu.roll(vcat, 64, 1)
    lane_q = lax.broadcasted_iota(jnp.int32, (blk, LANES), 1)
    if local:
        row = lax.broadcasted_iota(jnp.int32, (blk, nk), 0)
        col = lax.broadcasted_iota(jnp.int32, (blk, nk), 1)
        seq_lo = blk - n * blk
        seq_hi = blk + (last - n + 1) * blk
        slack = jnp.minimum(jnp.minimum(col - row, row + 2 * blk - col),
                            jnp.minimum(col - seq_lo, seq_hi - 1 - col))
        valid = (jnp.where(col >= 3 * blk, 0, slack) >= 0)[None]
    for j in range(A_KV_HEADS):
        low = (lane_k < 64) if j == 0 else (lane_k >= 64)
        kd = jnp.where(low, kcat, krol).astype(BF16)
        vd = jnp.where(low, vcat, vrol)
        v_even = jnp.where(lane_k < 64, vd, 0.0).astype(BF16)
        v_odd = jnp.where(lane_k >= 64, vd, 0.0).astype(BF16)
        parts = []
        for pair in range(2):
            qp = q_ref[0, :, (2 * j + pair) * LANES:(2 * j + pair + 1) * LANES]
            parts.append(jnp.where(lane_q < 64, qp, 0.0))
            parts.append(jnp.where(lane_q >= 64, qp, 0.0))
        qs = jnp.concatenate(parts, axis=0).astype(BF16)
        s = lax.dot_general(qs, kd, (((1,), (1,)), ((), ())), preferred_element_type=F32)
        if local:
            s = jnp.where(valid, s.reshape(4, blk, nk), NEG_INF).reshape(4 * blk, nk)
        sink = jnp.concatenate(
            [jnp.full((blk, 1), sink_ref[4 * j + g], F32) for g in range(4)], axis=0)
        m = jnp.maximum(jnp.max(s, axis=1, keepdims=True), sink)
        p = jnp.exp(s - m)
        den = jnp.sum(p, axis=1, keepdims=True) + jnp.exp(sink - m)
        inv = 1.0 / den
        pb = p.astype(BF16)
        for pair in range(2):
            r0 = 2 * pair * blk
            o = (jnp.dot(pb[r0:r0 + blk], v_even, preferred_element_type=F32)
                 + jnp.dot(pb[r0 + blk:r0 + 2 * blk], v_odd, preferred_element_type=F32))
            scale = jnp.where(lane_q < 64, inv[r0:r0 + blk], inv[r0 + blk:r0 + 2 * blk])
            c0 = (2 * j + pair) * LANES
            o_ref[0, :, c0:c0 + LANES] = o * scale


def _attn_call(sink, q, k, v, kx, vx, *, local):
    b, sq, _ = q.shape
    blk = A_BLOCK
    nq = sq // blk
    lx = kx.shape[1]
    smem = pl.BlockSpec(memory_space=pltpu.SMEM)
    q_spec = pl.BlockSpec((1, blk, A_WIDTH), lambda bi, n: (bi, n, 0))
    x_spec = pl.BlockSpec((1, lx, A_KV_WIDTH), lambda bi, n: (bi, 0, 0))
    if local:
        prev = pl.BlockSpec((1, blk, A_KV_WIDTH), lambda bi, n: (bi, jnp.maximum(n - 1, 0), 0))
        cur = pl.BlockSpec((1, blk, A_KV_WIDTH), lambda bi, n: (bi, n, 0))
        nxt = pl.BlockSpec((1, blk, A_KV_WIDTH), lambda bi, n: (bi, jnp.minimum(n + 1, nq - 1), 0))
        in_specs = [smem, q_spec, prev, cur, nxt, prev, cur, nxt, x_spec, x_spec]
        args = (sink, q, k, k, k, v, v, v, kx, vx)
    else:
        in_specs = [smem, q_spec, x_spec, x_spec]
        args = (sink, q, kx, vx)
    return pl.pallas_call(
        functools.partial(_attn_kernel, local=local),
        grid=(b, nq),
        in_specs=in_specs,
        out_specs=pl.BlockSpec((1, blk, A_WIDTH), lambda bi, n: (bi, n, 0)),
        out_shape=jax.ShapeDtypeStruct((b, sq, A_WIDTH), F32),
        compiler_params=_cparams(("arbitrary", "arbitrary")),
        name="win_attn" if local else "ctx_attn",
    )(*args)


def _head_of(shape, dim):
    return lax.broadcasted_iota(jnp.int32, shape, dim) // HEAD_DIM


def _group_mean(t, avg):
    hi = t.astype(BF16)
    lo = (t - hi.astype(F32)).astype(BF16)
    return (jnp.dot(hi, avg, preferred_element_type=F32) + jnp.dot(lo, avg, preferred_element_type=F32))


def _ret_kernel(*refs, reverse, finalize):
    if finalize:
        lg_ref, lgl_ref, q_ref, k_ref, v_ref, r0_ref, yin_ref, y_ref, rfin_ref, r_scr = refs
    else:
        lg_ref, lgl_ref, q_ref, k_ref, v_ref, r0_ref, y_ref, rfin_ref, r_scr = refs
    c = pl.program_id(0)
    nb, ch, w = q_ref.shape

    @pl.when(c == 0)
    def _():
        r_scr[...] = r0_ref[...]

    lgl = lgl_ref[...]
    pos = lax.broadcasted_iota(jnp.int32, (ch, 1), 0).astype(F32)
    if reverse:
        xi = jnp.exp(lgl * (ch - pos))
        zeta = jnp.exp(lgl * pos)
    else:
        xi = jnp.exp(lgl * (pos + 1.0))
        zeta = jnp.exp(lgl * (ch - 1.0 - pos))
    g_chunk = jnp.exp(lgl * float(ch))
    ri = lax.broadcasted_iota(jnp.int32, (ch, ch), 0)
    ci = lax.broadcasted_iota(jnp.int32, (ch, ch), 1)
    diff = (ci - ri) if reverse else (ri - ci)
    dist = jnp.maximum(diff, 0).astype(F32)
    decays = [jnp.where(diff >= 0, jnp.exp(lg_ref[h] * dist), 0.0) for h in range(R_HEADS)]
    lane_head = _head_of((ch, w), 1)
    same_head = _head_of((w, w), 0) == _head_of((w, w), 1)
    avg = jnp.where(same_head, 1.0 / HEAD_DIM, 0.0).astype(BF16)

    for bi in range(nb):
        q = q_ref[bi]
        k = k_ref[bi]
        v = v_ref[bi]
        r = r_scr[bi]
        cross = jnp.dot((q * xi).astype(BF16), r.astype(BF16), preferred_element_type=F32)
        q4 = jnp.concatenate([jnp.where(lane_head == h, q, 0.0) for h in range(R_HEADS)], axis=0).astype(BF16)
        sc = lax.dot_general(q4, k.astype(BF16), (((1,), (1,)), ((), ())), preferred_element_type=F32)
        s4 = jnp.concatenate([(sc[h * ch:(h + 1) * ch] * decays[h]).astype(BF16) for h in range(R_HEADS)], axis=1)
        v4 = jnp.concatenate([jnp.where(lane_head == h, v, 0.0) for h in range(R_HEADS)], axis=0).astype(BF16)
        y = jnp.dot(s4, v4, preferred_element_type=F32) + cross

        kz = (k * zeta).astype(BF16)
        ktv = lax.dot_general(kz, v.astype(BF16), (((0,), (0,)), ((), ())), preferred_element_type=F32)
        r_new = g_chunk * r + jnp.where(same_head, ktv, 0.0)
        r_scr[bi] = r_new
        rfin_ref[bi] = r_new

        if finalize:
            y = y + yin_ref[bi]
            mu = _group_mean(y, avg)
            d = y - mu
            var = _group_mean(d * d, avg)
            y = d * lax.rsqrt(var + EPS)
        y_ref[bi] = y


def _ret_call(lg, lgl, q, k, v, r0, y_in, *, reverse):
    b, s, w = q.shape
    ch = min(R_STEP, s)
    nc = s // ch
    finalize = y_in is not None
    cidx = (lambda c: (0, nc - 1 - c, 0)) if reverse else (lambda c: (0, c, 0))
    chunk = pl.BlockSpec((b, ch, w), cidx)
    state = pl.BlockSpec((b, w, w), lambda c: (0, 0, 0))
    in_specs = [pl.BlockSpec(memory_space=pltpu.SMEM), pl.BlockSpec((1, w), lambda c: (0, 0)),
                chunk, chunk, chunk, state]
    args = [lg, lgl, q, k, v, r0]
    if finalize:
        in_specs.append(chunk)
        args.append(y_in)
    return pl.pallas_call(
        functools.partial(_ret_kernel, reverse=reverse, finalize=finalize),
        grid=(nc,),
        in_specs=in_specs,
        out_specs=[chunk, state],
        out_shape=[jax.ShapeDtypeStruct((b, s, w), F32), jax.ShapeDtypeStruct((b, w, w), F32)],
        scratch_shapes=[pltpu.VMEM((b, w, w), F32)],
        compiler_params=_cparams(("arbitrary",)),
        name="retention_bwd" if reverse else "retention_fwd",
    )(*args)


def _dft_cs(n, scale):
    a = 2.0 * np.pi * np.outer(np.arange(n), np.arange(n)) / n
    return np.cos(a) * scale, np.sin(a) * scale


def _fourier_rows_kernel(m_ref, tc_ref, ts_ref, z_ref, o_ref):
    rows, cb = z_ref.shape[1], z_ref.shape[2]
    m = m_ref[...].astype(BF16)
    for ci in range(cb):
        z = z_ref[0, :, ci, :]
        zz = jnp.concatenate([z[:, :F_WIDTH], z[:, F_WIDTH:]], axis=0).astype(BF16)
        a = jnp.dot(m, zz, preferred_element_type=F32)
        a_re, a_im = a[:rows], a[rows:]
        tc = jnp.concatenate([tc_ref[ci]] * (F_WIDTH // LANES), axis=1)
        ts = jnp.concatenate([ts_ref[ci]] * (F_WIDTH // LANES), axis=1)
        o_ref[0, :, 0, ci, :] = a_re * tc + a_im * ts
        o_ref[0, :, 1, ci, :] = a_im * tc - a_re * ts


def _fourier_cols_kernel(g_ref, b_ref, o_ref):
    g = g_ref[...].astype(BF16)
    kt, _, cw, f = b_ref.shape[1:]
    for i in range(kt):
        bm = b_ref[0, i].reshape(2 * cw, f).astype(BF16)
        o_ref[0, :, i, :] = jnp.dot(g, bm, preferred_element_type=F32)


def _fourier_latent(zf, rows):
    b, s, _ = zf.shape
    cw = GRID_W
    c_r, s_r = _dft_cs(rows, rows ** -0.5)
    m1 = jnp.asarray(np.block([[c_r, s_r], [-s_r, c_r]]), F32)
    ang = 2.0 * np.pi * np.outer(np.arange(cw), np.arange(rows)) / s
    tc = jnp.asarray(np.repeat(np.cos(ang)[:, :, None], LANES, axis=2), F32)
    ts = jnp.asarray(np.repeat(np.sin(ang)[:, :, None], LANES, axis=2), F32)
    cb = 8
    bk = pl.pallas_call(
        _fourier_rows_kernel,
        grid=(cw // cb, b),
        in_specs=[_const_spec(m1.shape),
                  pl.BlockSpec((cb, rows, LANES), lambda j, bi: (j, 0, 0)),
                  pl.BlockSpec((cb, rows, LANES), lambda j, bi: (j, 0, 0)),
                  pl.BlockSpec((1, rows, cb, 2 * F_WIDTH), lambda j, bi: (bi, 0, j, 0))],
        out_specs=pl.BlockSpec((1, rows, 2, cb, F_WIDTH), lambda j, bi: (bi, 0, 0, j, 0)),
        out_shape=jax.ShapeDtypeStruct((b, rows, 2, cw, F_WIDTH), F32),
        compiler_params=_cparams(("arbitrary", "arbitrary")),
        name="fourier_rows",
    )(m1, tc, ts, zf.reshape(b, rows, cw, 2 * F_WIDTH))
    c_c, s_c = _dft_cs(cw, cw ** -0.5)
    g = jnp.asarray(np.concatenate([c_c, s_c], axis=1), F32)
    kt = 8
    out = pl.pallas_call(
        _fourier_cols_kernel,
        grid=(b, rows // kt),
        in_specs=[_const_spec(g.shape),
                  pl.BlockSpec((1, kt, 2, cw, F_WIDTH), lambda bi, i: (bi, i, 0, 0, 0))],
        out_specs=pl.BlockSpec((1, cw, kt, F_WIDTH), lambda bi, i: (bi, 0, i, 0)),
        out_shape=jax.ShapeDtypeStruct((b, cw, rows, F_WIDTH), F32),
        compiler_params=_cparams(("arbitrary", "arbitrary")),
        name="fourier_cols",
    )(g, bk)
    return out.reshape(b, s, F_WIDTH)


def _fourier_dense_kernel(m_ref, z_ref, o_ref):
    z = z_ref[0]
    zz = jnp.concatenate([z[:, :F_WIDTH], z[:, F_WIDTH:]], axis=0)
    o_ref[0] = jnp.dot(m_ref[...], zz, preferred_element_type=F32, precision=HI)


def _fourier_dense(zf):
    b, n, _ = zf.shape
    c_n, s_n = _dft_cs(n, n ** -0.5)
    m = jnp.asarray(np.concatenate([c_n, s_n], axis=1), F32)
    return pl.pallas_call(
        _fourier_dense_kernel,
        grid=(b,),
        in_specs=[_const_spec(m.shape), pl.BlockSpec((1, n, 2 * F_WIDTH), lambda bi: (bi, 0, 0))],
        out_specs=pl.BlockSpec((1, n, F_WIDTH), lambda bi: (bi, 0, 0)),
        out_shape=jax.ShapeDtypeStruct((b, n, F_WIDTH), F32),
        compiler_params=_cparams(("arbitrary",)),
        name="fourier_dense",
    )(m, zf)


def _merge_kernel(x_ref, mod_ref, gpre_ref, gpost_ref, ya_ref, wc_ref, wprev_ref, wnext_ref, bb_ref,
                  yr_ref, yf_ref, cw_ref, cb_ref, wz_ref, wm_ref, wo_ref, wout_ref, o_ref, *, tiles_per_seq):
    i = pl.program_id(0)
    x = x_ref[...]
    mod = mod_ref[0]
    h = _modulated_norm(x, gpre_ref[...], mod).astype(BF16)
    tm = x.shape[0]

    wc = wc_ref[...]
    t = i % tiles_per_seq
    prev_row = jnp.where(t > 0, wprev_ref[7:8, :], 0.0)
    next_row = jnp.where(t < tiles_per_seq - 1, wnext_ref[0:1, :], 0.0)
    row = lax.broadcasted_iota(jnp.int32, wc.shape, 0)
    up = jnp.where(row == 0, prev_row, pltpu.roll(wc, 1, 0))
    dn = jnp.where(row == tm - 1, next_row, pltpu.roll(wc, tm - 1, 0))
    conv = up * cw_ref[0:1, :] + wc * cw_ref[1:2, :] + dn * cw_ref[2:3, :] + cb_ref[...]
    yb = bb_ref[...] * conv

    ys = (ya_ref[...], yb, yr_ref[...], yf_ref[...])
    total = jnp.zeros((tm, D_MODEL), F32)
    for br in range(N_BRANCH):
        zo, zw = Z_OFFS[br], Z_SIZES[br]
        z = jnp.dot(h, wz_ref[:, zo:zo + zw], preferred_element_type=F32)
        act = (ys[br] * (z * _sigmoid(z))).astype(BF16)
        proj = jnp.dot(act, wo_ref[zo:zo + zw, :], preferred_element_type=F32)
        gate = _sigmoid(jnp.dot(h, wm_ref[:, br * D_MODEL:(br + 1) * D_MODEL], preferred_element_type=F32))
        total = total + gate * proj
    y = jnp.dot(total.astype(BF16), wout_ref[...], preferred_element_type=F32)
    ms = jnp.mean(y * y, axis=-1, keepdims=True)
    yn = y * lax.rsqrt(ms + EPS) * gpost_ref[...]
    o_ref[...] = x + mod[:, 2 * D_MODEL:3 * D_MODEL] * yn


def _merge_call(x2, mod3, g_pre, g_post, ya, wc, bb, yr, yf, conv_w, conv_b, wz, wm, wo, wout,
                *, tm, tiles_per_group, tiles_per_seq):
    rows, d = x2.shape
    nt = rows // tm
    hb = tm // 8
    nhb = rows // 8
    row_spec = lambda w: pl.BlockSpec((tm, w), lambda i: (i, 0))
    return pl.pallas_call(
        functools.partial(_merge_kernel, tiles_per_seq=tiles_per_seq),
        grid=(nt,),
        in_specs=[row_spec(d),
                  pl.BlockSpec((1, 1, 3 * d), lambda i: (i // tiles_per_group, 0, 0)),
                  _const_spec((1, d)), _const_spec((1, d)),
                  row_spec(A_WIDTH),
                  row_spec(B_WIDTH),
                  pl.BlockSpec((8, B_WIDTH), lambda i: (jnp.maximum(i * hb - 1, 0), 0)),
                  pl.BlockSpec((8, B_WIDTH), lambda i: (jnp.minimum((i + 1) * hb, nhb - 1), 0)),
                  row_spec(B_WIDTH), row_spec(R_WIDTH), row_spec(F_WIDTH),
                  _const_spec(conv_w.shape), _const_spec(conv_b.shape),
                  _weight_spec(wz.shape), _weight_spec(wm.shape), _weight_spec(wo.shape),
                  _weight_spec(wout.shape)],
        out_specs=row_spec(d),
        out_shape=jax.ShapeDtypeStruct((rows, d), F32),
        compiler_params=_cparams(("arbitrary",)),
        name="merge",
    )(x2, mod3, g_pre, g_post, ya, wc, wc, wc, bb, yr, yf, conv_w, conv_b, wz, wm, wo, wout)


def _rope_tables(n):
    rows = n // GRID_W
    row = jnp.broadcast_to(jnp.arange(rows)[:, None], (rows, GRID_W)).reshape(-1).astype(F32)
    col = jnp.broadcast_to(jnp.arange(GRID_W)[None, :], (rows, GRID_W)).reshape(-1).astype(F32)
    half = HEAD_DIM // 2
    inv = ROPE_BASE ** (-jnp.arange(0, half, 2, dtype=F32) / half)
    ang_r = row[:, None] * inv
    ang_c = col[:, None] * inv
    cos = jnp.concatenate([jnp.cos(ang_r), jnp.cos(ang_r), jnp.cos(ang_c), jnp.cos(ang_c)], axis=1)
    sin = jnp.concatenate([-jnp.sin(ang_r), jnp.sin(ang_r), -jnp.sin(ang_c), jnp.sin(ang_c)], axis=1)
    return jnp.tile(cos, (1, LANES // HEAD_DIM)), jnp.tile(sin, (1, LANES // HEAD_DIM))


def _cols(w, names):
    return jnp.concatenate([w[:, _OFFS[n]:_OFFS[n] + _SIZE[n]] for n in names], axis=1)


def kernel(x, c, ctx, c_ctx, w_ada, b_ada, norm_pre, norm_post, w_in, attn_sink, conv_w, conv_b, ret_decay,
           w_o_attn, w_o_conv, w_o_ret, w_o_fourier, w_out):
    b, s, d = x.shape
    lc = ctx.shape[1]
    depth = w_in.shape[0]
    rows_grid = s // GRID_W

    cv = jnp.zeros((8, d), F32).at[:b].set(c).at[b].set(c_ctx)
    mods = _ada_call(cv, w_ada, b_ada)

    cos_x, sin_x = _rope_tables(s)
    cos_c = jnp.ones((lc, LANES), F32)
    sin_c = jnp.zeros((lc, LANES), F32)

    c64, s64 = _dft_cs(HEAD_DIM, HEAD_DIM ** -0.5)
    eye = np.eye(F_WIDTH // HEAD_DIM)
    fd = jnp.asarray(np.concatenate([np.kron(eye, c64), -np.kron(eye, s64)], axis=1), F32)

    lg_all = jax.nn.log_sigmoid(ret_decay.astype(F32))
    zero_state = jnp.zeros((b, R_WIDTH, R_WIDTH), F32)

    tm_x = 512
    tm_m = 512
    x2 = x.reshape(b * s, d)
    xc2 = ctx.reshape(b * lc, d)
    for l in range(depth):
        update_ctx = l < depth - 1
        w_l = w_in[l]
        w_mix = _cols(w_l, MIX_NAMES).astype(BF16)
        wz = _cols(w_l, Z_NAMES).astype(BF16)
        wm = w_l[:, _OFFS["merge"]:].astype(BF16)
        wo = jnp.concatenate([w_o_attn[l], w_o_conv[l], w_o_ret[l], w_o_fourier[l]], axis=0).astype(BF16)
        wout = w_out[l].astype(BF16)
        g_pre = norm_pre[l].reshape(1, d)
        g_post = norm_post[l].reshape(1, d)
        mod_x = mods[l, :b].reshape(b, 1, 3 * d)
        mod_c = mods[l, b:b + 1].reshape(1, 1, 3 * d)
        cb = conv_b[l].reshape(1, B_WIDTH)
        lg = lg_all[l]
        lgl = jnp.repeat(lg, HEAD_DIM, axis=1)

        px = _inproj_call(x2, mod_x, g_pre, w_mix, cos_x, sin_x, fd, tm=tm_x, tiles_per_group=s // tm_x)
        pc = _inproj_call(xc2, mod_c, g_pre, w_mix, cos_c, sin_c, fd, tm=lc, tiles_per_group=b)
        qa, ka, va, wcx, bbx, qr, kr, vr, zf = [t.reshape(b, s, -1) for t in px]
        qac, kac, vac, wcc, bbc, qrc, krc, vrc, zfc = [t.reshape(b, lc, -1) for t in pc]

        ya = _attn_call(attn_sink[l], qa, ka, va, kac, vac, local=True)

        ycf, st_f = _ret_call(lg[0], lgl[0:1], qrc, krc, vrc, zero_state, None, reverse=False)
        ycr, st_b = _ret_call(lg[1], lgl[1:2], qrc, krc, vrc, zero_state, ycf, reverse=True)
        yf_, _ = _ret_call(lg[0], lgl[0:1], qr, kr, vr, st_f, None, reverse=False)
        yr, _ = _ret_call(lg[1], lgl[1:2], qr, kr, vr, st_b, yf_, reverse=True)

        yfo = _fourier_latent(zf, rows_grid)

        x2_new = _merge_call(x2, mod_x, g_pre, g_post, ya.reshape(b * s, -1), wcx.reshape(b * s, -1),
                             bbx.reshape(b * s, -1), yr.reshape(b * s, -1), yfo.reshape(b * s, -1),
                             conv_w[l], cb, wz, wm, wo, wout,
                             tm=tm_m, tiles_per_group=s // tm_m, tiles_per_seq=s // tm_m)
        if update_ctx:
            yac = _attn_call(attn_sink[l], qac, None, None, kac, vac, local=False)
            yfc = _fourier_dense(zfc)
            xc2 = _merge_call(xc2, mod_c, g_pre, g_post, yac.reshape(b * lc, -1), wcc.reshape(b * lc, -1),
                              bbc.reshape(b * lc, -1), ycr.reshape(b * lc, -1), yfc.reshape(b * lc, -1),
                              conv_w[l], cb, wz, wm, wo, wout,
                              tm=lc, tiles_per_group=b, tiles_per_seq=1)
        x2 = x2_new
    return x2.reshape(b, s, d)
```

```python
import functools
import math

import numpy as np
import jax
import jax.numpy as jnp
from jax import lax
from jax.experimental import pallas as pl
from jax.experimental.pallas import tpu as pltpu

D_MODEL = 1024
GRID_W = 64
HEAD_DIM = 64
EPS = 1e-6
NEG_INF = -1e30
A_HEADS = 8
A_KV_HEADS = 2
A_BLOCK = 128
A_WIDTH = A_HEADS * HEAD_DIM
A_KV_WIDTH = A_KV_HEADS * HEAD_DIM
ROPE_BASE = 10000.0
B_WIDTH = 256
R_HEADS = 4
R_WIDTH = R_HEADS * HEAD_DIM
R_STEP = 256
F_WIDTH = 256
N_BRANCH = 4

IN_NAMES = ("a_q", "a_k", "a_v", "a_z", "b_u", "b_b", "b_c", "b_z", "r_q", "r_k", "r_v", "r_z", "f_u", "f_z", "merge")
IN_SIZES = (A_WIDTH, A_KV_WIDTH, A_KV_WIDTH, A_WIDTH, B_WIDTH, B_WIDTH, B_WIDTH, B_WIDTH,
            R_WIDTH, R_WIDTH, R_WIDTH, R_WIDTH, F_WIDTH, F_WIDTH, N_BRANCH * D_MODEL)
_OFFS = dict(zip(IN_NAMES, np.cumsum((0,) + IN_SIZES)[:-1].tolist()))
_SIZE = dict(zip(IN_NAMES, IN_SIZES))

MIX_NAMES = ("a_q", "a_k", "a_v", "b_u", "b_b", "b_c", "r_q", "r_k", "r_v", "f_u")
MIX_SIZE = dict({n: _SIZE[n] for n in MIX_NAMES}, a_k=2 * A_KV_WIDTH, a_v=2 * A_KV_WIDTH)
MIX_OFFS = dict(zip(MIX_NAMES, np.cumsum((0,) + tuple(MIX_SIZE[n] for n in MIX_NAMES))[:-1].tolist()))
Z_NAMES = ("a_z", "b_z", "r_z", "f_z")
Z_SIZES = tuple(_SIZE[n] for n in Z_NAMES)
Z_OFFS = tuple(np.cumsum((0,) + Z_SIZES)[:-1].tolist())
Z_WIDTH = sum(Z_SIZES)

ATTN_SUB = 4
LOG2E = math.log2(math.e)
LANES = 128
VMEM_LIMIT = 56 * 1024 * 1024

BF16 = jnp.bfloat16
F32 = jnp.float32
HI = lax.Precision.HIGHEST


def _cparams(sem):
    return pltpu.CompilerParams(dimension_semantics=sem, vmem_limit_bytes=VMEM_LIMIT)


def _const_spec(shape):
    nd = len(shape)
    return pl.BlockSpec(shape, lambda *_: (0,) * nd)


def _weight_spec(shape):
    nd = len(shape)
    return pl.BlockSpec(shape, lambda *_: (0,) * nd, pipeline_mode=pl.Buffered(1))


def _sigmoid(v):
    return 1.0 / (1.0 + jnp.exp(-v))


def _ada_kernel(cv_ref, w_ref, b_ref, o_ref):
    cv = cv_ref[...]
    s = cv * _sigmoid(cv)
    o_ref[0] = jnp.dot(s, w_ref[0], preferred_element_type=F32, precision=HI) + b_ref[0]


def _ada_call(cv, w_ada, b_ada):
    depth, d, d3 = w_ada.shape
    tn = 1024
    return pl.pallas_call(
        _ada_kernel,
        grid=(depth, d3 // tn),
        in_specs=[pl.BlockSpec((8, d), lambda l, j: (0, 0)),
                  pl.BlockSpec((1, d, tn), lambda l, j: (l, 0, j)),
                  pl.BlockSpec((1, 1, tn), lambda l, j: (l, 0, j))],
        out_specs=pl.BlockSpec((1, 8, tn), lambda l, j: (l, 0, j)),
        out_shape=jax.ShapeDtypeStruct((depth, 8, d3), F32),
        compiler_params=_cparams(("arbitrary", "arbitrary")),
        name="ada_mod",
    )(cv, w_ada, b_ada.reshape(depth, 1, d3))


def _modulated_norm(x, g, mod):
    ms = jnp.mean(x * x, axis=-1, keepdims=True)
    y = x * lax.rsqrt(ms + EPS) * g
    return y * (1.0 + mod[:, D_MODEL:2 * D_MODEL]) + mod[:, 0:D_MODEL]


def _rope(t, cos, sin_signed, first_half):
    outs = []
    for j in range(t.shape[1] // LANES):
        tj = t[:, j * LANES:(j + 1) * LANES]
        partner = jnp.where(first_half, pltpu.roll(tj, LANES - 16, 1), pltpu.roll(tj, 16, 1))
        outs.append(tj * cos + partner * sin_signed)
    return outs[0] if len(outs) == 1 else jnp.concatenate(outs, axis=1)


def _inproj_kernel(x_ref, mod_ref, g_ref, w_ref, cos_ref, sin_ref, fd_ref,
                   qa_ref, ka_ref, va_ref, wc_ref, bb_ref, qr_ref, kr_ref, vr_ref, zf_ref):
    h = _modulated_norm(x_ref[...], g_ref[...], mod_ref[0]).astype(BF16)

    def proj(name):
        o = MIX_OFFS[name]
        return jnp.dot(h, w_ref[:, o:o + MIX_SIZE[name]], preferred_element_type=F32)

    cos = cos_ref[...]
    sin = sin_ref[...]
    lane = lax.broadcasted_iota(jnp.int32, cos.shape, 1)
    first_half = (lane % 32) < 16
    low_half = lane < HEAD_DIM
    k_scale = HEAD_DIM ** -0.5
    q = _rope(proj("a_q"), cos, sin, first_half) * (k_scale * LOG2E)
    for p in range(A_WIDTH // LANES):
        qp = q[:, p * LANES:(p + 1) * LANES]
        qa_ref[:, (2 * p) * LANES:(2 * p + 1) * LANES] = jnp.where(low_half, qp, 0.0).astype(BF16)
        qa_ref[:, (2 * p + 1) * LANES:(2 * p + 2) * LANES] = jnp.where(low_half, 0.0, qp).astype(BF16)
    ka_ref[...] = _rope(proj("a_k"), cos, sin, first_half).astype(BF16)
    v = proj("a_v")
    for j in range(A_KV_HEADS):
        vj = v[:, j * LANES:(j + 1) * LANES]
        va_ref[:, (2 * j) * LANES:(2 * j + 1) * LANES] = jnp.where(low_half, vj, 1.0).astype(BF16)
        va_ref[:, (2 * j + 1) * LANES:(2 * j + 2) * LANES] = jnp.where(low_half, 1.0, vj).astype(BF16)
    wc_ref[...] = proj("b_c") * proj("b_u")
    bb_ref[...] = proj("b_b")
    qr_ref[...] = _rope(proj("r_q"), cos, sin, first_half)
    kr_ref[...] = _rope(proj("r_k"), cos, sin, first_half) * k_scale
    vr_ref[...] = proj("r_v")
    zf_ref[...] = jnp.dot(proj("f_u").astype(BF16), fd_ref[...].astype(BF16), preferred_element_type=F32)


def _inproj_call(x2, mod3, g_pre, w_mix, cos_t, sin_t, fd, *, tm, tiles_per_group):
    rows, d = x2.shape
    nt = rows // tm
    tiles_per_seq = cos_t.shape[0] // tm
    widths = (2 * A_WIDTH, 2 * A_KV_WIDTH, 4 * A_KV_WIDTH, B_WIDTH, B_WIDTH, R_WIDTH, R_WIDTH, R_WIDTH, 2 * F_WIDTH)
    dtypes = (BF16, BF16, BF16, F32, F32, F32, F32, F32, F32)
    row_spec = lambda w: pl.BlockSpec((tm, w), lambda i: (i, 0))
    return pl.pallas_call(
        _inproj_kernel,
        grid=(nt,),
        in_specs=[row_spec(d),
                  pl.BlockSpec((1, 1, 3 * d), lambda i: (i // tiles_per_group, 0, 0)),
                  _const_spec((1, d)),
                  _const_spec(w_mix.shape),
                  pl.BlockSpec((tm, LANES), lambda i: (i % tiles_per_seq, 0)),
                  pl.BlockSpec((tm, LANES), lambda i: (i % tiles_per_seq, 0)),
                  _const_spec(fd.shape)],
        out_specs=[row_spec(w) for w in widths],
        out_shape=[jax.ShapeDtypeStruct((rows, w), dt) for w, dt in zip(widths, dtypes)],
        compiler_params=_cparams(("arbitrary",)),
        name="in_proj",
    )(x2, mod3, g_pre, w_mix, cos_t, sin_t, fd)


def _attn_kernel(*refs, local):
    if local:
        (sink_ref, q_ref, kp_ref, kc_ref, kn_ref, vp_ref, vc_ref, vn_ref, kx_ref, vx_ref, o_ref,
         kwin, vwin) = refs
    else:
        sink_ref, q_ref, kx_ref, vx_ref, o_ref = refs
    blk = A_BLOCK
    nsub = q_ref.shape[1] // blk
    n = pl.program_id(1)
    last = pl.num_programs(1) * nsub - 1
    contract_last = (((1,), (1,)), ((), ()))

    lane_q = lax.broadcasted_iota(jnp.int32, (blk, LANES), 1)
    nloc = 3 * blk
    if local:
        step = nsub * blk
        kwin[0:blk] = kp_ref[0]
        kwin[blk:blk + step] = kc_ref[0]
        kwin[blk + step:2 * blk + step] = kn_ref[0]
        vwin[0:blk] = vp_ref[0]
        vwin[blk:blk + step] = vc_ref[0]
        vwin[blk + step:2 * blk + step] = vn_ref[0]
        row = lax.broadcasted_iota(jnp.int32, (blk, nloc), 0)
        col = lax.broadcasted_iota(jnp.int32, (blk, nloc), 1)
        band = jnp.minimum(col - row, row + 2 * blk - col)

    for sub in range(nsub):
        q0 = sub * blk
        if local:
            g_blk = n * nsub + sub
            seq_lo = blk - g_blk * blk
            seq_hi = blk + (last - g_blk + 1) * blk
            valid = (jnp.minimum(band, jnp.minimum(col - seq_lo, seq_hi - 1 - col)) >= 0)[None]
        for j in range(A_KV_HEADS):
            kcol = slice(j * LANES, (j + 1) * LANES)
            ecol = slice(2 * j * LANES, (2 * j + 1) * LANES)
            ocol = slice((2 * j + 1) * LANES, (2 * j + 2) * LANES)
            qs = jnp.concatenate([q_ref[0, q0:q0 + blk, (4 * j + g) * LANES:(4 * j + g + 1) * LANES]
                                  for g in range(4)], axis=0)
            s = lax.dot_general(qs, kx_ref[0, :, kcol], contract_last, preferred_element_type=F32)
            if local:
                s_loc = lax.dot_general(qs, kwin[q0:q0 + nloc, kcol], contract_last, preferred_element_type=F32)
                s_loc = jnp.where(valid, s_loc.reshape(4, blk, nloc), NEG_INF).reshape(4 * blk, nloc)
                s = jnp.concatenate([s_loc, s], axis=1)
            sink = jnp.concatenate(
                [jnp.full((blk, 1), sink_ref[4 * j + g] * LOG2E, F32) for g in range(4)], axis=0)
            m = jnp.maximum(jnp.max(s, axis=1, keepdims=True), sink)
            pb = jnp.exp2(s - m).astype(BF16)
            e_sink = jnp.exp2(sink - m)

            def weighted(prow, vcol):
                if not local:
                    return jnp.dot(pb[prow], vx_ref[0, :, vcol], preferred_element_type=F32)
                return (jnp.dot(pb[prow, :nloc], vwin[q0:q0 + nloc, vcol], preferred_element_type=F32)
                        + jnp.dot(pb[prow, nloc:], vx_ref[0, :, vcol], preferred_element_type=F32))

            for pair in range(2):
                r0 = 2 * pair * blk
                o_even = weighted(slice(r0, r0 + blk), ecol)
                o_odd = weighted(slice(r0 + blk, r0 + 2 * blk), ocol)
                num = jnp.where(lane_q < 64, o_even, o_odd)
                den = (pltpu.roll(jnp.where(lane_q < 64, o_odd, o_even), 64, 1)
                       + jnp.where(lane_q < 64, e_sink[r0:r0 + blk], e_sink[r0 + blk:r0 + 2 * blk]))
                c0 = (2 * j + pair) * LANES
                o_ref[0, q0:q0 + blk, c0:c0 + LANES] = num / den


def _attn_call(sink, q, k, v, kx, vx, *, local):
    b, sq, _ = q.shape
    blk = A_BLOCK
    nq = sq // blk
    nsub = min(ATTN_SUB, nq)
    step = nsub * blk
    lx = kx.shape[1]
    smem = pl.BlockSpec(memory_space=pltpu.SMEM)
    kw, vw = kx.shape[2], vx.shape[2]
    q_spec = pl.BlockSpec((1, step, q.shape[2]), lambda bi, n: (bi, n, 0))
    if local:
        def halo(w):
            return (pl.BlockSpec((1, blk, w), lambda bi, n: (bi, jnp.maximum(n * nsub - 1, 0), 0)),
                    pl.BlockSpec((1, step, w), lambda bi, n: (bi, n, 0)),
                    pl.BlockSpec((1, blk, w), lambda bi, n: (bi, jnp.minimum((n + 1) * nsub, nq - 1), 0)))
        in_specs = [smem, q_spec, *halo(kw), *halo(vw),
                    pl.BlockSpec((1, lx, kw), lambda bi, n: (bi, 0, 0)),
                    pl.BlockSpec((1, lx, vw), lambda bi, n: (bi, 0, 0))]
        args = (sink, q, k, k, k, v, v, v, kx, vx)
        scratch = [pltpu.VMEM((step + 2 * blk, kw), k.dtype), pltpu.VMEM((step + 2 * blk, vw), v.dtype)]
    else:
        in_specs = [smem, q_spec, pl.BlockSpec((1, lx, kw), lambda bi, n: (bi, 0, 0)),
                    pl.BlockSpec((1, lx, vw), lambda bi, n: (bi, 0, 0))]
        args = (sink, q, kx, vx)
        scratch = []
    return pl.pallas_call(
        functools.partial(_attn_kernel, local=local),
        grid=(b, nq // nsub),
        in_specs=in_specs,
        out_specs=pl.BlockSpec((1, step, A_WIDTH), lambda bi, n: (bi, n, 0)),
        out_shape=jax.ShapeDtypeStruct((b, sq, A_WIDTH), F32),
        scratch_shapes=scratch,
        compiler_params=_cparams(("arbitrary", "arbitrary")),
        name="win_attn" if local else "ctx_attn",
    )(*args)


def _head_of(shape, dim):
    return lax.broadcasted_iota(jnp.int32, shape, dim) // HEAD_DIM


def _group_mean(t, avg):
    hi = t.astype(BF16)
    lo = (t - hi.astype(F32)).astype(BF16)
    return (jnp.dot(hi, avg, preferred_element_type=F32) + jnp.dot(lo, avg, preferred_element_type=F32))


def _ret_kernel(*refs, reverse, finalize):
    if finalize:
        lg_ref, lgl_ref, q_ref, k_ref, v_ref, r0_ref, yin_ref, y_ref, rfin_ref, r_scr = refs
    else:
        lg_ref, lgl_ref, q_ref, k_ref, v_ref, r0_ref, y_ref, rfin_ref, r_scr = refs
    c = pl.program_id(0)
    nb, ch, w = q_ref.shape

    @pl.when(c == 0)
    def _():
        r_scr[...] = r0_ref[...]

    lgl = lgl_ref[...]
    pos = lax.broadcasted_iota(jnp.int32, (ch, 1), 0).astype(F32)
    if reverse:
        xi = jnp.exp(lgl * (ch - pos))
        zeta = jnp.exp(lgl * pos)
    else:
        xi = jnp.exp(lgl * (pos + 1.0))
        zeta = jnp.exp(lgl * (ch - 1.0 - pos))
    g_chunk = jnp.exp(lgl * float(ch))
    ri = lax.broadcasted_iota(jnp.int32, (ch, ch), 0)
    ci = lax.broadcasted_iota(jnp.int32, (ch, ch), 1)
    diff = (ci - ri) if reverse else (ri - ci)
    dist = jnp.maximum(diff, 0).astype(F32)
    decays = [jnp.where(diff >= 0, jnp.exp(lg_ref[h] * dist), 0.0) for h in range(R_HEADS)]
    lane_head = _head_of((ch, w), 1)
    same_head = _head_of((w, w), 0) == _head_of((w, w), 1)
    avg = jnp.where(same_head, 1.0 / HEAD_DIM, 0.0).astype(BF16)

    for bi in range(nb):
        q = q_ref[bi]
        k = k_ref[bi]
        v = v_ref[bi]
        r = r_scr[bi]
        cross = jnp.dot((q * xi).astype(BF16), r.astype(BF16), preferred_element_type=F32)
        q4 = jnp.concatenate([jnp.where(lane_head == h, q, 0.0) for h in range(R_HEADS)], axis=0).astype(BF16)
        sc = lax.dot_general(q4, k.astype(BF16), (((1,), (1,)), ((), ())), preferred_element_type=F32)
        s4 = jnp.concatenate([(sc[h * ch:(h + 1) * ch] * decays[h]).astype(BF16) for h in range(R_HEADS)], axis=1)
        v4 = jnp.concatenate([jnp.where(lane_head == h, v, 0.0) for h in range(R_HEADS)], axis=0).astype(BF16)
        y = jnp.dot(s4, v4, preferred_element_type=F32) + cross

        kz = (k * zeta).astype(BF16)
        ktv = lax.dot_general(kz, v.astype(BF16), (((0,), (0,)), ((), ())), preferred_element_type=F32)
        r_new = g_chunk * r + jnp.where(same_head, ktv, 0.0)
        r_scr[bi] = r_new
        rfin_ref[bi] = r_new

        if finalize:
            y = y + yin_ref[bi]
            mu = _group_mean(y, avg)
            d = y - mu
            var = _group_mean(d * d, avg)
            y = d * lax.rsqrt(var + EPS)
        y_ref[bi] = y


def _ret_call(lg, lgl, q, k, v, r0, y_in, *, reverse):
    b, s, w = q.shape
    ch = min(R_STEP, s)
    nc = s // ch
    finalize = y_in is not None
    cidx = (lambda c: (0, nc - 1 - c, 0)) if reverse else (lambda c: (0, c, 0))
    chunk = pl.BlockSpec((b, ch, w), cidx)
    state = pl.BlockSpec((b, w, w), lambda c: (0, 0, 0))
    in_specs = [pl.BlockSpec(memory_space=pltpu.SMEM), pl.BlockSpec((1, w), lambda c: (0, 0)),
                chunk, chunk, chunk, state]
    args = [lg, lgl, q, k, v, r0]
    if finalize:
        in_specs.append(chunk)
        args.append(y_in)
    return pl.pallas_call(
        functools.partial(_ret_kernel, reverse=reverse, finalize=finalize),
        grid=(nc,),
        in_specs=in_specs,
        out_specs=[chunk, state],
        out_shape=[jax.ShapeDtypeStruct((b, s, w), F32), jax.ShapeDtypeStruct((b, w, w), F32)],
        scratch_shapes=[pltpu.VMEM((b, w, w), F32)],
        compiler_params=_cparams(("arbitrary",)),
        name="retention_bwd" if reverse else "retention_fwd",
    )(*args)


def _dft_cs(n, scale):
    a = 2.0 * np.pi * np.outer(np.arange(n), np.arange(n)) / n
    return np.cos(a) * scale, np.sin(a) * scale


def _fourier_rows_kernel(m_ref, tc_ref, ts_ref, z_ref, o_ref):
    rows, cb = z_ref.shape[1], z_ref.shape[2]
    m = m_ref[...].astype(BF16)
    for ci in range(cb):
        z = z_ref[0, :, ci, :]
        zz = jnp.concatenate([z[:, :F_WIDTH], z[:, F_WIDTH:]], axis=0).astype(BF16)
        a = jnp.dot(m, zz, preferred_element_type=F32)
        a_re, a_im = a[:rows], a[rows:]
        tc = jnp.concatenate([tc_ref[ci]] * (F_WIDTH // LANES), axis=1)
        ts = jnp.concatenate([ts_ref[ci]] * (F_WIDTH // LANES), axis=1)
        o_ref[0, :, 0, ci, :] = a_re * tc + a_im * ts
        o_ref[0, :, 1, ci, :] = a_im * tc - a_re * ts


def _fourier_cols_kernel(g_ref, b_ref, o_ref):
    g = g_ref[...].astype(BF16)
    kt, _, cw, f = b_ref.shape[1:]
    for i in range(kt):
        bm = b_ref[0, i].reshape(2 * cw, f).astype(BF16)
        o_ref[0, :, i, :] = jnp.dot(g, bm, preferred_element_type=F32)


def _fourier_latent(zf, rows):
    b, s, _ = zf.shape
    cw = GRID_W
    c_r, s_r = _dft_cs(rows, rows ** -0.5)
    m1 = jnp.asarray(np.block([[c_r, s_r], [-s_r, c_r]]), F32)
    ang = 2.0 * np.pi * np.outer(np.arange(cw), np.arange(rows)) / s
    tc = jnp.asarray(np.repeat(np.cos(ang)[:, :, None], LANES, axis=2), F32)
    ts = jnp.asarray(np.repeat(np.sin(ang)[:, :, None], LANES, axis=2), F32)
    cb = 8
    bk = pl.pallas_call(
        _fourier_rows_kernel,
        grid=(cw // cb, b),
        in_specs=[_const_spec(m1.shape),
                  pl.BlockSpec((cb, rows, LANES), lambda j, bi: (j, 0, 0)),
                  pl.BlockSpec((cb, rows, LANES), lambda j, bi: (j, 0, 0)),
                  pl.BlockSpec((1, rows, cb, 2 * F_WIDTH), lambda j, bi: (bi, 0, j, 0))],
        out_specs=pl.BlockSpec((1, rows, 2, cb, F_WIDTH), lambda j, bi: (bi, 0, 0, j, 0)),
        out_shape=jax.ShapeDtypeStruct((b, rows, 2, cw, F_WIDTH), F32),
        compiler_params=_cparams(("arbitrary", "arbitrary")),
        name="fourier_rows",
    )(m1, tc, ts, zf.reshape(b, rows, cw, 2 * F_WIDTH))
    c_c, s_c = _dft_cs(cw, cw ** -0.5)
    g = jnp.asarray(np.concatenate([c_c, s_c], axis=1), F32)
    kt = 8
    out = pl.pallas_call(
        _fourier_cols_kernel,
        grid=(b, rows // kt),
        in_specs=[_const_spec(g.shape),
                  pl.BlockSpec((1, kt, 2, cw, F_WIDTH), lambda bi, i: (bi, i, 0, 0, 0))],
        out_specs=pl.BlockSpec((1, cw, kt, F_WIDTH), lambda bi, i: (bi, 0, i, 0)),
        out_shape=jax.ShapeDtypeStruct((b, cw, rows, F_WIDTH), F32),
        compiler_params=_cparams(("arbitrary", "arbitrary")),
        name="fourier_cols",
    )(g, bk)
    return out.reshape(b, s, F_WIDTH)


def _fourier_dense_kernel(m_ref, z_ref, o_ref):
    z = z_ref[0]
    zz = jnp.concatenate([z[:, :F_WIDTH], z[:, F_WIDTH:]], axis=0)
    o_ref[0] = jnp.dot(m_ref[...], zz, preferred_element_type=F32, precision=HI)


def _fourier_dense(zf):
    b, n, _ = zf.shape
    c_n, s_n = _dft_cs(n, n ** -0.5)
    m = jnp.asarray(np.concatenate([c_n, s_n], axis=1), F32)
    return pl.pallas_call(
        _fourier_dense_kernel,
        grid=(b,),
        in_specs=[_const_spec(m.shape), pl.BlockSpec((1, n, 2 * F_WIDTH), lambda bi: (bi, 0, 0))],
        out_specs=pl.BlockSpec((1, n, F_WIDTH), lambda bi: (bi, 0, 0)),
        out_shape=jax.ShapeDtypeStruct((b, n, F_WIDTH), F32),
        compiler_params=_cparams(("arbitrary",)),
        name="fourier_dense",
    )(m, zf)


def _merge_kernel(x_ref, mod_ref, gpre_ref, gpost_ref, ya_ref, wc_ref, wprev_ref, wnext_ref, bb_ref,
                  yr_ref, yf_ref, cw_ref, cb_ref, wz_ref, wm_ref, wo_ref, wout_ref, o_ref, *, tiles_per_seq):
    i = pl.program_id(0)
    x = x_ref[...]
    mod = mod_ref[0]
    h = _modulated_norm(x, gpre_ref[...], mod).astype(BF16)
    tm = x.shape[0]

    wc = wc_ref[...]
    t = i % tiles_per_seq
    prev_row = jnp.where(t > 0, wprev_ref[7:8, :], 0.0)
    next_row = jnp.where(t < tiles_per_seq - 1, wnext_ref[0:1, :], 0.0)
    row = lax.broadcasted_iota(jnp.int32, wc.shape, 0)
    up = jnp.where(row == 0, prev_row, pltpu.roll(wc, 1, 0))
    dn = jnp.where(row == tm - 1, next_row, pltpu.roll(wc, tm - 1, 0))
    conv = up * cw_ref[0:1, :] + wc * cw_ref[1:2, :] + dn * cw_ref[2:3, :] + cb_ref[...]
    yb = bb_ref[...] * conv

    ys = (ya_ref[...], yb, yr_ref[...], yf_ref[...])
    total = jnp.zeros((tm, D_MODEL), F32)
    for br in range(N_BRANCH):
        zo, zw = Z_OFFS[br], Z_SIZES[br]
        z = jnp.dot(h, wz_ref[:, zo:zo + zw], preferred_element_type=F32)
        act = (ys[br] * (z * _sigmoid(z))).astype(BF16)
        proj = jnp.dot(act, wo_ref[zo:zo + zw, :], preferred_element_type=F32)
        gate = _sigmoid(jnp.dot(h, wm_ref[:, br * D_MODEL:(br + 1) * D_MODEL], preferred_element_type=F32))
        total = total + gate * proj
    y = jnp.dot(total.astype(BF16), wout_ref[...], preferred_element_type=F32)
    ms = jnp.mean(y * y, axis=-1, keepdims=True)
    yn = y * lax.rsqrt(ms + EPS) * gpost_ref[...]
    o_ref[...] = x + mod[:, 2 * D_MODEL:3 * D_MODEL] * yn


def _merge_call(x2, mod3, g_pre, g_post, ya, wc, bb, yr, yf, conv_w, conv_b, wz, wm, wo, wout,
                *, tm, tiles_per_group, tiles_per_seq):
    rows, d = x2.shape
    nt = rows // tm
    hb = tm // 8
    nhb = rows // 8
    row_spec = lambda w: pl.BlockSpec((tm, w), lambda i: (i, 0))
    return pl.pallas_call(
        functools.partial(_merge_kernel, tiles_per_seq=tiles_per_seq),
        grid=(nt,),
        in_specs=[row_spec(d),
                  pl.BlockSpec((1, 1, 3 * d), lambda i: (i // tiles_per_group, 0, 0)),
                  _const_spec((1, d)), _const_spec((1, d)),
                  row_spec(A_WIDTH),
                  row_spec(B_WIDTH),
                  pl.BlockSpec((8, B_WIDTH), lambda i: (jnp.maximum(i * hb - 1, 0), 0)),
                  pl.BlockSpec((8, B_WIDTH), lambda i: (jnp.minimum((i + 1) * hb, nhb - 1), 0)),
                  row_spec(B_WIDTH), row_spec(R_WIDTH), row_spec(F_WIDTH),
                  _const_spec(conv_w.shape), _const_spec(conv_b.shape),
                  _weight_spec(wz.shape), _weight_spec(wm.shape), _weight_spec(wo.shape),
                  _weight_spec(wout.shape)],
        out_specs=row_spec(d),
        out_shape=jax.ShapeDtypeStruct((rows, d), F32),
        compiler_params=_cparams(("arbitrary",)),
        name="merge",
    )(x2, mod3, g_pre, g_post, ya, wc, wc, wc, bb, yr, yf, conv_w, conv_b, wz, wm, wo, wout)


def _rope_tables(n):
    rows = n // GRID_W
    row = jnp.broadcast_to(jnp.arange(rows)[:, None], (rows, GRID_W)).reshape(-1).astype(F32)
    col = jnp.broadcast_to(jnp.arange(GRID_W)[None, :], (rows, GRID_W)).reshape(-1).astype(F32)
    half = HEAD_DIM // 2
    inv = ROPE_BASE ** (-jnp.arange(0, half, 2, dtype=F32) / half)
    ang_r = row[:, None] * inv
    ang_c = col[:, None] * inv
    cos = jnp.concatenate([jnp.cos(ang_r), jnp.cos(ang_r), jnp.cos(ang_c), jnp.cos(ang_c)], axis=1)
    sin = jnp.concatenate([-jnp.sin(ang_r), jnp.sin(ang_r), -jnp.sin(ang_c), jnp.sin(ang_c)], axis=1)
    return jnp.tile(cos, (1, LANES // HEAD_DIM)), jnp.tile(sin, (1, LANES // HEAD_DIM))


def _cols(w, names):
    return jnp.concatenate([w[:, _OFFS[n]:_OFFS[n] + _SIZE[n]] for n in names], axis=1)


def _mix_weights(w):
    parts = []
    for n in MIX_NAMES:
        t = w[:, _OFFS[n]:_OFFS[n] + _SIZE[n]]
        if n in ("a_k", "a_v"):
            t = jnp.concatenate([t[:, h * HEAD_DIM:(h + 1) * HEAD_DIM]
                                 for h in range(A_KV_HEADS) for _ in range(2)], axis=1)
        parts.append(t)
    return jnp.concatenate(parts, axis=1)


def kernel(x, c, ctx, c_ctx, w_ada, b_ada, norm_pre, norm_post, w_in, attn_sink, conv_w, conv_b, ret_decay,
           w_o_attn, w_o_conv, w_o_ret, w_o_fourier, w_out):
    b, s, d = x.shape
    lc = ctx.shape[1]
    depth = w_in.shape[0]
    rows_grid = s // GRID_W

    cv = jnp.zeros((8, d), F32).at[:b].set(c).at[b].set(c_ctx)
    mods = _ada_call(cv, w_ada, b_ada)

    cos_x, sin_x = _rope_tables(s)
    cos_c = jnp.ones((lc, LANES), F32)
    sin_c = jnp.zeros((lc, LANES), F32)

    c64, s64 = _dft_cs(HEAD_DIM, HEAD_DIM ** -0.5)
    eye = np.eye(F_WIDTH // HEAD_DIM)
    fd = jnp.asarray(np.concatenate([np.kron(eye, c64), -np.kron(eye, s64)], axis=1), F32)

    lg_all = jax.nn.log_sigmoid(ret_decay.astype(F32))
    zero_state = jnp.zeros((b, R_WIDTH, R_WIDTH), F32)

    tm_x = 512
    tm_m = 512
    x2 = x.reshape(b * s, d)
    xc2 = ctx.reshape(b * lc, d)
    for l in range(depth):
        update_ctx = l < depth - 1
        w_l = w_in[l]
        w_mix = _mix_weights(w_l).astype(BF16)
        wz = _cols(w_l, Z_NAMES).astype(BF16)
        wm = w_l[:, _OFFS["merge"]:].astype(BF16)
        wo = jnp.concatenate([w_o_attn[l], w_o_conv[l], w_o_ret[l], w_o_fourier[l]], axis=0).astype(BF16)
        wout = w_out[l].astype(BF16)
        g_pre = norm_pre[l].reshape(1, d)
        g_post = norm_post[l].reshape(1, d)
        mod_x = mods[l, :b].reshape(b, 1, 3 * d)
        mod_c = mods[l, b:b + 1].reshape(1, 1, 3 * d)
        cb = conv_b[l].reshape(1, B_WIDTH)
        lg = lg_all[l]
        lgl = jnp.repeat(lg, HEAD_DIM, axis=1)

        px = _inproj_call(x2, mod_x, g_pre, w_mix, cos_x, sin_x, fd, tm=tm_x, tiles_per_group=s // tm_x)
        pc = _inproj_call(xc2, mod_c, g_pre, w_mix, cos_c, sin_c, fd, tm=lc, tiles_per_group=b)
        qa, ka, va, wcx, bbx, qr, kr, vr, zf = [t.reshape(b, s, -1) for t in px]
        qac, kac, vac, wcc, bbc, qrc, krc, vrc, zfc = [t.reshape(b, lc, -1) for t in pc]

        ya = _attn_call(attn_sink[l], qa, ka, va, kac, vac, local=True)

        ycf, st_f = _ret_call(lg[0], lgl[0:1], qrc, krc, vrc, zero_state, None, reverse=False)
        ycr, st_b = _ret_call(lg[1], lgl[1:2], qrc, krc, vrc, zero_state, ycf, reverse=True)
        yf_, _ = _ret_call(lg[0], lgl[0:1], qr, kr, vr, st_f, None, reverse=False)
        yr, _ = _ret_call(lg[1], lgl[1:2], qr, kr, vr, st_b, yf_, reverse=True)

        yfo = _fourier_latent(zf, rows_grid)

        x2_new = _merge_call(x2, mod_x, g_pre, g_post, ya.reshape(b * s, -1), wcx.reshape(b * s, -1),
                             bbx.reshape(b * s, -1), yr.reshape(b * s, -1), yfo.reshape(b * s, -1),
                             conv_w[l], cb, wz, wm, wo, wout,
                             tm=tm_m, tiles_per_group=s // tm_m, tiles_per_seq=s // tm_m)
        if update_ctx:
            yac = _attn_call(attn_sink[l], qac, None, None, kac, vac, local=False)
            yfc = _fourier_dense(zfc)
            xc2 = _merge_call(xc2, mod_c, g_pre, g_post, yac.reshape(b * lc, -1), wcc.reshape(b * lc, -1),
                              bbc.reshape(b * lc, -1), ycr.reshape(b * lc, -1), yfc.reshape(b * lc, -1),
                              conv_w[l], cb, wz, wm, wo, wout,
                              tm=lc, tiles_per_group=b, tiles_per_seq=1)
        x2 = x2_new
    return x2.reshape(b, s, d)
```

```python
import functools
import math

import numpy as np
import jax
import jax.numpy as jnp
from jax import lax
from jax.experimental import pallas as pl
from jax.experimental.pallas import tpu as pltpu

D_MODEL = 1024
GRID_W = 64
HEAD_DIM = 64
EPS = 1e-6
NEG_INF = -1e30
A_HEADS = 8
A_KV_HEADS = 2
A_BLOCK = 128
A_WIDTH = A_HEADS * HEAD_DIM
A_KV_WIDTH = A_KV_HEADS * HEAD_DIM
ROPE_BASE = 10000.0
B_WIDTH = 256
R_HEADS = 4
R_WIDTH = R_HEADS * HEAD_DIM
R_STEP = 256
F_WIDTH = 256
N_BRANCH = 4

IN_NAMES = ("a_q", "a_k", "a_v", "a_z", "b_u", "b_b", "b_c", "b_z", "r_q", "r_k", "r_v", "r_z", "f_u", "f_z", "merge")
IN_SIZES = (A_WIDTH, A_KV_WIDTH, A_KV_WIDTH, A_WIDTH, B_WIDTH, B_WIDTH, B_WIDTH, B_WIDTH,
            R_WIDTH, R_WIDTH, R_WIDTH, R_WIDTH, F_WIDTH, F_WIDTH, N_BRANCH * D_MODEL)
_OFFS = dict(zip(IN_NAMES, np.cumsum((0,) + IN_SIZES)[:-1].tolist()))
_SIZE = dict(zip(IN_NAMES, IN_SIZES))

MIX_NAMES = ("a_q", "a_k", "a_v", "b_u", "b_b", "b_c", "r_q", "r_k", "r_v", "f_u")
MIX_SIZE = dict({n: _SIZE[n] for n in MIX_NAMES}, a_k=2 * A_KV_WIDTH, a_v=2 * A_KV_WIDTH)
MIX_OFFS = dict(zip(MIX_NAMES, np.cumsum((0,) + tuple(MIX_SIZE[n] for n in MIX_NAMES))[:-1].tolist()))
Z_NAMES = ("a_z", "b_z", "r_z", "f_z")
Z_SIZES = tuple(_SIZE[n] for n in Z_NAMES)
Z_OFFS = tuple(np.cumsum((0,) + Z_SIZES)[:-1].tolist())
Z_WIDTH = sum(Z_SIZES)

MERGE_CHUNK = 256
ATTN_SUB = 4
LOG2E = math.log2(math.e)
LANES = 128
VMEM_LIMIT = 56 * 1024 * 1024

BF16 = jnp.bfloat16
F32 = jnp.float32
HI = lax.Precision.HIGHEST


def _cparams(sem):
    return pltpu.CompilerParams(dimension_semantics=sem, vmem_limit_bytes=VMEM_LIMIT)


def _const_spec(shape):
    nd = len(shape)
    return pl.BlockSpec(shape, lambda *_: (0,) * nd)


def _weight_spec(shape):
    nd = len(shape)
    return pl.BlockSpec(shape, lambda *_: (0,) * nd, pipeline_mode=pl.Buffered(1))


def _sigmoid(v):
    return 1.0 / (1.0 + jnp.exp(-v))


def _ada_kernel(cv_ref, w_ref, b_ref, o_ref):
    cv = cv_ref[...]
    s = cv * _sigmoid(cv)
    o_ref[0] = jnp.dot(s, w_ref[0], preferred_element_type=F32, precision=HI) + b_ref[0]


def _ada_call(cv, w_ada, b_ada):
    depth, d, d3 = w_ada.shape
    tn = 1024
    return pl.pallas_call(
        _ada_kernel,
        grid=(depth, d3 // tn),
        in_specs=[pl.BlockSpec((8, d), lambda l, j: (0, 0)),
                  pl.BlockSpec((1, d, tn), lambda l, j: (l, 0, j)),
                  pl.BlockSpec((1, 1, tn), lambda l, j: (l, 0, j))],
        out_specs=pl.BlockSpec((1, 8, tn), lambda l, j: (l, 0, j)),
        out_shape=jax.ShapeDtypeStruct((depth, 8, d3), F32),
        compiler_params=_cparams(("arbitrary", "arbitrary")),
        name="ada_mod",
    )(cv, w_ada, b_ada.reshape(depth, 1, d3))


def _modulated_norm(x, g, mod):
    ms = jnp.mean(x * x, axis=-1, keepdims=True)
    y = x * lax.rsqrt(ms + EPS) * g
    return y * (1.0 + mod[:, D_MODEL:2 * D_MODEL]) + mod[:, 0:D_MODEL]


def _rope(t, cos, sin_signed, first_half):
    outs = []
    for j in range(t.shape[1] // LANES):
        tj = t[:, j * LANES:(j + 1) * LANES]
        partner = jnp.where(first_half, pltpu.roll(tj, LANES - 16, 1), pltpu.roll(tj, 16, 1))
        outs.append(tj * cos + partner * sin_signed)
    return outs[0] if len(outs) == 1 else jnp.concatenate(outs, axis=1)


def _inproj_kernel(x_ref, mod_ref, g_ref, w_ref, cos_ref, sin_ref, fd_ref,
                   qa_ref, ka_ref, va_ref, wc_ref, bb_ref, qr_ref, kr_ref, vr_ref, zf_ref):
    h = _modulated_norm(x_ref[...], g_ref[...], mod_ref[0]).astype(BF16)

    def proj(name):
        o = MIX_OFFS[name]
        return jnp.dot(h, w_ref[:, o:o + MIX_SIZE[name]], preferred_element_type=F32)

    cos = cos_ref[...]
    sin = sin_ref[...]
    lane = lax.broadcasted_iota(jnp.int32, cos.shape, 1)
    first_half = (lane % 32) < 16
    low_half = lane < HEAD_DIM
    k_scale = HEAD_DIM ** -0.5
    q = _rope(proj("a_q"), cos, sin, first_half) * (k_scale * LOG2E)
    for p in range(A_WIDTH // LANES):
        qp = q[:, p * LANES:(p + 1) * LANES]
        qa_ref[:, (2 * p) * LANES:(2 * p + 1) * LANES] = jnp.where(low_half, qp, 0.0).astype(BF16)
        qa_ref[:, (2 * p + 1) * LANES:(2 * p + 2) * LANES] = jnp.where(low_half, 0.0, qp).astype(BF16)
    ka_ref[...] = _rope(proj("a_k"), cos, sin, first_half).astype(BF16)
    v = proj("a_v")
    for j in range(A_KV_HEADS):
        vj = v[:, j * LANES:(j + 1) * LANES]
        va_ref[:, (2 * j) * LANES:(2 * j + 1) * LANES] = jnp.where(low_half, vj, 1.0).astype(BF16)
        va_ref[:, (2 * j + 1) * LANES:(2 * j + 2) * LANES] = jnp.where(low_half, 1.0, vj).astype(BF16)
    wc_ref[...] = proj("b_c") * proj("b_u")
    bb_ref[...] = proj("b_b")
    qr_ref[...] = _rope(proj("r_q"), cos, sin, first_half)
    kr_ref[...] = _rope(proj("r_k"), cos, sin, first_half) * k_scale
    vr_ref[...] = proj("r_v")
    zf_ref[...] = jnp.dot(proj("f_u").astype(BF16), fd_ref[...].astype(BF16), preferred_element_type=F32)


def _inproj_call(x2, mod3, g_pre, w_mix, cos_t, sin_t, fd, *, tm, tiles_per_group):
    rows, d = x2.shape
    nt = rows // tm
    tiles_per_seq = cos_t.shape[0] // tm
    widths = (2 * A_WIDTH, 2 * A_KV_WIDTH, 4 * A_KV_WIDTH, B_WIDTH, B_WIDTH, R_WIDTH, R_WIDTH, R_WIDTH, 2 * F_WIDTH)
    dtypes = (BF16, BF16, BF16, F32, F32, F32, F32, F32, F32)
    row_spec = lambda w: pl.BlockSpec((tm, w), lambda i: (i, 0))
    return pl.pallas_call(
        _inproj_kernel,
        grid=(nt,),
        in_specs=[row_spec(d),
                  pl.BlockSpec((1, 1, 3 * d), lambda i: (i // tiles_per_group, 0, 0)),
                  _const_spec((1, d)),
                  _const_spec(w_mix.shape),
                  pl.BlockSpec((tm, LANES), lambda i: (i % tiles_per_seq, 0)),
                  pl.BlockSpec((tm, LANES), lambda i: (i % tiles_per_seq, 0)),
                  _const_spec(fd.shape)],
        out_specs=[row_spec(w) for w in widths],
        out_shape=[jax.ShapeDtypeStruct((rows, w), dt) for w, dt in zip(widths, dtypes)],
        compiler_params=_cparams(("arbitrary",)),
        name="in_proj",
    )(x2, mod3, g_pre, w_mix, cos_t, sin_t, fd)


def _attn_kernel(*refs, local):
    if local:
        (sink_ref, q_ref, kp_ref, kc_ref, kn_ref, vp_ref, vc_ref, vn_ref, kx_ref, vx_ref, o_ref,
         kwin, vwin) = refs
    else:
        sink_ref, q_ref, kx_ref, vx_ref, o_ref = refs
    blk = A_BLOCK
    nsub = q_ref.shape[1] // blk
    n = pl.program_id(1)
    last = pl.num_programs(1) * nsub - 1
    contract_last = (((1,), (1,)), ((), ()))

    lane_q = lax.broadcasted_iota(jnp.int32, (blk, LANES), 1)
    nloc = 3 * blk
    if local:
        step = nsub * blk
        kwin[0:blk] = kp_ref[0]
        kwin[blk:blk + step] = kc_ref[0]
        kwin[blk + step:2 * blk + step] = kn_ref[0]
        vwin[0:blk] = vp_ref[0]
        vwin[blk:blk + step] = vc_ref[0]
        vwin[blk + step:2 * blk + step] = vn_ref[0]
        row = lax.broadcasted_iota(jnp.int32, (blk, nloc), 0)
        col = lax.broadcasted_iota(jnp.int32, (blk, nloc), 1)
        band = jnp.minimum(col - row, row + 2 * blk - col)

    for sub in range(nsub):
        q0 = sub * blk
        if local:
            g_blk = n * nsub + sub
            seq_lo = blk - g_blk * blk
            seq_hi = blk + (last - g_blk + 1) * blk
            valid = (jnp.minimum(band, jnp.minimum(col - seq_lo, seq_hi - 1 - col)) >= 0)[None]
        for j in range(A_KV_HEADS):
            kcol = slice(j * LANES, (j + 1) * LANES)
            ecol = slice(2 * j * LANES, (2 * j + 1) * LANES)
            ocol = slice((2 * j + 1) * LANES, (2 * j + 2) * LANES)
            qs = jnp.concatenate([q_ref[0, q0:q0 + blk, (4 * j + g) * LANES:(4 * j + g + 1) * LANES]
                                  for g in range(4)], axis=0)
            s = lax.dot_general(qs, kx_ref[0, :, kcol], contract_last, preferred_element_type=F32)
            if local:
                s_loc = lax.dot_general(qs, kwin[q0:q0 + nloc, kcol], contract_last, preferred_element_type=F32)
                s_loc = jnp.where(valid, s_loc.reshape(4, blk, nloc), NEG_INF).reshape(4 * blk, nloc)
                s = jnp.concatenate([s_loc, s], axis=1)
            sink = jnp.concatenate(
                [jnp.full((blk, 1), sink_ref[4 * j + g] * LOG2E, F32) for g in range(4)], axis=0)
            m = jnp.maximum(jnp.max(s, axis=1, keepdims=True), sink)
            pb = jnp.exp2(s - m).astype(BF16)
            e_sink = jnp.exp2(sink - m)

            vcol = slice(2 * j * LANES, (2 * j + 2) * LANES)
            if local:
                o_all = (jnp.dot(pb[:, :nloc], vwin[q0:q0 + nloc, vcol], preferred_element_type=F32)
                         + jnp.dot(pb[:, nloc:], vx_ref[0, :, vcol], preferred_element_type=F32))
            else:
                o_all = jnp.dot(pb, vx_ref[0, :, vcol], preferred_element_type=F32)

            for pair in range(2):
                r0 = 2 * pair * blk
                o_even = o_all[r0:r0 + blk, :LANES]
                o_odd = o_all[r0 + blk:r0 + 2 * blk, LANES:]
                num = jnp.where(lane_q < 64, o_even, o_odd)
                den = (pltpu.roll(jnp.where(lane_q < 64, o_odd, o_even), 64, 1)
                       + jnp.where(lane_q < 64, e_sink[r0:r0 + blk], e_sink[r0 + blk:r0 + 2 * blk]))
                c0 = (2 * j + pair) * LANES
                o_ref[0, q0:q0 + blk, c0:c0 + LANES] = num / den


def _attn_call(sink, q, k, v, kx, vx, *, local):
    b, sq, _ = q.shape
    blk = A_BLOCK
    nq = sq // blk
    nsub = min(ATTN_SUB, nq)
    step = nsub * blk
    lx = kx.shape[1]
    smem = pl.BlockSpec(memory_space=pltpu.SMEM)
    kw, vw = kx.shape[2], vx.shape[2]
    q_spec = pl.BlockSpec((1, step, q.shape[2]), lambda bi, n: (bi, n, 0))
    if local:
        def halo(w):
            return (pl.BlockSpec((1, blk, w), lambda bi, n: (bi, jnp.maximum(n * nsub - 1, 0), 0)),
                    pl.BlockSpec((1, step, w), lambda bi, n: (bi, n, 0)),
                    pl.BlockSpec((1, blk, w), lambda bi, n: (bi, jnp.minimum((n + 1) * nsub, nq - 1), 0)))
        in_specs = [smem, q_spec, *halo(kw), *halo(vw),
                    pl.BlockSpec((1, lx, kw), lambda bi, n: (bi, 0, 0)),
                    pl.BlockSpec((1, lx, vw), lambda bi, n: (bi, 0, 0))]
        args = (sink, q, k, k, k, v, v, v, kx, vx)
        scratch = [pltpu.VMEM((step + 2 * blk, kw), k.dtype), pltpu.VMEM((step + 2 * blk, vw), v.dtype)]
    else:
        in_specs = [smem, q_spec, pl.BlockSpec((1, lx, kw), lambda bi, n: (bi, 0, 0)),
                    pl.BlockSpec((1, lx, vw), lambda bi, n: (bi, 0, 0))]
        args = (sink, q, kx, vx)
        scratch = []
    return pl.pallas_call(
        functools.partial(_attn_kernel, local=local),
        grid=(b, nq // nsub),
        in_specs=in_specs,
        out_specs=pl.BlockSpec((1, step, A_WIDTH), lambda bi, n: (bi, n, 0)),
        out_shape=jax.ShapeDtypeStruct((b, sq, A_WIDTH), F32),
        scratch_shapes=scratch,
        compiler_params=_cparams(("arbitrary", "arbitrary")),
        name="win_attn" if local else "ctx_attn",
    )(*args)


def _head_of(shape, dim):
    return lax.broadcasted_iota(jnp.int32, shape, dim) // HEAD_DIM


def _group_mean(t, avg):
    hi = t.astype(BF16)
    lo = (t - hi.astype(F32)).astype(BF16)
    return (jnp.dot(hi, avg, preferred_element_type=F32) + jnp.dot(lo, avg, preferred_element_type=F32))


def _ret_kernel(*refs, reverse, finalize):
    if finalize:
        lg_ref, lgl_ref, q_ref, k_ref, v_ref, r0_ref, yin_ref, y_ref, rfin_ref, r_scr = refs
    else:
        lg_ref, lgl_ref, q_ref, k_ref, v_ref, r0_ref, y_ref, rfin_ref, r_scr = refs
    c = pl.program_id(0)
    nb, ch, w = q_ref.shape

    @pl.when(c == 0)
    def _():
        r_scr[...] = r0_ref[...]

    lgl = lgl_ref[...]
    pos = lax.broadcasted_iota(jnp.int32, (ch, 1), 0).astype(F32)
    if reverse:
        xi = jnp.exp(lgl * (ch - pos))
        zeta = jnp.exp(lgl * pos)
    else:
        xi = jnp.exp(lgl * (pos + 1.0))
        zeta = jnp.exp(lgl * (ch - 1.0 - pos))
    g_chunk = jnp.exp(lgl * float(ch))
    ri = lax.broadcasted_iota(jnp.int32, (ch, ch), 0)
    ci = lax.broadcasted_iota(jnp.int32, (ch, ch), 1)
    diff = (ci - ri) if reverse else (ri - ci)
    dist = jnp.maximum(diff, 0).astype(F32)
    decays = [jnp.where(diff >= 0, jnp.exp(lg_ref[h] * dist), 0.0) for h in range(R_HEADS)]
    lane_head = _head_of((ch, w), 1)
    same_head = _head_of((w, w), 0) == _head_of((w, w), 1)
    avg = jnp.where(same_head, 1.0 / HEAD_DIM, 0.0).astype(BF16)

    for bi in range(nb):
        q = q_ref[bi]
        k = k_ref[bi]
        v = v_ref[bi]
        r = r_scr[bi]
        cross = jnp.dot((q * xi).astype(BF16), r.astype(BF16), preferred_element_type=F32)
        q4 = jnp.concatenate([jnp.where(lane_head == h, q, 0.0) for h in range(R_HEADS)], axis=0).astype(BF16)
        sc = lax.dot_general(q4, k.astype(BF16), (((1,), (1,)), ((), ())), preferred_element_type=F32)
        s4 = jnp.concatenate([(sc[h * ch:(h + 1) * ch] * decays[h]).astype(BF16) for h in range(R_HEADS)], axis=1)
        v4 = jnp.concatenate([jnp.where(lane_head == h, v, 0.0) for h in range(R_HEADS)], axis=0).astype(BF16)
        y = jnp.dot(s4, v4, preferred_element_type=F32) + cross

        kz = (k * zeta).astype(BF16)
        ktv = lax.dot_general(kz, v.astype(BF16), (((0,), (0,)), ((), ())), preferred_element_type=F32)
        r_new = g_chunk * r + jnp.where(same_head, ktv, 0.0)
        r_scr[bi] = r_new
        rfin_ref[bi] = r_new

        if finalize:
            y = y + yin_ref[bi]
            mu = _group_mean(y, avg)
            d = y - mu
            var = _group_mean(d * d, avg)
            y = d * lax.rsqrt(var + EPS)
        y_ref[bi] = y


def _ret_call(lg, lgl, q, k, v, r0, y_in, *, reverse):
    b, s, w = q.shape
    ch = min(R_STEP, s)
    nc = s // ch
    finalize = y_in is not None
    cidx = (lambda c: (0, nc - 1 - c, 0)) if reverse else (lambda c: (0, c, 0))
    chunk = pl.BlockSpec((b, ch, w), cidx)
    state = pl.BlockSpec((b, w, w), lambda c: (0, 0, 0))
    in_specs = [pl.BlockSpec(memory_space=pltpu.SMEM), pl.BlockSpec((1, w), lambda c: (0, 0)),
                chunk, chunk, chunk, state]
    args = [lg, lgl, q, k, v, r0]
    if finalize:
        in_specs.append(chunk)
        args.append(y_in)
    return pl.pallas_call(
        functools.partial(_ret_kernel, reverse=reverse, finalize=finalize),
        grid=(nc,),
        in_specs=in_specs,
        out_specs=[chunk, state],
        out_shape=[jax.ShapeDtypeStruct((b, s, w), F32), jax.ShapeDtypeStruct((b, w, w), F32)],
        scratch_shapes=[pltpu.VMEM((b, w, w), F32)],
        compiler_params=_cparams(("arbitrary",)),
        name="retention_bwd" if reverse else "retention_fwd",
    )(*args)


def _dft_cs(n, scale):
    a = 2.0 * np.pi * np.outer(np.arange(n), np.arange(n)) / n
    return np.cos(a) * scale, np.sin(a) * scale


def _fourier_rows_kernel(m_ref, tc_ref, ts_ref, z_ref, o_ref):
    rows, cb = z_ref.shape[1], z_ref.shape[2]
    m = m_ref[...].astype(BF16)
    for ci in range(cb):
        z = z_ref[0, :, ci, :]
        zz = jnp.concatenate([z[:, :F_WIDTH], z[:, F_WIDTH:]], axis=0).astype(BF16)
        a = jnp.dot(m, zz, preferred_element_type=F32)
        a_re, a_im = a[:rows], a[rows:]
        tc = jnp.concatenate([tc_ref[ci]] * (F_WIDTH // LANES), axis=1)
        ts = jnp.concatenate([ts_ref[ci]] * (F_WIDTH // LANES), axis=1)
        o_ref[0, :, 0, ci, :] = a_re * tc + a_im * ts
        o_ref[0, :, 1, ci, :] = a_im * tc - a_re * ts


def _fourier_cols_kernel(g_ref, b_ref, o_ref):
    g = g_ref[...].astype(BF16)
    kt, _, cw, f = b_ref.shape[1:]
    for i in range(kt):
        bm = b_ref[0, i].reshape(2 * cw, f).astype(BF16)
        o_ref[0, :, i, :] = jnp.dot(g, bm, preferred_element_type=F32)


def _fourier_latent(zf, rows):
    b, s, _ = zf.shape
    cw = GRID_W
    c_r, s_r = _dft_cs(rows, rows ** -0.5)
    m1 = jnp.asarray(np.block([[c_r, s_r], [-s_r, c_r]]), F32)
    ang = 2.0 * np.pi * np.outer(np.arange(cw), np.arange(rows)) / s
    tc = jnp.asarray(np.repeat(np.cos(ang)[:, :, None], LANES, axis=2), F32)
    ts = jnp.asarray(np.repeat(np.sin(ang)[:, :, None], LANES, axis=2), F32)
    cb = 8
    bk = pl.pallas_call(
        _fourier_rows_kernel,
        grid=(cw // cb, b),
        in_specs=[_const_spec(m1.shape),
                  pl.BlockSpec((cb, rows, LANES), lambda j, bi: (j, 0, 0)),
                  pl.BlockSpec((cb, rows, LANES), lambda j, bi: (j, 0, 0)),
                  pl.BlockSpec((1, rows, cb, 2 * F_WIDTH), lambda j, bi: (bi, 0, j, 0))],
        out_specs=pl.BlockSpec((1, rows, 2, cb, F_WIDTH), lambda j, bi: (bi, 0, 0, j, 0)),
        out_shape=jax.ShapeDtypeStruct((b, rows, 2, cw, F_WIDTH), F32),
        compiler_params=_cparams(("arbitrary", "arbitrary")),
        name="fourier_rows",
    )(m1, tc, ts, zf.reshape(b, rows, cw, 2 * F_WIDTH))
    c_c, s_c = _dft_cs(cw, cw ** -0.5)
    g = jnp.asarray(np.concatenate([c_c, s_c], axis=1), F32)
    kt = 8
    out = pl.pallas_call(
        _fourier_cols_kernel,
        grid=(b, rows // kt),
        in_specs=[_const_spec(g.shape),
                  pl.BlockSpec((1, kt, 2, cw, F_WIDTH), lambda bi, i: (bi, i, 0, 0, 0))],
        out_specs=pl.BlockSpec((1, cw, kt, F_WIDTH), lambda bi, i: (bi, 0, i, 0)),
        out_shape=jax.ShapeDtypeStruct((b, cw, rows, F_WIDTH), F32),
        compiler_params=_cparams(("arbitrary", "arbitrary")),
        name="fourier_cols",
    )(g, bk)
    return out.reshape(b, s, F_WIDTH)


def _fourier_dense_kernel(m_ref, z_ref, o_ref):
    z = z_ref[0]
    zz = jnp.concatenate([z[:, :F_WIDTH], z[:, F_WIDTH:]], axis=0)
    o_ref[0] = jnp.dot(m_ref[...], zz, preferred_element_type=F32, precision=HI)


def _fourier_dense(zf):
    b, n, _ = zf.shape
    c_n, s_n = _dft_cs(n, n ** -0.5)
    m = jnp.asarray(np.concatenate([c_n, s_n], axis=1), F32)
    return pl.pallas_call(
        _fourier_dense_kernel,
        grid=(b,),
        in_specs=[_const_spec(m.shape), pl.BlockSpec((1, n, 2 * F_WIDTH), lambda bi: (bi, 0, 0))],
        out_specs=pl.BlockSpec((1, n, F_WIDTH), lambda bi: (bi, 0, 0)),
        out_shape=jax.ShapeDtypeStruct((b, n, F_WIDTH), F32),
        compiler_params=_cparams(("arbitrary",)),
        name="fourier_dense",
    )(m, zf)


def _merge_kernel(x_ref, mod_ref, gpre_ref, gpost_ref, ya_ref, wc_ref, wprev_ref, wnext_ref, bb_ref,
                  yr_ref, yf_ref, cw_ref, cb_ref, wz_ref, wm_ref, wo_ref, wout_ref, o_ref, *, tiles_per_seq):
    i = pl.program_id(0)
    x = x_ref[...]
    mod = mod_ref[0]
    h = _modulated_norm(x, gpre_ref[...], mod).astype(BF16)
    tm = x.shape[0]

    wc = wc_ref[...]
    t = i % tiles_per_seq
    prev_row = jnp.where(t > 0, wprev_ref[7:8, :], 0.0)
    next_row = jnp.where(t < tiles_per_seq - 1, wnext_ref[0:1, :], 0.0)
    row = lax.broadcasted_iota(jnp.int32, wc.shape, 0)
    up = jnp.where(row == 0, prev_row, pltpu.roll(wc, 1, 0))
    dn = jnp.where(row == tm - 1, next_row, pltpu.roll(wc, tm - 1, 0))
    conv = up * cw_ref[0:1, :] + wc * cw_ref[1:2, :] + dn * cw_ref[2:3, :] + cb_ref[...]
    yb = bb_ref[...] * conv

    ys = (ya_ref[...], yb, yr_ref[...], yf_ref[...])
    acts = []
    for br in range(N_BRANCH):
        zo, zw = Z_OFFS[br], Z_SIZES[br]
        z = jnp.dot(h, wz_ref[:, zo:zo + zw], preferred_element_type=F32)
        acts.append((ys[br] * (z * _sigmoid(z))).astype(BF16))
    y = jnp.zeros((tm, D_MODEL), F32)
    for c0 in range(0, D_MODEL, MERGE_CHUNK):
        total = jnp.zeros((tm, MERGE_CHUNK), F32)
        for br in range(N_BRANCH):
            zo, zw = Z_OFFS[br], Z_SIZES[br]
            proj = jnp.dot(acts[br], wo_ref[zo:zo + zw, c0:c0 + MERGE_CHUNK], preferred_element_type=F32)
            g0 = br * D_MODEL + c0
            gate = _sigmoid(jnp.dot(h, wm_ref[:, g0:g0 + MERGE_CHUNK], preferred_element_type=F32))
            total = total + gate * proj
        y = y + jnp.dot(total.astype(BF16), wout_ref[c0:c0 + MERGE_CHUNK, :], preferred_element_type=F32)
    ms = jnp.mean(y * y, axis=-1, keepdims=True)
    yn = y * lax.rsqrt(ms + EPS) * gpost_ref[...]
    o_ref[...] = x + mod[:, 2 * D_MODEL:3 * D_MODEL] * yn


def _merge_call(x2, mod3, g_pre, g_post, ya, wc, bb, yr, yf, conv_w, conv_b, wz, wm, wo, wout,
                *, tm, tiles_per_group, tiles_per_seq):
    rows, d = x2.shape
    nt = rows // tm
    hb = tm // 8
    nhb = rows // 8
    row_spec = lambda w: pl.BlockSpec((tm, w), lambda i: (i, 0))
    return pl.pallas_call(
        functools.partial(_merge_kernel, tiles_per_seq=tiles_per_seq),
        grid=(nt,),
        in_specs=[row_spec(d),
                  pl.BlockSpec((1, 1, 3 * d), lambda i: (i // tiles_per_group, 0, 0)),
                  _const_spec((1, d)), _const_spec((1, d)),
                  row_spec(A_WIDTH),
                  row_spec(B_WIDTH),
                  pl.BlockSpec((8, B_WIDTH), lambda i: (jnp.maximum(i * hb - 1, 0), 0)),
                  pl.BlockSpec((8, B_WIDTH), lambda i: (jnp.minimum((i + 1) * hb, nhb - 1), 0)),
                  row_spec(B_WIDTH), row_spec(R_WIDTH), row_spec(F_WIDTH),
                  _const_spec(conv_w.shape), _const_spec(conv_b.shape),
                  _weight_spec(wz.shape), _weight_spec(wm.shape), _weight_spec(wo.shape),
                  _weight_spec(wout.shape)],
        out_specs=row_spec(d),
        out_shape=jax.ShapeDtypeStruct((rows, d), F32),
        compiler_params=_cparams(("arbitrary",)),
        name="merge",
    )(x2, mod3, g_pre, g_post, ya, wc, wc, wc, bb, yr, yf, conv_w, conv_b, wz, wm, wo, wout)


def _rope_tables(n):
    rows = n // GRID_W
    row = jnp.broadcast_to(jnp.arange(rows)[:, None], (rows, GRID_W)).reshape(-1).astype(F32)
    col = jnp.broadcast_to(jnp.arange(GRID_W)[None, :], (rows, GRID_W)).reshape(-1).astype(F32)
    half = HEAD_DIM // 2
    inv = ROPE_BASE ** (-jnp.arange(0, half, 2, dtype=F32) / half)
    ang_r = row[:, None] * inv
    ang_c = col[:, None] * inv
    cos = jnp.concatenate([jnp.cos(ang_r), jnp.cos(ang_r), jnp.cos(ang_c), jnp.cos(ang_c)], axis=1)
    sin = jnp.concatenate([-jnp.sin(ang_r), jnp.sin(ang_r), -jnp.sin(ang_c), jnp.sin(ang_c)], axis=1)
    return jnp.tile(cos, (1, LANES // HEAD_DIM)), jnp.tile(sin, (1, LANES // HEAD_DIM))


def _cols(w, names):
    return jnp.concatenate([w[:, _OFFS[n]:_OFFS[n] + _SIZE[n]] for n in names], axis=1)


def _mix_weights(w):
    parts = []
    for n in MIX_NAMES:
        t = w[:, _OFFS[n]:_OFFS[n] + _SIZE[n]]
        if n in ("a_k", "a_v"):
            t = jnp.concatenate([t[:, h * HEAD_DIM:(h + 1) * HEAD_DIM]
                                 for h in range(A_KV_HEADS) for _ in range(2)], axis=1)
        parts.append(t)
    return jnp.concatenate(parts, axis=1)


def kernel(x, c, ctx, c_ctx, w_ada, b_ada, norm_pre, norm_post, w_in, attn_sink, conv_w, conv_b, ret_decay,
           w_o_attn, w_o_conv, w_o_ret, w_o_fourier, w_out):
    b, s, d = x.shape
    lc = ctx.shape[1]
    depth = w_in.shape[0]
    rows_grid = s // GRID_W

    cv = jnp.zeros((8, d), F32).at[:b].set(c).at[b].set(c_ctx)
    mods = _ada_call(cv, w_ada, b_ada)

    cos_x, sin_x = _rope_tables(s)
    cos_c = jnp.ones((lc, LANES), F32)
    sin_c = jnp.zeros((lc, LANES), F32)

    c64, s64 = _dft_cs(HEAD_DIM, HEAD_DIM ** -0.5)
    eye = np.eye(F_WIDTH // HEAD_DIM)
    fd = jnp.asarray(np.concatenate([np.kron(eye, c64), -np.kron(eye, s64)], axis=1), F32)

    lg_all = jax.nn.log_sigmoid(ret_decay.astype(F32))
    zero_state = jnp.zeros((b, R_WIDTH, R_WIDTH), F32)

    tm_x = 512
    tm_m = 512
    x2 = x.reshape(b * s, d)
    xc2 = ctx.reshape(b * lc, d)
    for l in range(depth):
        update_ctx = l < depth - 1
        w_l = w_in[l]
        w_mix = _mix_weights(w_l).astype(BF16)
        wz = _cols(w_l, Z_NAMES).astype(BF16)
        wm = w_l[:, _OFFS["merge"]:].astype(BF16)
        wo = jnp.concatenate([w_o_attn[l], w_o_conv[l], w_o_ret[l], w_o_fourier[l]], axis=0).astype(BF16)
        wout = w_out[l].astype(BF16)
        g_pre = norm_pre[l].reshape(1, d)
        g_post = norm_post[l].reshape(1, d)
        mod_x = mods[l, :b].reshape(b, 1, 3 * d)
        mod_c = mods[l, b:b + 1].reshape(1, 1, 3 * d)
        cb = conv_b[l].reshape(1, B_WIDTH)
        lg = lg_all[l]
        lgl = jnp.repeat(lg, HEAD_DIM, axis=1)

        px = _inproj_call(x2, mod_x, g_pre, w_mix, cos_x, sin_x, fd, tm=tm_x, tiles_per_group=s // tm_x)
        pc = _inproj_call(xc2, mod_c, g_pre, w_mix, cos_c, sin_c, fd, tm=lc, tiles_per_group=b)
        qa, ka, va, wcx, bbx, qr, kr, vr, zf = [t.reshape(b, s, -1) for t in px]
        qac, kac, vac, wcc, bbc, qrc, krc, vrc, zfc = [t.reshape(b, lc, -1) for t in pc]

        ya = _attn_call(attn_sink[l], qa, ka, va, kac, vac, local=True)

        ycf, st_f = _ret_call(lg[0], lgl[0:1], qrc, krc, vrc, zero_state, None, reverse=False)
        ycr, st_b = _ret_call(lg[1], lgl[1:2], qrc, krc, vrc, zero_state, ycf, reverse=True)
        yf_, _ = _ret_call(lg[0], lgl[0:1], qr, kr, vr, st_f, None, reverse=False)
        yr, _ = _ret_call(lg[1], lgl[1:2], qr, kr, vr, st_b, yf_, reverse=True)

        yfo = _fourier_latent(zf, rows_grid)

        x2_new = _merge_call(x2, mod_x, g_pre, g_post, ya.reshape(b * s, -1), wcx.reshape(b * s, -1),
                             bbx.reshape(b * s, -1), yr.reshape(b * s, -1), yfo.reshape(b * s, -1),
                             conv_w[l], cb, wz, wm, wo, wout,
                             tm=tm_m, tiles_per_group=s // tm_m, tiles_per_seq=s // tm_m)
        if update_ctx:
            yac = _attn_call(attn_sink[l], qac, None, None, kac, vac, local=False)
            yfc = _fourier_dense(zfc)
            xc2 = _merge_call(xc2, mod_c, g_pre, g_post, yac.reshape(b * lc, -1), wcc.reshape(b * lc, -1),
                              bbc.reshape(b * lc, -1), ycr.reshape(b * lc, -1), yfc.reshape(b * lc, -1),
                              conv_w[l], cb, wz, wm, wo, wout,
                              tm=lc, tiles_per_group=b, tiles_per_seq=1)
        x2 = x2_new
    return x2.reshape(b, s, d)
```

```python
import functools
import math

import numpy as np
import jax
import jax.numpy as jnp
from jax import lax
from jax.experimental import pallas as pl
from jax.experimental.pallas import tpu as pltpu

D_MODEL = 1024
GRID_W = 64
HEAD_DIM = 64
EPS = 1e-6
NEG_INF = -1e30
A_HEADS = 8
A_KV_HEADS = 2
A_BLOCK = 128
A_WIDTH = A_HEADS * HEAD_DIM
A_KV_WIDTH = A_KV_HEADS * HEAD_DIM
ROPE_BASE = 10000.0
B_WIDTH = 256
R_HEADS = 4
R_WIDTH = R_HEADS * HEAD_DIM
R_STEP = 256
F_WIDTH = 256
N_BRANCH = 4

IN_NAMES = ("a_q", "a_k", "a_v", "a_z", "b_u", "b_b", "b_c", "b_z", "r_q", "r_k", "r_v", "r_z", "f_u", "f_z", "merge")
IN_SIZES = (A_WIDTH, A_KV_WIDTH, A_KV_WIDTH, A_WIDTH, B_WIDTH, B_WIDTH, B_WIDTH, B_WIDTH,
            R_WIDTH, R_WIDTH, R_WIDTH, R_WIDTH, F_WIDTH, F_WIDTH, N_BRANCH * D_MODEL)
_OFFS = dict(zip(IN_NAMES, np.cumsum((0,) + IN_SIZES)[:-1].tolist()))
_SIZE = dict(zip(IN_NAMES, IN_SIZES))

MIX_NAMES = ("a_q", "a_k", "a_v", "b_u", "b_b", "b_c", "r_q", "r_k", "r_v", "f_u")
MIX_SIZE = dict({n: _SIZE[n] for n in MIX_NAMES}, a_k=2 * A_KV_WIDTH, a_v=2 * A_KV_WIDTH)
MIX_OFFS = dict(zip(MIX_NAMES, np.cumsum((0,) + tuple(MIX_SIZE[n] for n in MIX_NAMES))[:-1].tolist()))
Z_NAMES = ("a_z", "b_z", "r_z", "f_z")
Z_SIZES = tuple(_SIZE[n] for n in Z_NAMES)
Z_OFFS = tuple(np.cumsum((0,) + Z_SIZES)[:-1].tolist())
Z_WIDTH = sum(Z_SIZES)

MERGE_CHUNK = 256
ATTN_SUB = 8
LOG2E = math.log2(math.e)
LANES = 128
VMEM_LIMIT = 56 * 1024 * 1024

BF16 = jnp.bfloat16
F32 = jnp.float32
HI = lax.Precision.HIGHEST


def _cparams(sem):
    return pltpu.CompilerParams(dimension_semantics=sem, vmem_limit_bytes=VMEM_LIMIT)


def _const_spec(shape):
    nd = len(shape)
    return pl.BlockSpec(shape, lambda *_: (0,) * nd)


def _weight_spec(shape):
    nd = len(shape)
    return pl.BlockSpec(shape, lambda *_: (0,) * nd, pipeline_mode=pl.Buffered(1))


def _sigmoid(v):
    return 1.0 / (1.0 + jnp.exp(-v))


def _ada_kernel(cv_ref, w_ref, b_ref, o_ref):
    cv = cv_ref[...]
    s = cv * _sigmoid(cv)
    o_ref[0] = jnp.dot(s, w_ref[0], preferred_element_type=F32, precision=HI) + b_ref[0]


def _ada_call(cv, w_ada, b_ada):
    depth, d, d3 = w_ada.shape
    tn = 1024
    return pl.pallas_call(
        _ada_kernel,
        grid=(depth, d3 // tn),
        in_specs=[pl.BlockSpec((8, d), lambda l, j: (0, 0)),
                  pl.BlockSpec((1, d, tn), lambda l, j: (l, 0, j)),
                  pl.BlockSpec((1, 1, tn), lambda l, j: (l, 0, j))],
        out_specs=pl.BlockSpec((1, 8, tn), lambda l, j: (l, 0, j)),
        out_shape=jax.ShapeDtypeStruct((depth, 8, d3), F32),
        compiler_params=_cparams(("arbitrary", "arbitrary")),
        name="ada_mod",
    )(cv, w_ada, b_ada.reshape(depth, 1, d3))


def _modulated_norm(x, g, mod):
    ms = jnp.mean(x * x, axis=-1, keepdims=True)
    y = x * lax.rsqrt(ms + EPS) * g
    return y * (1.0 + mod[:, D_MODEL:2 * D_MODEL]) + mod[:, 0:D_MODEL]


def _rope(t, cos, sin_signed, first_half):
    outs = []
    for j in range(t.shape[1] // LANES):
        tj = t[:, j * LANES:(j + 1) * LANES]
        partner = jnp.where(first_half, pltpu.roll(tj, LANES - 16, 1), pltpu.roll(tj, 16, 1))
        outs.append(tj * cos + partner * sin_signed)
    return outs[0] if len(outs) == 1 else jnp.concatenate(outs, axis=1)


def _inproj_kernel(x_ref, mod_ref, g_ref, w_ref, cos_ref, sin_ref, fd_ref,
                   qa_ref, ka_ref, va_ref, wc_ref, bb_ref, qr_ref, kr_ref, vr_ref, zf_ref):
    h = _modulated_norm(x_ref[...], g_ref[...], mod_ref[0]).astype(BF16)

    def proj(name):
        o = MIX_OFFS[name]
        return jnp.dot(h, w_ref[:, o:o + MIX_SIZE[name]], preferred_element_type=F32)

    cos = cos_ref[...]
    sin = sin_ref[...]
    lane = lax.broadcasted_iota(jnp.int32, cos.shape, 1)
    first_half = (lane % 32) < 16
    low_half = lane < HEAD_DIM
    k_scale = HEAD_DIM ** -0.5
    q = _rope(proj("a_q"), cos, sin, first_half) * (k_scale * LOG2E)
    for p in range(A_WIDTH // LANES):
        qp = q[:, p * LANES:(p + 1) * LANES]
        qa_ref[:, (2 * p) * LANES:(2 * p + 1) * LANES] = jnp.where(low_half, qp, 0.0).astype(BF16)
        qa_ref[:, (2 * p + 1) * LANES:(2 * p + 2) * LANES] = jnp.where(low_half, 0.0, qp).astype(BF16)
    ka_ref[...] = _rope(proj("a_k"), cos, sin, first_half).astype(BF16)
    v = proj("a_v")
    for j in range(A_KV_HEADS):
        vj = v[:, j * LANES:(j + 1) * LANES]
        va_ref[:, (2 * j) * LANES:(2 * j + 1) * LANES] = jnp.where(low_half, vj, 1.0).astype(BF16)
        va_ref[:, (2 * j + 1) * LANES:(2 * j + 2) * LANES] = jnp.where(low_half, 1.0, vj).astype(BF16)
    wc_ref[...] = proj("b_c") * proj("b_u")
    bb_ref[...] = proj("b_b")
    qr_ref[...] = _rope(proj("r_q"), cos, sin, first_half)
    kr_ref[...] = _rope(proj("r_k"), cos, sin, first_half) * k_scale
    vr_ref[...] = proj("r_v")
    zf_ref[...] = jnp.dot(proj("f_u").astype(BF16), fd_ref[...].astype(BF16), preferred_element_type=F32)


def _inproj_call(x2, mod3, g_pre, w_mix, cos_t, sin_t, fd, *, tm, tiles_per_group):
    rows, d = x2.shape
    nt = rows // tm
    tiles_per_seq = cos_t.shape[0] // tm
    widths = (2 * A_WIDTH, 2 * A_KV_WIDTH, 4 * A_KV_WIDTH, B_WIDTH, B_WIDTH, R_WIDTH, R_WIDTH, R_WIDTH, 2 * F_WIDTH)
    dtypes = (BF16, BF16, BF16, F32, F32, F32, F32, F32, F32)
    row_spec = lambda w: pl.BlockSpec((tm, w), lambda i: (i, 0))
    return pl.pallas_call(
        _inproj_kernel,
        grid=(nt,),
        in_specs=[row_spec(d),
                  pl.BlockSpec((1, 1, 3 * d), lambda i: (i // tiles_per_group, 0, 0)),
                  _const_spec((1, d)),
                  _const_spec(w_mix.shape),
                  pl.BlockSpec((tm, LANES), lambda i: (i % tiles_per_seq, 0)),
                  pl.BlockSpec((tm, LANES), lambda i: (i % tiles_per_seq, 0)),
                  _const_spec(fd.shape)],
        out_specs=[row_spec(w) for w in widths],
        out_shape=[jax.ShapeDtypeStruct((rows, w), dt) for w, dt in zip(widths, dtypes)],
        compiler_params=_cparams(("arbitrary",)),
        name="in_proj",
    )(x2, mod3, g_pre, w_mix, cos_t, sin_t, fd)


def _attn_kernel(*refs, local):
    if local:
        (sink_ref, q_ref, kp_ref, kc_ref, kn_ref, vp_ref, vc_ref, vn_ref, kx_ref, vx_ref, o_ref,
         kwin, vwin) = refs
    else:
        sink_ref, q_ref, kx_ref, vx_ref, o_ref = refs
    blk = A_BLOCK
    nsub = q_ref.shape[1] // blk
    n = pl.program_id(1)
    last = pl.num_programs(1) * nsub - 1
    contract_last = (((1,), (1,)), ((), ()))

    lane_q = lax.broadcasted_iota(jnp.int32, (blk, LANES), 1)
    nloc = 3 * blk
    if local:
        step = nsub * blk
        kwin[0:blk] = kp_ref[0]
        kwin[blk:blk + step] = kc_ref[0]
        kwin[blk + step:2 * blk + step] = kn_ref[0]
        vwin[0:blk] = vp_ref[0]
        vwin[blk:blk + step] = vc_ref[0]
        vwin[blk + step:2 * blk + step] = vn_ref[0]
        row = lax.broadcasted_iota(jnp.int32, (blk, nloc), 0)
        col = lax.broadcasted_iota(jnp.int32, (blk, nloc), 1)
        band = jnp.minimum(col - row, row + 2 * blk - col)

    for sub in range(nsub):
        q0 = sub * blk
        if local:
            g_blk = n * nsub + sub
            seq_lo = blk - g_blk * blk
            seq_hi = blk + (last - g_blk + 1) * blk
            valid = (jnp.minimum(band, jnp.minimum(col - seq_lo, seq_hi - 1 - col)) >= 0)[None]
        for j in range(A_KV_HEADS):
            kcol = slice(j * LANES, (j + 1) * LANES)
            ecol = slice(2 * j * LANES, (2 * j + 1) * LANES)
            ocol = slice((2 * j + 1) * LANES, (2 * j + 2) * LANES)
            qs = jnp.concatenate([q_ref[0, q0:q0 + blk, (4 * j + g) * LANES:(4 * j + g + 1) * LANES]
                                  for g in range(4)], axis=0)
            s = lax.dot_general(qs, kx_ref[0, :, kcol], contract_last, preferred_element_type=F32)
            if local:
                s_loc = lax.dot_general(qs, kwin[q0:q0 + nloc, kcol], contract_last, preferred_element_type=F32)
                s_loc = jnp.where(valid, s_loc.reshape(4, blk, nloc), NEG_INF).reshape(4 * blk, nloc)
                s = jnp.concatenate([s_loc, s], axis=1)
            sink = jnp.concatenate(
                [jnp.full((blk, 1), sink_ref[4 * j + g] * LOG2E, F32) for g in range(4)], axis=0)
            m = jnp.maximum(jnp.max(s, axis=1, keepdims=True), sink)
            pb = jnp.exp2(s - m).astype(BF16)
            e_sink = jnp.exp2(sink - m)

            vcol = slice(2 * j * LANES, (2 * j + 2) * LANES)
            if local:
                o_all = (jnp.dot(pb[:, :nloc], vwin[q0:q0 + nloc, vcol], preferred_element_type=F32)
                         + jnp.dot(pb[:, nloc:], vx_ref[0, :, vcol], preferred_element_type=F32))
            else:
                o_all = jnp.dot(pb, vx_ref[0, :, vcol], preferred_element_type=F32)

            for pair in range(2):
                r0 = 2 * pair * blk
                o_even = o_all[r0:r0 + blk, :LANES]
                o_odd = o_all[r0 + blk:r0 + 2 * blk, LANES:]
                num = jnp.where(lane_q < 64, o_even, o_odd)
                den = (pltpu.roll(jnp.where(lane_q < 64, o_odd, o_even), 64, 1)
                       + jnp.where(lane_q < 64, e_sink[r0:r0 + blk], e_sink[r0 + blk:r0 + 2 * blk]))
                c0 = (2 * j + pair) * LANES
                o_ref[0, q0:q0 + blk, c0:c0 + LANES] = num / den


def _attn_call(sink, q, k, v, kx, vx, *, local):
    b, sq, _ = q.shape
    blk = A_BLOCK
    nq = sq // blk
    nsub = min(ATTN_SUB, nq)
    step = nsub * blk
    lx = kx.shape[1]
    smem = pl.BlockSpec(memory_space=pltpu.SMEM)
    kw, vw = kx.shape[2], vx.shape[2]
    q_spec = pl.BlockSpec((1, step, q.shape[2]), lambda bi, n: (bi, n, 0))
    if local:
        def halo(w):
            return (pl.BlockSpec((1, blk, w), lambda bi, n: (bi, jnp.maximum(n * nsub - 1, 0), 0)),
                    pl.BlockSpec((1, step, w), lambda bi, n: (bi, n, 0)),
                    pl.BlockSpec((1, blk, w), lambda bi, n: (bi, jnp.minimum((n + 1) * nsub, nq - 1), 0)))
        in_specs = [smem, q_spec, *halo(kw), *halo(vw),
                    pl.BlockSpec((1, lx, kw), lambda bi, n: (bi, 0, 0)),
                    pl.BlockSpec((1, lx, vw), lambda bi, n: (bi, 0, 0))]
        args = (sink, q, k, k, k, v, v, v, kx, vx)
        scratch = [pltpu.VMEM((step + 2 * blk, kw), k.dtype), pltpu.VMEM((step + 2 * blk, vw), v.dtype)]
    else:
        in_specs = [smem, q_spec, pl.BlockSpec((1, lx, kw), lambda bi, n: (bi, 0, 0)),
                    pl.BlockSpec((1, lx, vw), lambda bi, n: (bi, 0, 0))]
        args = (sink, q, kx, vx)
        scratch = []
    return pl.pallas_call(
        functools.partial(_attn_kernel, local=local),
        grid=(b, nq // nsub),
        in_specs=in_specs,
        out_specs=pl.BlockSpec((1, step, A_WIDTH), lambda bi, n: (bi, n, 0)),
        out_shape=jax.ShapeDtypeStruct((b, sq, A_WIDTH), F32),
        scratch_shapes=scratch,
        compiler_params=_cparams(("arbitrary", "arbitrary")),
        name="win_attn" if local else "ctx_attn",
    )(*args)


def _head_of(shape, dim):
    return lax.broadcasted_iota(jnp.int32, shape, dim) // HEAD_DIM


def _group_mean(t, avg):
    hi = t.astype(BF16)
    lo = (t - hi.astype(F32)).astype(BF16)
    return (jnp.dot(hi, avg, preferred_element_type=F32) + jnp.dot(lo, avg, preferred_element_type=F32))


def _ret_kernel(*refs, reverse, finalize):
    if finalize:
        lg_ref, lgl_ref, q_ref, k_ref, v_ref, r0_ref, yin_ref, y_ref, rfin_ref, r_scr = refs
    else:
        lg_ref, lgl_ref, q_ref, k_ref, v_ref, r0_ref, y_ref, rfin_ref, r_scr = refs
    c = pl.program_id(0)
    nb, ch, w = q_ref.shape

    @pl.when(c == 0)
    def _():
        r_scr[...] = r0_ref[...]

    lgl = lgl_ref[...]
    pos = lax.broadcasted_iota(jnp.int32, (ch, 1), 0).astype(F32)
    if reverse:
        xi = jnp.exp(lgl * (ch - pos))
        zeta = jnp.exp(lgl * pos)
    else:
        xi = jnp.exp(lgl * (pos + 1.0))
        zeta = jnp.exp(lgl * (ch - 1.0 - pos))
    g_chunk = jnp.exp(lgl * float(ch))
    ri = lax.broadcasted_iota(jnp.int32, (ch, ch), 0)
    ci = lax.broadcasted_iota(jnp.int32, (ch, ch), 1)
    diff = (ci - ri) if reverse else (ri - ci)
    dist = jnp.maximum(diff, 0).astype(F32)
    decays = [jnp.where(diff >= 0, jnp.exp(lg_ref[h] * dist), 0.0) for h in range(R_HEADS)]
    lane_head = _head_of((ch, w), 1)
    same_head = _head_of((w, w), 0) == _head_of((w, w), 1)
    avg = jnp.where(same_head, 1.0 / HEAD_DIM, 0.0).astype(BF16)

    for bi in range(nb):
        q = q_ref[bi]
        k = k_ref[bi]
        v = v_ref[bi]
        r = r_scr[bi]
        cross = jnp.dot((q * xi).astype(BF16), r.astype(BF16), preferred_element_type=F32)
        q4 = jnp.concatenate([jnp.where(lane_head == h, q, 0.0) for h in range(R_HEADS)], axis=0).astype(BF16)
        sc = lax.dot_general(q4, k.astype(BF16), (((1,), (1,)), ((), ())), preferred_element_type=F32)
        s4 = jnp.concatenate([(sc[h * ch:(h + 1) * ch] * decays[h]).astype(BF16) for h in range(R_HEADS)], axis=1)
        v4 = jnp.concatenate([jnp.where(lane_head == h, v, 0.0) for h in range(R_HEADS)], axis=0).astype(BF16)
        y = jnp.dot(s4, v4, preferred_element_type=F32) + cross

        kz = (k * zeta).astype(BF16)
        ktv = lax.dot_general(kz, v.astype(BF16), (((0,), (0,)), ((), ())), preferred_element_type=F32)
        r_new = g_chunk * r + jnp.where(same_head, ktv, 0.0)
        r_scr[bi] = r_new
        rfin_ref[bi] = r_new

        if finalize:
            y = y + yin_ref[bi]
            mu = _group_mean(y, avg)
            d = y - mu
            var = _group_mean(d * d, avg)
            y = d * lax.rsqrt(var + EPS)
        y_ref[bi] = y


def _ret_call(lg, lgl, q, k, v, r0, y_in, *, reverse):
    b, s, w = q.shape
    ch = min(R_STEP, s)
    nc = s // ch
    finalize = y_in is not None
    cidx = (lambda c: (0, nc - 1 - c, 0)) if reverse else (lambda c: (0, c, 0))
    chunk = pl.BlockSpec((b, ch, w), cidx)
    state = pl.BlockSpec((b, w, w), lambda c: (0, 0, 0))
    in_specs = [pl.BlockSpec(memory_space=pltpu.SMEM), pl.BlockSpec((1, w), lambda c: (0, 0)),
                chunk, chunk, chunk, state]
    args = [lg, lgl, q, k, v, r0]
    if finalize:
        in_specs.append(chunk)
        args.append(y_in)
    return pl.pallas_call(
        functools.partial(_ret_kernel, reverse=reverse, finalize=finalize),
        grid=(nc,),
        in_specs=in_specs,
        out_specs=[chunk, state],
        out_shape=[jax.ShapeDtypeStruct((b, s, w), F32), jax.ShapeDtypeStruct((b, w, w), F32)],
        scratch_shapes=[pltpu.VMEM((b, w, w), F32)],
        compiler_params=_cparams(("arbitrary",)),
        name="retention_bwd" if reverse else "retention_fwd",
    )(*args)


def _dft_cs(n, scale):
    a = 2.0 * np.pi * np.outer(np.arange(n), np.arange(n)) / n
    return np.cos(a) * scale, np.sin(a) * scale


def _fourier_rows_kernel(m_ref, tc_ref, ts_ref, z_ref, o_ref):
    rows, cb = z_ref.shape[1], z_ref.shape[2]
    m = m_ref[...].astype(BF16)
    for ci in range(cb):
        z = z_ref[0, :, ci, :]
        zz = jnp.concatenate([z[:, :F_WIDTH], z[:, F_WIDTH:]], axis=0).astype(BF16)
        a = jnp.dot(m, zz, preferred_element_type=F32)
        a_re, a_im = a[:rows], a[rows:]
        tc = jnp.concatenate([tc_ref[ci]] * (F_WIDTH // LANES), axis=1)
        ts = jnp.concatenate([ts_ref[ci]] * (F_WIDTH // LANES), axis=1)
        o_ref[0, :, 0, ci, :] = (a_re * tc + a_im * ts).astype(o_ref.dtype)
        o_ref[0, :, 1, ci, :] = (a_im * tc - a_re * ts).astype(o_ref.dtype)


def _fourier_cols_kernel(g_ref, b_ref, o_ref):
    g = g_ref[...].astype(BF16)
    kt, _, cw, f = b_ref.shape[1:]
    for i in range(kt):
        bm = b_ref[0, i].reshape(2 * cw, f).astype(BF16)
        o_ref[0, :, i, :] = jnp.dot(g, bm, preferred_element_type=F32)


def _fourier_latent(zf, rows):
    b, s, _ = zf.shape
    cw = GRID_W
    c_r, s_r = _dft_cs(rows, rows ** -0.5)
    m1 = jnp.asarray(np.block([[c_r, s_r], [-s_r, c_r]]), F32)
    ang = 2.0 * np.pi * np.outer(np.arange(cw), np.arange(rows)) / s
    tc = jnp.asarray(np.repeat(np.cos(ang)[:, :, None], LANES, axis=2), F32)
    ts = jnp.asarray(np.repeat(np.sin(ang)[:, :, None], LANES, axis=2), F32)
    cb = 16
    bk = pl.pallas_call(
        _fourier_rows_kernel,
        grid=(cw // cb, b),
        in_specs=[_const_spec(m1.shape),
                  pl.BlockSpec((cb, rows, LANES), lambda j, bi: (j, 0, 0)),
                  pl.BlockSpec((cb, rows, LANES), lambda j, bi: (j, 0, 0)),
                  pl.BlockSpec((1, rows, cb, 2 * F_WIDTH), lambda j, bi: (bi, 0, j, 0))],
        out_specs=pl.BlockSpec((1, rows, 2, cb, F_WIDTH), lambda j, bi: (bi, 0, 0, j, 0)),
        out_shape=jax.ShapeDtypeStruct((b, rows, 2, cw, F_WIDTH), F32),
        compiler_params=_cparams(("arbitrary", "arbitrary")),
        name="fourier_rows",
    )(m1, tc, ts, zf.reshape(b, rows, cw, 2 * F_WIDTH))
    c_c, s_c = _dft_cs(cw, cw ** -0.5)
    g = jnp.asarray(np.concatenate([c_c, s_c], axis=1), F32)
    kt = 16
    out = pl.pallas_call(
        _fourier_cols_kernel,
        grid=(b, rows // kt),
        in_specs=[_const_spec(g.shape),
                  pl.BlockSpec((1, kt, 2, cw, F_WIDTH), lambda bi, i: (bi, i, 0, 0, 0))],
        out_specs=pl.BlockSpec((1, cw, kt, F_WIDTH), lambda bi, i: (bi, 0, i, 0)),
        out_shape=jax.ShapeDtypeStruct((b, cw, rows, F_WIDTH), F32),
        compiler_params=_cparams(("arbitrary", "arbitrary")),
        name="fourier_cols",
    )(g, bk)
    return out.reshape(b, s, F_WIDTH)


def _fourier_dense_kernel(m_ref, z_ref, o_ref):
    z = z_ref[0].astype(F32)
    zz = jnp.concatenate([z[:, :F_WIDTH], z[:, F_WIDTH:]], axis=0)
    o_ref[0] = jnp.dot(m_ref[...], zz, preferred_element_type=F32, precision=HI)


def _fourier_dense(zf):
    b, n, _ = zf.shape
    c_n, s_n = _dft_cs(n, n ** -0.5)
    m = jnp.asarray(np.concatenate([c_n, s_n], axis=1), F32)
    return pl.pallas_call(
        _fourier_dense_kernel,
        grid=(b,),
        in_specs=[_const_spec(m.shape), pl.BlockSpec((1, n, 2 * F_WIDTH), lambda bi: (bi, 0, 0))],
        out_specs=pl.BlockSpec((1, n, F_WIDTH), lambda bi: (bi, 0, 0)),
        out_shape=jax.ShapeDtypeStruct((b, n, F_WIDTH), F32),
        compiler_params=_cparams(("arbitrary",)),
        name="fourier_dense",
    )(m, zf)


def _merge_kernel(x_ref, mod_ref, gpre_ref, gpost_ref, ya_ref, wc_ref, wprev_ref, wnext_ref, bb_ref,
                  yr_ref, yf_ref, cw_ref, cb_ref, wz_ref, wm_ref, wo_ref, wout_ref, o_ref, *, tiles_per_seq):
    i = pl.program_id(0)
    x = x_ref[...]
    mod = mod_ref[0]
    h = _modulated_norm(x, gpre_ref[...], mod).astype(BF16)
    tm = x.shape[0]

    wc = wc_ref[...]
    t = i % tiles_per_seq
    prev_row = jnp.where(t > 0, wprev_ref[7:8, :], 0.0)
    next_row = jnp.where(t < tiles_per_seq - 1, wnext_ref[0:1, :], 0.0)
    row = lax.broadcasted_iota(jnp.int32, wc.shape, 0)
    up = jnp.where(row == 0, prev_row, pltpu.roll(wc, 1, 0))
    dn = jnp.where(row == tm - 1, next_row, pltpu.roll(wc, tm - 1, 0))
    conv = up * cw_ref[0:1, :] + wc * cw_ref[1:2, :] + dn * cw_ref[2:3, :] + cb_ref[...]
    yb = bb_ref[...] * conv

    ys = (ya_ref[...], yb, yr_ref[...], yf_ref[...])
    acts = []
    for br in range(N_BRANCH):
        zo, zw = Z_OFFS[br], Z_SIZES[br]
        z = jnp.dot(h, wz_ref[:, zo:zo + zw], preferred_element_type=F32)
        acts.append((ys[br] * (z * _sigmoid(z))).astype(BF16))
    y = jnp.zeros((tm, D_MODEL), F32)
    for c0 in range(0, D_MODEL, MERGE_CHUNK):
        total = jnp.zeros((tm, MERGE_CHUNK), F32)
        for br in range(N_BRANCH):
            zo, zw = Z_OFFS[br], Z_SIZES[br]
            proj = jnp.dot(acts[br], wo_ref[zo:zo + zw, c0:c0 + MERGE_CHUNK], preferred_element_type=F32)
            g0 = br * D_MODEL + c0
            gate = _sigmoid(jnp.dot(h, wm_ref[:, g0:g0 + MERGE_CHUNK], preferred_element_type=F32))
            total = total + gate * proj
        y = y + jnp.dot(total.astype(BF16), wout_ref[c0:c0 + MERGE_CHUNK, :], preferred_element_type=F32)
    ms = jnp.mean(y * y, axis=-1, keepdims=True)
    yn = y * lax.rsqrt(ms + EPS) * gpost_ref[...]
    o_ref[...] = x + mod[:, 2 * D_MODEL:3 * D_MODEL] * yn


def _merge_call(x2, mod3, g_pre, g_post, ya, wc, bb, yr, yf, conv_w, conv_b, wz, wm, wo, wout,
                *, tm, tiles_per_group, tiles_per_seq):
    rows, d = x2.shape
    nt = rows // tm
    hb = tm // 8
    nhb = rows // 8
    row_spec = lambda w: pl.BlockSpec((tm, w), lambda i: (i, 0))
    return pl.pallas_call(
        functools.partial(_merge_kernel, tiles_per_seq=tiles_per_seq),
        grid=(nt,),
        in_specs=[row_spec(d),
                  pl.BlockSpec((1, 1, 3 * d), lambda i: (i // tiles_per_group, 0, 0)),
                  _const_spec((1, d)), _const_spec((1, d)),
                  row_spec(A_WIDTH),
                  row_spec(B_WIDTH),
                  pl.BlockSpec((8, B_WIDTH), lambda i: (jnp.maximum(i * hb - 1, 0), 0)),
                  pl.BlockSpec((8, B_WIDTH), lambda i: (jnp.minimum((i + 1) * hb, nhb - 1), 0)),
                  row_spec(B_WIDTH), row_spec(R_WIDTH), row_spec(F_WIDTH),
                  _const_spec(conv_w.shape), _const_spec(conv_b.shape),
                  _weight_spec(wz.shape), _weight_spec(wm.shape), _weight_spec(wo.shape),
                  _weight_spec(wout.shape)],
        out_specs=row_spec(d),
        out_shape=jax.ShapeDtypeStruct((rows, d), F32),
        compiler_params=_cparams(("arbitrary",)),
        name="merge",
    )(x2, mod3, g_pre, g_post, ya, wc, wc, wc, bb, yr, yf, conv_w, conv_b, wz, wm, wo, wout)


def _rope_tables(n):
    rows = n // GRID_W
    row = jnp.broadcast_to(jnp.arange(rows)[:, None], (rows, GRID_W)).reshape(-1).astype(F32)
    col = jnp.broadcast_to(jnp.arange(GRID_W)[None, :], (rows, GRID_W)).reshape(-1).astype(F32)
    half = HEAD_DIM // 2
    inv = ROPE_BASE ** (-jnp.arange(0, half, 2, dtype=F32) / half)
    ang_r = row[:, None] * inv
    ang_c = col[:, None] * inv
    cos = jnp.concatenate([jnp.cos(ang_r), jnp.cos(ang_r), jnp.cos(ang_c), jnp.cos(ang_c)], axis=1)
    sin = jnp.concatenate([-jnp.sin(ang_r), jnp.sin(ang_r), -jnp.sin(ang_c), jnp.sin(ang_c)], axis=1)
    return jnp.tile(cos, (1, LANES // HEAD_DIM)), jnp.tile(sin, (1, LANES // HEAD_DIM))


def _cols(w, names):
    return jnp.concatenate([w[:, _OFFS[n]:_OFFS[n] + _SIZE[n]] for n in names], axis=1)


def _mix_weights(w):
    parts = []
    for n in MIX_NAMES:
        t = w[:, _OFFS[n]:_OFFS[n] + _SIZE[n]]
        if n in ("a_k", "a_v"):
            t = jnp.concatenate([t[:, h * HEAD_DIM:(h + 1) * HEAD_DIM]
                                 for h in range(A_KV_HEADS) for _ in range(2)], axis=1)
        parts.append(t)
    return jnp.concatenate(parts, axis=1)


def kernel(x, c, ctx, c_ctx, w_ada, b_ada, norm_pre, norm_post, w_in, attn_sink, conv_w, conv_b, ret_decay,
           w_o_attn, w_o_conv, w_o_ret, w_o_fourier, w_out):
    b, s, d = x.shape
    lc = ctx.shape[1]
    depth = w_in.shape[0]
    rows_grid = s // GRID_W

    cv = jnp.zeros((8, d), F32).at[:b].set(c).at[b].set(c_ctx)
    mods = _ada_call(cv, w_ada, b_ada)

    cos_x, sin_x = _rope_tables(s)
    cos_c = jnp.ones((lc, LANES), F32)
    sin_c = jnp.zeros((lc, LANES), F32)

    c64, s64 = _dft_cs(HEAD_DIM, HEAD_DIM ** -0.5)
    eye = np.eye(F_WIDTH // HEAD_DIM)
    fd = jnp.asarray(np.concatenate([np.kron(eye, c64), -np.kron(eye, s64)], axis=1), F32)

    lg_all = jax.nn.log_sigmoid(ret_decay.astype(F32))
    zero_state = jnp.zeros((b, R_WIDTH, R_WIDTH), F32)

    tm_x = 1024
    tm_m = 512
    x2 = x.reshape(b * s, d)
    xc2 = ctx.reshape(b * lc, d)
    for l in range(depth):
        update_ctx = l < depth - 1
        w_l = w_in[l]
        w_mix = _mix_weights(w_l).astype(BF16)
        wz = _cols(w_l, Z_NAMES).astype(BF16)
        wm = w_l[:, _OFFS["merge"]:].astype(BF16)
        wo = jnp.concatenate([w_o_attn[l], w_o_conv[l], w_o_ret[l], w_o_fourier[l]], axis=0).astype(BF16)
        wout = w_out[l].astype(BF16)
        g_pre = norm_pre[l].reshape(1, d)
        g_post = norm_post[l].reshape(1, d)
        mod_x = mods[l, :b].reshape(b, 1, 3 * d)
        mod_c = mods[l, b:b + 1].reshape(1, 1, 3 * d)
        cb = conv_b[l].reshape(1, B_WIDTH)
        lg = lg_all[l]
        lgl = jnp.repeat(lg, HEAD_DIM, axis=1)

        px = _inproj_call(x2, mod_x, g_pre, w_mix, cos_x, sin_x, fd, tm=tm_x, tiles_per_group=s // tm_x)
        pc = _inproj_call(xc2, mod_c, g_pre, w_mix, cos_c, sin_c, fd, tm=lc, tiles_per_group=b)
        qa, ka, va, wcx, bbx, qr, kr, vr, zf = [t.reshape(b, s, -1) for t in px]
        qac, kac, vac, wcc, bbc, qrc, krc, vrc, zfc = [t.reshape(b, lc, -1) for t in pc]

        ya = _attn_call(attn_sink[l], qa, ka, va, kac, vac, local=True)

        ycf, st_f = _ret_call(lg[0], lgl[0:1], qrc, krc, vrc, zero_state, None, reverse=False)
        ycr, st_b = _ret_call(lg[1], lgl[1:2], qrc, krc, vrc, zero_state, ycf, reverse=True)
        yf_, _ = _ret_call(lg[0], lgl[0:1], qr, kr, vr, st_f, None, reverse=False)
        yr, _ = _ret_call(lg[1], lgl[1:2], qr, kr, vr, st_b, yf_, reverse=True)

        yfo = _fourier_latent(zf, rows_grid)

        x2_new = _merge_call(x2, mod_x, g_pre, g_post, ya.reshape(b * s, -1), wcx.reshape(b * s, -1),
                             bbx.reshape(b * s, -1), yr.reshape(b * s, -1), yfo.reshape(b * s, -1),
                             conv_w[l], cb, wz, wm, wo, wout,
                             tm=tm_m, tiles_per_group=s // tm_m, tiles_per_seq=s // tm_m)
        if update_ctx:
            yac = _attn_call(attn_sink[l], qac, None, None, kac, vac, local=False)
            yfc = _fourier_dense(zfc)
            xc2 = _merge_call(xc2, mod_c, g_pre, g_post, yac.reshape(b * lc, -1), wcc.reshape(b * lc, -1),
                              bbc.reshape(b * lc, -1), ycr.reshape(b * lc, -1), yfc.reshape(b * lc, -1),
                              conv_w[l], cb, wz, wm, wo, wout,
                              tm=lc, tiles_per_group=b, tiles_per_seq=1)
        x2 = x2_new
    return x2.reshape(b, s, d)
```

```python
import functools
import math

import numpy as np
import jax
import jax.numpy as jnp
from jax import lax
from jax.experimental import pallas as pl
from jax.experimental.pallas import tpu as pltpu

D_MODEL = 1024
GRID_W = 64
HEAD_DIM = 64
EPS = 1e-6
NEG_INF = -1e30
A_HEADS = 8
A_KV_HEADS = 2
A_BLOCK = 128
A_WIDTH = A_HEADS * HEAD_DIM
A_KV_WIDTH = A_KV_HEADS * HEAD_DIM
ROPE_BASE = 10000.0
B_WIDTH = 256
R_HEADS = 4
R_WIDTH = R_HEADS * HEAD_DIM
R_STEP = 256
F_WIDTH = 256
N_BRANCH = 4

IN_NAMES = ("a_q", "a_k", "a_v", "a_z", "b_u", "b_b", "b_c", "b_z", "r_q", "r_k", "r_v", "r_z", "f_u", "f_z", "merge")
IN_SIZES = (A_WIDTH, A_KV_WIDTH, A_KV_WIDTH, A_WIDTH, B_WIDTH, B_WIDTH, B_WIDTH, B_WIDTH,
            R_WIDTH, R_WIDTH, R_WIDTH, R_WIDTH, F_WIDTH, F_WIDTH, N_BRANCH * D_MODEL)
_OFFS = dict(zip(IN_NAMES, np.cumsum((0,) + IN_SIZES)[:-1].tolist()))
_SIZE = dict(zip(IN_NAMES, IN_SIZES))

MIX_NAMES = ("a_q", "a_k", "a_v", "b_u", "b_b", "b_c", "r_q", "r_k", "r_v", "f_u")
MIX_SIZE = dict({n: _SIZE[n] for n in MIX_NAMES}, a_k=2 * A_KV_WIDTH, a_v=2 * A_KV_WIDTH)
MIX_OFFS = dict(zip(MIX_NAMES, np.cumsum((0,) + tuple(MIX_SIZE[n] for n in MIX_NAMES))[:-1].tolist()))
Z_NAMES = ("a_z", "b_z", "r_z", "f_z")
Z_SIZES = tuple(_SIZE[n] for n in Z_NAMES)
Z_OFFS = tuple(np.cumsum((0,) + Z_SIZES)[:-1].tolist())
Z_WIDTH = sum(Z_SIZES)

MERGE_CHUNK = 256
ATTN_SUB = 8
LOG2E = math.log2(math.e)
LANES = 128
VMEM_LIMIT = 56 * 1024 * 1024

BF16 = jnp.bfloat16
F32 = jnp.float32
HI = lax.Precision.HIGHEST


def _cparams(sem):
    return pltpu.CompilerParams(dimension_semantics=sem, vmem_limit_bytes=VMEM_LIMIT)


def _const_spec(shape):
    nd = len(shape)
    return pl.BlockSpec(shape, lambda *_: (0,) * nd)


def _weight_spec(shape):
    nd = len(shape)
    return pl.BlockSpec(shape, lambda *_: (0,) * nd, pipeline_mode=pl.Buffered(1))


def _sigmoid(v):
    return 1.0 / (1.0 + jnp.exp(-v))


def _ada_kernel(cv_ref, w_ref, b_ref, o_ref):
    cv = cv_ref[...]
    s = cv * _sigmoid(cv)
    o_ref[0] = jnp.dot(s, w_ref[0], preferred_element_type=F32, precision=HI) + b_ref[0]


def _ada_call(cv, w_ada, b_ada):
    depth, d, d3 = w_ada.shape
    tn = 1024
    return pl.pallas_call(
        _ada_kernel,
        grid=(depth, d3 // tn),
        in_specs=[pl.BlockSpec((8, d), lambda l, j: (0, 0)),
                  pl.BlockSpec((1, d, tn), lambda l, j: (l, 0, j)),
                  pl.BlockSpec((1, 1, tn), lambda l, j: (l, 0, j))],
        out_specs=pl.BlockSpec((1, 8, tn), lambda l, j: (l, 0, j)),
        out_shape=jax.ShapeDtypeStruct((depth, 8, d3), F32),
        compiler_params=_cparams(("arbitrary", "arbitrary")),
        name="ada_mod",
    )(cv, w_ada, b_ada.reshape(depth, 1, d3))


def _modulated_norm(x, g, mod):
    ms = jnp.mean(x * x, axis=-1, keepdims=True)
    y = x * lax.rsqrt(ms + EPS) * g
    return y * (1.0 + mod[:, D_MODEL:2 * D_MODEL]) + mod[:, 0:D_MODEL]


def _rope(t, cos, sin_signed, first_half):
    outs = []
    for j in range(t.shape[1] // LANES):
        tj = t[:, j * LANES:(j + 1) * LANES]
        partner = jnp.where(first_half, pltpu.roll(tj, LANES - 16, 1), pltpu.roll(tj, 16, 1))
        outs.append(tj * cos + partner * sin_signed)
    return outs[0] if len(outs) == 1 else jnp.concatenate(outs, axis=1)


def _inproj_kernel(x_ref, mod_ref, g_ref, w_ref, cos_ref, sin_ref, fd_ref,
                   qa_ref, ka_ref, va_ref, wc_ref, bb_ref, qr_ref, kr_ref, vr_ref, zf_ref):
    h = _modulated_norm(x_ref[...], g_ref[...], mod_ref[0]).astype(BF16)

    def proj(name):
        o = MIX_OFFS[name]
        return jnp.dot(h, w_ref[:, o:o + MIX_SIZE[name]], preferred_element_type=F32)

    cos = cos_ref[...]
    sin = sin_ref[...]
    lane = lax.broadcasted_iota(jnp.int32, cos.shape, 1)
    first_half = (lane % 32) < 16
    low_half = lane < HEAD_DIM
    k_scale = HEAD_DIM ** -0.5
    q = _rope(proj("a_q"), cos, sin, first_half) * (k_scale * LOG2E)
    for p in range(A_WIDTH // LANES):
        qp = q[:, p * LANES:(p + 1) * LANES]
        qa_ref[:, (2 * p) * LANES:(2 * p + 1) * LANES] = jnp.where(low_half, qp, 0.0).astype(BF16)
        qa_ref[:, (2 * p + 1) * LANES:(2 * p + 2) * LANES] = jnp.where(low_half, 0.0, qp).astype(BF16)
    ka_ref[...] = _rope(proj("a_k"), cos, sin, first_half).astype(BF16)
    v = proj("a_v")
    for j in range(A_KV_HEADS):
        vj = v[:, j * LANES:(j + 1) * LANES]
        va_ref[:, (2 * j) * LANES:(2 * j + 1) * LANES] = jnp.where(low_half, vj, 1.0).astype(BF16)
        va_ref[:, (2 * j + 1) * LANES:(2 * j + 2) * LANES] = jnp.where(low_half, 1.0, vj).astype(BF16)
    wc_ref[...] = proj("b_c") * proj("b_u")
    bb_ref[...] = proj("b_b")
    qr_ref[...] = _rope(proj("r_q"), cos, sin, first_half)
    kr_ref[...] = _rope(proj("r_k"), cos, sin, first_half) * k_scale
    vr_ref[...] = proj("r_v")
    zf_ref[...] = jnp.dot(proj("f_u").astype(BF16), fd_ref[...].astype(BF16), preferred_element_type=F32)


def _inproj_call(x2, mod3, g_pre, w_mix, cos_t, sin_t, fd, *, tm, tiles_per_group):
    rows, d = x2.shape
    nt = rows // tm
    tiles_per_seq = cos_t.shape[0] // tm
    widths = (2 * A_WIDTH, 2 * A_KV_WIDTH, 4 * A_KV_WIDTH, B_WIDTH, B_WIDTH, R_WIDTH, R_WIDTH, R_WIDTH, 2 * F_WIDTH)
    dtypes = (BF16, BF16, BF16, F32, F32, F32, F32, F32, F32)
    row_spec = lambda w: pl.BlockSpec((tm, w), lambda i: (i, 0))
    return pl.pallas_call(
        _inproj_kernel,
        grid=(nt,),
        in_specs=[row_spec(d),
                  pl.BlockSpec((1, 1, 3 * d), lambda i: (i // tiles_per_group, 0, 0)),
                  _const_spec((1, d)),
                  _const_spec(w_mix.shape),
                  pl.BlockSpec((tm, LANES), lambda i: (i % tiles_per_seq, 0)),
                  pl.BlockSpec((tm, LANES), lambda i: (i % tiles_per_seq, 0)),
                  _const_spec(fd.shape)],
        out_specs=[row_spec(w) for w in widths],
        out_shape=[jax.ShapeDtypeStruct((rows, w), dt) for w, dt in zip(widths, dtypes)],
        compiler_params=_cparams(("arbitrary",)),
        name="in_proj",
    )(x2, mod3, g_pre, w_mix, cos_t, sin_t, fd)


def _attn_kernel(*refs, local):
    if local:
        (sink_ref, q_ref, kp_ref, kc_ref, kn_ref, vp_ref, vc_ref, vn_ref, kx_ref, vx_ref, o_ref,
         kwin, vwin) = refs
    else:
        sink_ref, q_ref, kx_ref, vx_ref, o_ref = refs
    blk = A_BLOCK
    nsub = q_ref.shape[1] // blk
    n = pl.program_id(1)
    last = pl.num_programs(1) * nsub - 1
    contract_last = (((1,), (1,)), ((), ()))

    lane_q = lax.broadcasted_iota(jnp.int32, (blk, LANES), 1)
    nloc = 3 * blk
    if local:
        step = nsub * blk
        kwin[0:blk] = kp_ref[0]
        kwin[blk:blk + step] = kc_ref[0]
        kwin[blk + step:2 * blk + step] = kn_ref[0]
        vwin[0:blk] = vp_ref[0]
        vwin[blk:blk + step] = vc_ref[0]
        vwin[blk + step:2 * blk + step] = vn_ref[0]
        row = lax.broadcasted_iota(jnp.int32, (blk, nloc), 0)
        col = lax.broadcasted_iota(jnp.int32, (blk, nloc), 1)
        band = jnp.minimum(col - row, row + 2 * blk - col)

    for sub in range(nsub):
        q0 = sub * blk
        if local:
            g_blk = n * nsub + sub
            seq_lo = blk - g_blk * blk
            seq_hi = blk + (last - g_blk + 1) * blk
            valid = (jnp.minimum(band, jnp.minimum(col - seq_lo, seq_hi - 1 - col)) >= 0)[None]
        for j in range(A_KV_HEADS):
            kcol = slice(j * LANES, (j + 1) * LANES)
            ecol = slice(2 * j * LANES, (2 * j + 1) * LANES)
            ocol = slice((2 * j + 1) * LANES, (2 * j + 2) * LANES)
            qs = jnp.concatenate([q_ref[0, q0:q0 + blk, (4 * j + g) * LANES:(4 * j + g + 1) * LANES]
                                  for g in range(4)], axis=0)
            s = lax.dot_general(qs, kx_ref[0, :, kcol], contract_last, preferred_element_type=F32)
            if local:
                s_loc = lax.dot_general(qs, kwin[q0:q0 + nloc, kcol], contract_last, preferred_element_type=F32)
                s_loc = jnp.where(valid, s_loc.reshape(4, blk, nloc), NEG_INF).reshape(4 * blk, nloc)
                s = jnp.concatenate([s_loc, s], axis=1)
            sink = jnp.concatenate(
                [jnp.full((blk, 1), sink_ref[4 * j + g] * LOG2E, F32) for g in range(4)], axis=0)
            m = jnp.maximum(jnp.max(s, axis=1, keepdims=True), sink)
            pb = jnp.exp2((s - m).astype(BF16))
            e_sink = jnp.exp2(sink - m)

            vcol = slice(2 * j * LANES, (2 * j + 2) * LANES)
            if local:
                o_all = (jnp.dot(pb[:, :nloc], vwin[q0:q0 + nloc, vcol], preferred_element_type=F32)
                         + jnp.dot(pb[:, nloc:], vx_ref[0, :, vcol], preferred_element_type=F32))
            else:
                o_all = jnp.dot(pb, vx_ref[0, :, vcol], preferred_element_type=F32)

            for pair in range(2):
                r0 = 2 * pair * blk
                o_even = o_all[r0:r0 + blk, :LANES]
                o_odd = o_all[r0 + blk:r0 + 2 * blk, LANES:]
                num = jnp.where(lane_q < 64, o_even, o_odd)
                den = (pltpu.roll(jnp.where(lane_q < 64, o_odd, o_even), 64, 1)
                       + jnp.where(lane_q < 64, e_sink[r0:r0 + blk], e_sink[r0 + blk:r0 + 2 * blk]))
                c0 = (2 * j + pair) * LANES
                o_ref[0, q0:q0 + blk, c0:c0 + LANES] = num / den


def _attn_call(sink, q, k, v, kx, vx, *, local):
    b, sq, _ = q.shape
    blk = A_BLOCK
    nq = sq // blk
    nsub = min(ATTN_SUB, nq)
    step = nsub * blk
    lx = kx.shape[1]
    smem = pl.BlockSpec(memory_space=pltpu.SMEM)
    kw, vw = kx.shape[2], vx.shape[2]
    q_spec = pl.BlockSpec((1, step, q.shape[2]), lambda bi, n: (bi, n, 0))
    if local:
        def halo(w):
            return (pl.BlockSpec((1, blk, w), lambda bi, n: (bi, jnp.maximum(n * nsub - 1, 0), 0)),
                    pl.BlockSpec((1, step, w), lambda bi, n: (bi, n, 0)),
                    pl.BlockSpec((1, blk, w), lambda bi, n: (bi, jnp.minimum((n + 1) * nsub, nq - 1), 0)))
        in_specs = [smem, q_spec, *halo(kw), *halo(vw),
                    pl.BlockSpec((1, lx, kw), lambda bi, n: (bi, 0, 0)),
                    pl.BlockSpec((1, lx, vw), lambda bi, n: (bi, 0, 0))]
        args = (sink, q, k, k, k, v, v, v, kx, vx)
        scratch = [pltpu.VMEM((step + 2 * blk, kw), k.dtype), pltpu.VMEM((step + 2 * blk, vw), v.dtype)]
    else:
        in_specs = [smem, q_spec, pl.BlockSpec((1, lx, kw), lambda bi, n: (bi, 0, 0)),
                    pl.BlockSpec((1, lx, vw), lambda bi, n: (bi, 0, 0))]
        args = (sink, q, kx, vx)
        scratch = []
    return pl.pallas_call(
        functools.partial(_attn_kernel, local=local),
        grid=(b, nq // nsub),
        in_specs=in_specs,
        out_specs=pl.BlockSpec((1, step, A_WIDTH), lambda bi, n: (bi, n, 0)),
        out_shape=jax.ShapeDtypeStruct((b, sq, A_WIDTH), F32),
        scratch_shapes=scratch,
        compiler_params=_cparams(("arbitrary", "arbitrary")),
        name="win_attn" if local else "ctx_attn",
    )(*args)


def _head_of(shape, dim):
    return lax.broadcasted_iota(jnp.int32, shape, dim) // HEAD_DIM


def _group_mean(t, avg):
    hi = t.astype(BF16)
    lo = (t - hi.astype(F32)).astype(BF16)
    return (jnp.dot(hi, avg, preferred_element_type=F32) + jnp.dot(lo, avg, preferred_element_type=F32))


def _ret_kernel(*refs, reverse, finalize):
    if finalize:
        lg_ref, lgl_ref, q_ref, k_ref, v_ref, r0_ref, yin_ref, y_ref, rfin_ref, r_scr = refs
    else:
        lg_ref, lgl_ref, q_ref, k_ref, v_ref, r0_ref, y_ref, rfin_ref, r_scr = refs
    c = pl.program_id(0)
    nb, ch, w = q_ref.shape

    @pl.when(c == 0)
    def _():
        r_scr[...] = r0_ref[...]

    lgl = lgl_ref[...]
    pos = lax.broadcasted_iota(jnp.int32, (ch, 1), 0).astype(F32)
    if reverse:
        xi = jnp.exp(lgl * (ch - pos))
        zeta = jnp.exp(lgl * pos)
    else:
        xi = jnp.exp(lgl * (pos + 1.0))
        zeta = jnp.exp(lgl * (ch - 1.0 - pos))
    g_chunk = jnp.exp(lgl * float(ch))
    ri = lax.broadcasted_iota(jnp.int32, (ch, ch), 0)
    ci = lax.broadcasted_iota(jnp.int32, (ch, ch), 1)
    diff = (ci - ri) if reverse else (ri - ci)
    dist = jnp.maximum(diff, 0).astype(F32)
    decays = [jnp.where(diff >= 0, jnp.exp(lg_ref[h] * dist), 0.0) for h in range(R_HEADS)]
    lane_head = _head_of((ch, w), 1)
    same_head = _head_of((w, w), 0) == _head_of((w, w), 1)
    avg = jnp.where(same_head, 1.0 / HEAD_DIM, 0.0).astype(BF16)

    for bi in range(nb):
        q = q_ref[bi]
        k = k_ref[bi]
        v = v_ref[bi]
        r = r_scr[bi]
        cross = jnp.dot((q * xi).astype(BF16), r.astype(BF16), preferred_element_type=F32)
        q4 = jnp.concatenate([jnp.where(lane_head == h, q, 0.0) for h in range(R_HEADS)], axis=0).astype(BF16)
        sc = lax.dot_general(q4, k.astype(BF16), (((1,), (1,)), ((), ())), preferred_element_type=F32)
        s4 = jnp.concatenate([(sc[h * ch:(h + 1) * ch] * decays[h]).astype(BF16) for h in range(R_HEADS)], axis=1)
        v4 = jnp.concatenate([jnp.where(lane_head == h, v, 0.0) for h in range(R_HEADS)], axis=0).astype(BF16)
        y = jnp.dot(s4, v4, preferred_element_type=F32) + cross

        kz = (k * zeta).astype(BF16)
        ktv = lax.dot_general(kz, v.astype(BF16), (((0,), (0,)), ((), ())), preferred_element_type=F32)
        r_new = g_chunk * r + jnp.where(same_head, ktv, 0.0)
        r_scr[bi] = r_new
        rfin_ref[bi] = r_new

        if finalize:
            y = y + yin_ref[bi]
            mu = _group_mean(y, avg)
            d = y - mu
            var = _group_mean(d * d, avg)
            y = d * lax.rsqrt(var + EPS)
        y_ref[bi] = y


def _ret_call(lg, lgl, q, k, v, r0, y_in, *, reverse):
    b, s, w = q.shape
    ch = min(R_STEP, s)
    nc = s // ch
    finalize = y_in is not None
    cidx = (lambda c: (0, nc - 1 - c, 0)) if reverse else (lambda c: (0, c, 0))
    chunk = pl.BlockSpec((b, ch, w), cidx)
    state = pl.BlockSpec((b, w, w), lambda c: (0, 0, 0))
    in_specs = [pl.BlockSpec(memory_space=pltpu.SMEM), pl.BlockSpec((1, w), lambda c: (0, 0)),
                chunk, chunk, chunk, state]
    args = [lg, lgl, q, k, v, r0]
    if finalize:
        in_specs.append(chunk)
        args.append(y_in)
    return pl.pallas_call(
        functools.partial(_ret_kernel, reverse=reverse, finalize=finalize),
        grid=(nc,),
        in_specs=in_specs,
        out_specs=[chunk, state],
        out_shape=[jax.ShapeDtypeStruct((b, s, w), F32), jax.ShapeDtypeStruct((b, w, w), F32)],
        scratch_shapes=[pltpu.VMEM((b, w, w), F32)],
        compiler_params=_cparams(("arbitrary",)),
        name="retention_bwd" if reverse else "retention_fwd",
    )(*args)


def _dft_cs(n, scale):
    a = 2.0 * np.pi * np.outer(np.arange(n), np.arange(n)) / n
    return np.cos(a) * scale, np.sin(a) * scale


def _fourier_rows_kernel(m_ref, tc_ref, ts_ref, z_ref, o_ref):
    rows, cb = z_ref.shape[1], z_ref.shape[2]
    m = m_ref[...].astype(BF16)
    for ci in range(cb):
        z = z_ref[0, :, ci, :]
        zz = jnp.concatenate([z[:, :F_WIDTH], z[:, F_WIDTH:]], axis=0).astype(BF16)
        a = jnp.dot(m, zz, preferred_element_type=F32)
        a_re, a_im = a[:rows], a[rows:]
        tc = jnp.concatenate([tc_ref[ci]] * (F_WIDTH // LANES), axis=1)
        ts = jnp.concatenate([ts_ref[ci]] * (F_WIDTH // LANES), axis=1)
        o_ref[0, :, 0, ci, :] = (a_re * tc + a_im * ts).astype(o_ref.dtype)
        o_ref[0, :, 1, ci, :] = (a_im * tc - a_re * ts).astype(o_ref.dtype)


def _fourier_cols_kernel(g_ref, b_ref, o_ref):
    g = g_ref[...].astype(BF16)
    kt, _, cw, f = b_ref.shape[1:]
    for i in range(kt):
        bm = b_ref[0, i].reshape(2 * cw, f).astype(BF16)
        o_ref[0, :, i, :] = jnp.dot(g, bm, preferred_element_type=F32)


def _fourier_latent(zf, rows):
    b, s, _ = zf.shape
    cw = GRID_W
    c_r, s_r = _dft_cs(rows, rows ** -0.5)
    m1 = jnp.asarray(np.block([[c_r, s_r], [-s_r, c_r]]), F32)
    ang = 2.0 * np.pi * np.outer(np.arange(cw), np.arange(rows)) / s
    tc = jnp.asarray(np.repeat(np.cos(ang)[:, :, None], LANES, axis=2), F32)
    ts = jnp.asarray(np.repeat(np.sin(ang)[:, :, None], LANES, axis=2), F32)
    cb = 16
    bk = pl.pallas_call(
        _fourier_rows_kernel,
        grid=(cw // cb, b),
        in_specs=[_const_spec(m1.shape),
                  pl.BlockSpec((cb, rows, LANES), lambda j, bi: (j, 0, 0)),
                  pl.BlockSpec((cb, rows, LANES), lambda j, bi: (j, 0, 0)),
                  pl.BlockSpec((1, rows, cb, 2 * F_WIDTH), lambda j, bi: (bi, 0, j, 0))],
        out_specs=pl.BlockSpec((1, rows, 2, cb, F_WIDTH), lambda j, bi: (bi, 0, 0, j, 0)),
        out_shape=jax.ShapeDtypeStruct((b, rows, 2, cw, F_WIDTH), F32),
        compiler_params=_cparams(("arbitrary", "arbitrary")),
        name="fourier_rows",
    )(m1, tc, ts, zf.reshape(b, rows, cw, 2 * F_WIDTH))
    c_c, s_c = _dft_cs(cw, cw ** -0.5)
    g = jnp.asarray(np.concatenate([c_c, s_c], axis=1), F32)
    kt = 16
    out = pl.pallas_call(
        _fourier_cols_kernel,
        grid=(b, rows // kt),
        in_specs=[_const_spec(g.shape),
                  pl.BlockSpec((1, kt, 2, cw, F_WIDTH), lambda bi, i: (bi, i, 0, 0, 0))],
        out_specs=pl.BlockSpec((1, cw, kt, F_WIDTH), lambda bi, i: (bi, 0, i, 0)),
        out_shape=jax.ShapeDtypeStruct((b, cw, rows, F_WIDTH), F32),
        compiler_params=_cparams(("arbitrary", "arbitrary")),
        name="fourier_cols",
    )(g, bk)
    return out.reshape(b, s, F_WIDTH)


def _fourier_dense_kernel(m_ref, z_ref, o_ref):
    z = z_ref[0].astype(F32)
    zz = jnp.concatenate([z[:, :F_WIDTH], z[:, F_WIDTH:]], axis=0)
    o_ref[0] = jnp.dot(m_ref[...], zz, preferred_element_type=F32, precision=HI)


def _fourier_dense(zf):
    b, n, _ = zf.shape
    c_n, s_n = _dft_cs(n, n ** -0.5)
    m = jnp.asarray(np.concatenate([c_n, s_n], axis=1), F32)
    return pl.pallas_call(
        _fourier_dense_kernel,
        grid=(b,),
        in_specs=[_const_spec(m.shape), pl.BlockSpec((1, n, 2 * F_WIDTH), lambda bi: (bi, 0, 0))],
        out_specs=pl.BlockSpec((1, n, F_WIDTH), lambda bi: (bi, 0, 0)),
        out_shape=jax.ShapeDtypeStruct((b, n, F_WIDTH), F32),
        compiler_params=_cparams(("arbitrary",)),
        name="fourier_dense",
    )(m, zf)


def _merge_kernel(x_ref, mod_ref, gpre_ref, gpost_ref, ya_ref, wc_ref, wprev_ref, wnext_ref, bb_ref,
                  yr_ref, yf_ref, cw_ref, cb_ref, wz_ref, wm_ref, wo_ref, wout_ref, o_ref, *, tiles_per_seq):
    i = pl.program_id(0)
    x = x_ref[...]
    mod = mod_ref[0]
    h = _modulated_norm(x, gpre_ref[...], mod).astype(BF16)
    tm = x.shape[0]

    wc = wc_ref[...]
    t = i % tiles_per_seq
    prev_row = jnp.where(t > 0, wprev_ref[7:8, :], 0.0)
    next_row = jnp.where(t < tiles_per_seq - 1, wnext_ref[0:1, :], 0.0)
    row = lax.broadcasted_iota(jnp.int32, wc.shape, 0)
    up = jnp.where(row == 0, prev_row, pltpu.roll(wc, 1, 0))
    dn = jnp.where(row == tm - 1, next_row, pltpu.roll(wc, tm - 1, 0))
    conv = up * cw_ref[0:1, :] + wc * cw_ref[1:2, :] + dn * cw_ref[2:3, :] + cb_ref[...]
    yb = bb_ref[...] * conv

    ys = (ya_ref[...], yb, yr_ref[...], yf_ref[...])
    acts = []
    for br in range(N_BRANCH):
        zo, zw = Z_OFFS[br], Z_SIZES[br]
        z = jnp.dot(h, wz_ref[:, zo:zo + zw], preferred_element_type=F32)
        acts.append((ys[br] * (z * _sigmoid(z))).astype(BF16))
    y = jnp.zeros((tm, D_MODEL), F32)
    for c0 in range(0, D_MODEL, MERGE_CHUNK):
        total = jnp.zeros((tm, MERGE_CHUNK), F32)
        for br in range(N_BRANCH):
            zo, zw = Z_OFFS[br], Z_SIZES[br]
            proj = jnp.dot(acts[br], wo_ref[zo:zo + zw, c0:c0 + MERGE_CHUNK], preferred_element_type=F32)
            g0 = br * D_MODEL + c0
            gate = _sigmoid(jnp.dot(h, wm_ref[:, g0:g0 + MERGE_CHUNK], preferred_element_type=F32))
            total = total + gate * proj
        y = y + jnp.dot(total.astype(BF16), wout_ref[c0:c0 + MERGE_CHUNK, :], preferred_element_type=F32)
    ms = jnp.mean(y * y, axis=-1, keepdims=True)
    yn = y * lax.rsqrt(ms + EPS) * gpost_ref[...]
    o_ref[...] = x + mod[:, 2 * D_MODEL:3 * D_MODEL] * yn


def _merge_call(x2, mod3, g_pre, g_post, ya, wc, bb, yr, yf, conv_w, conv_b, wz, wm, wo, wout,
                *, tm, tiles_per_group, tiles_per_seq):
    rows, d = x2.shape
    nt = rows // tm
    hb = tm // 8
    nhb = rows // 8
    row_spec = lambda w: pl.BlockSpec((tm, w), lambda i: (i, 0))
    return pl.pallas_call(
        functools.partial(_merge_kernel, tiles_per_seq=tiles_per_seq),
        grid=(nt,),
        in_specs=[row_spec(d),
                  pl.BlockSpec((1, 1, 3 * d), lambda i: (i // tiles_per_group, 0, 0)),
                  _const_spec((1, d)), _const_spec((1, d)),
                  row_spec(A_WIDTH),
                  row_spec(B_WIDTH),
                  pl.BlockSpec((8, B_WIDTH), lambda i: (jnp.maximum(i * hb - 1, 0), 0)),
                  pl.BlockSpec((8, B_WIDTH), lambda i: (jnp.minimum((i + 1) * hb, nhb - 1), 0)),
                  row_spec(B_WIDTH), row_spec(R_WIDTH), row_spec(F_WIDTH),
                  _const_spec(conv_w.shape), _const_spec(conv_b.shape),
                  _weight_spec(wz.shape), _weight_spec(wm.shape), _weight_spec(wo.shape),
                  _weight_spec(wout.shape)],
        out_specs=row_spec(d),
        out_shape=jax.ShapeDtypeStruct((rows, d), F32),
        compiler_params=_cparams(("arbitrary",)),
        name="merge",
    )(x2, mod3, g_pre, g_post, ya, wc, wc, wc, bb, yr, yf, conv_w, conv_b, wz, wm, wo, wout)


def _rope_tables(n):
    rows = n // GRID_W
    row = jnp.broadcast_to(jnp.arange(rows)[:, None], (rows, GRID_W)).reshape(-1).astype(F32)
    col = jnp.broadcast_to(jnp.arange(GRID_W)[None, :], (rows, GRID_W)).reshape(-1).astype(F32)
    half = HEAD_DIM // 2
    inv = ROPE_BASE ** (-jnp.arange(0, half, 2, dtype=F32) / half)
    ang_r = row[:, None] * inv
    ang_c = col[:, None] * inv
    cos = jnp.concatenate([jnp.cos(ang_r), jnp.cos(ang_r), jnp.cos(ang_c), jnp.cos(ang_c)], axis=1)
    sin = jnp.concatenate([-jnp.sin(ang_r), jnp.sin(ang_r), -jnp.sin(ang_c), jnp.sin(ang_c)], axis=1)
    return jnp.tile(cos, (1, LANES // HEAD_DIM)), jnp.tile(sin, (1, LANES // HEAD_DIM))


def _cols(w, names):
    return jnp.concatenate([w[:, _OFFS[n]:_OFFS[n] + _SIZE[n]] for n in names], axis=1)


def _mix_weights(w):
    parts = []
    for n in MIX_NAMES:
        t = w[:, _OFFS[n]:_OFFS[n] + _SIZE[n]]
        if n in ("a_k", "a_v"):
            t = jnp.concatenate([t[:, h * HEAD_DIM:(h + 1) * HEAD_DIM]
                                 for h in range(A_KV_HEADS) for _ in range(2)], axis=1)
        parts.append(t)
    return jnp.concatenate(parts, axis=1)


def kernel(x, c, ctx, c_ctx, w_ada, b_ada, norm_pre, norm_post, w_in, attn_sink, conv_w, conv_b, ret_decay,
           w_o_attn, w_o_conv, w_o_ret, w_o_fourier, w_out):
    b, s, d = x.shape
    lc = ctx.shape[1]
    depth = w_in.shape[0]
    rows_grid = s // GRID_W

    cv = jnp.zeros((8, d), F32).at[:b].set(c).at[b].set(c_ctx)
    mods = _ada_call(cv, w_ada, b_ada)

    cos_x, sin_x = _rope_tables(s)
    cos_c = jnp.ones((lc, LANES), F32)
    sin_c = jnp.zeros((lc, LANES), F32)

    c64, s64 = _dft_cs(HEAD_DIM, HEAD_DIM ** -0.5)
    eye = np.eye(F_WIDTH // HEAD_DIM)
    fd = jnp.asarray(np.concatenate([np.kron(eye, c64), -np.kron(eye, s64)], axis=1), F32)

    lg_all = jax.nn.log_sigmoid(ret_decay.astype(F32))
    zero_state = jnp.zeros((b, R_WIDTH, R_WIDTH), F32)

    tm_x = 1024
    tm_m = 1024
    x2 = x.reshape(b * s, d)
    xc2 = ctx.reshape(b * lc, d)
    for l in range(depth):
        update_ctx = l < depth - 1
        w_l = w_in[l]
        w_mix = _mix_weights(w_l).astype(BF16)
        wz = _cols(w_l, Z_NAMES).astype(BF16)
        wm = w_l[:, _OFFS["merge"]:].astype(BF16)
        wo = jnp.concatenate([w_o_attn[l], w_o_conv[l], w_o_ret[l], w_o_fourier[l]], axis=0).astype(BF16)
        wout = w_out[l].astype(BF16)
        g_pre = norm_pre[l].reshape(1, d)
        g_post = norm_post[l].reshape(1, d)
        mod_x = mods[l, :b].reshape(b, 1, 3 * d)
        mod_c = mods[l, b:b + 1].reshape(1, 1, 3 * d)
        cb = conv_b[l].reshape(1, B_WIDTH)
        lg = lg_all[l]
        lgl = jnp.repeat(lg, HEAD_DIM, axis=1)

        px = _inproj_call(x2, mod_x, g_pre, w_mix, cos_x, sin_x, fd, tm=tm_x, tiles_per_group=s // tm_x)
        pc = _inproj_call(xc2, mod_c, g_pre, w_mix, cos_c, sin_c, fd, tm=lc, tiles_per_group=b)
        qa, ka, va, wcx, bbx, qr, kr, vr, zf = [t.reshape(b, s, -1) for t in px]
        qac, kac, vac, wcc, bbc, qrc, krc, vrc, zfc = [t.reshape(b, lc, -1) for t in pc]

        ya = _attn_call(attn_sink[l], qa, ka, va, kac, vac, local=True)

        ycf, st_f = _ret_call(lg[0], lgl[0:1], qrc, krc, vrc, zero_state, None, reverse=False)
        ycr, st_b = _ret_call(lg[1], lgl[1:2], qrc, krc, vrc, zero_state, ycf, reverse=True)
        yf_, _ = _ret_call(lg[0], lgl[0:1], qr, kr, vr, st_f, None, reverse=False)
        yr, _ = _ret_call(lg[1], lgl[1:2], qr, kr, vr, st_b, yf_, reverse=True)

        yfo = _fourier_latent(zf, rows_grid)

        x2_new = _merge_call(x2, mod_x, g_pre, g_post, ya.reshape(b * s, -1), wcx.reshape(b * s, -1),
                             bbx.reshape(b * s, -1), yr.reshape(b * s, -1), yfo.reshape(b * s, -1),
                             conv_w[l], cb, wz, wm, wo, wout,
                             tm=tm_m, tiles_per_group=s // tm_m, tiles_per_seq=s // tm_m)
        if update_ctx:
            yac = _attn_call(attn_sink[l], qac, None, None, kac, vac, local=False)
            yfc = _fourier_dense(zfc)
            xc2 = _merge_call(xc2, mod_c, g_pre, g_post, yac.reshape(b * lc, -1), wcc.reshape(b * lc, -1),
                              bbc.reshape(b * lc, -1), ycr.reshape(b * lc, -1), yfc.reshape(b * lc, -1),
                              conv_w[l], cb, wz, wm, wo, wout,
                              tm=lc, tiles_per_group=b, tiles_per_seq=1)
        x2 = x2_new
    return x2.reshape(b, s, d)
```

```python
import functools
import math

import numpy as np
import jax
import jax.numpy as jnp
from jax import lax
from jax.experimental import pallas as pl
from jax.experimental.pallas import tpu as pltpu

D_MODEL = 1024
GRID_W = 64
HEAD_DIM = 64
EPS = 1e-6
NEG_INF = -1e30
A_HEADS = 8
A_KV_HEADS = 2
A_BLOCK = 128
A_WIDTH = A_HEADS * HEAD_DIM
A_KV_WIDTH = A_KV_HEADS * HEAD_DIM
ROPE_BASE = 10000.0
B_WIDTH = 256
R_HEADS = 4
R_WIDTH = R_HEADS * HEAD_DIM
R_STEP = 256
F_WIDTH = 256
N_BRANCH = 4

IN_NAMES = ("a_q", "a_k", "a_v", "a_z", "b_u", "b_b", "b_c", "b_z", "r_q", "r_k", "r_v", "r_z", "f_u", "f_z", "merge")
IN_SIZES = (A_WIDTH, A_KV_WIDTH, A_KV_WIDTH, A_WIDTH, B_WIDTH, B_WIDTH, B_WIDTH, B_WIDTH,
            R_WIDTH, R_WIDTH, R_WIDTH, R_WIDTH, F_WIDTH, F_WIDTH, N_BRANCH * D_MODEL)
_OFFS = dict(zip(IN_NAMES, np.cumsum((0,) + IN_SIZES)[:-1].tolist()))
_SIZE = dict(zip(IN_NAMES, IN_SIZES))

Z_NAMES = ("a_z", "b_z", "r_z", "f_z")
Z_SIZES = tuple(_SIZE[n] for n in Z_NAMES)

MERGE_CHUNK = 256
ATTN_SUB = 8
LOG2E = math.log2(math.e)
LANES = 128
VMEM_LIMIT = 56 * 1024 * 1024

BF16 = jnp.bfloat16
F32 = jnp.float32
HI = lax.Precision.HIGHEST


def _cparams(sem):
    return pltpu.CompilerParams(dimension_semantics=sem, vmem_limit_bytes=VMEM_LIMIT)


def _const_spec(shape):
    nd = len(shape)
    return pl.BlockSpec(shape, lambda *_: (0,) * nd)


def _sigmoid(v):
    return 1.0 / (1.0 + jnp.exp(-v))


def _ada_kernel(cv_ref, w_ref, b_ref, o_ref):
    cv = cv_ref[...]
    s = cv * _sigmoid(cv)
    o_ref[0] = jnp.dot(s, w_ref[0], preferred_element_type=F32, precision=HI) + b_ref[0]


def _ada_call(cv, w_ada, b_ada):
    depth, d, d3 = w_ada.shape
    tn = 1024
    return pl.pallas_call(
        _ada_kernel,
        grid=(depth, d3 // tn),
        in_specs=[pl.BlockSpec((8, d), lambda l, j: (0, 0)),
                  pl.BlockSpec((1, d, tn), lambda l, j: (l, 0, j)),
                  pl.BlockSpec((1, 1, tn), lambda l, j: (l, 0, j))],
        out_specs=pl.BlockSpec((1, 8, tn), lambda l, j: (l, 0, j)),
        out_shape=jax.ShapeDtypeStruct((depth, 8, d3), F32),
        compiler_params=_cparams(("arbitrary", "arbitrary")),
        name="ada_mod",
    )(cv, w_ada, b_ada.reshape(depth, 1, d3))


def _modulated_norm(x, g, mod):
    ms = jnp.mean(x * x, axis=-1, keepdims=True)
    y = x * lax.rsqrt(ms + EPS) * g
    return y * (1.0 + mod[:, D_MODEL:2 * D_MODEL]) + mod[:, 0:D_MODEL]


def _rope(t, cos, sin_signed, first_half):
    outs = []
    for j in range(t.shape[1] // LANES):
        tj = t[:, j * LANES:(j + 1) * LANES]
        partner = jnp.where(first_half, pltpu.roll(tj, LANES - 16, 1), pltpu.roll(tj, 16, 1))
        outs.append(tj * cos + partner * sin_signed)
    return outs[0] if len(outs) == 1 else jnp.concatenate(outs, axis=1)


def _inproj_kernel(x_ref, mod_ref, g_ref, w_ref, cos_ref, sin_ref, fd_ref,
                   qa_ref, ka_ref, va_ref, wc_ref, bb_ref, qr_ref, kr_ref, vr_ref, zf_ref):
    h = _modulated_norm(x_ref[...], g_ref[...], mod_ref[0]).astype(BF16)

    def proj(name):
        o = _OFFS[name]
        return jnp.dot(h, w_ref[0, :, o:o + _SIZE[name]], preferred_element_type=F32)

    cos = cos_ref[...]
    sin = sin_ref[...]
    lane = lax.broadcasted_iota(jnp.int32, cos.shape, 1)
    first_half = (lane % 32) < 16
    low_half = lane < HEAD_DIM
    k_scale = HEAD_DIM ** -0.5
    q = _rope(proj("a_q"), cos, sin, first_half) * (k_scale * LOG2E)
    for p in range(A_WIDTH // LANES):
        qp = q[:, p * LANES:(p + 1) * LANES]
        qa_ref[:, (2 * p) * LANES:(2 * p + 1) * LANES] = jnp.where(low_half, qp, 0.0).astype(BF16)
        qa_ref[:, (2 * p + 1) * LANES:(2 * p + 2) * LANES] = jnp.where(low_half, 0.0, qp).astype(BF16)
    k = _rope(proj("a_k"), cos, sin, first_half)
    k_sw = pltpu.roll(k, HEAD_DIM, 1)
    ka_ref[:, 0:LANES] = jnp.where(low_half, k, k_sw).astype(BF16)
    ka_ref[:, LANES:2 * LANES] = jnp.where(low_half, k_sw, k).astype(BF16)
    v = proj("a_v")
    v_sw = pltpu.roll(v, HEAD_DIM, 1)
    va_ref[:, 0:LANES] = jnp.where(low_half, v, 1.0).astype(BF16)
    va_ref[:, LANES:2 * LANES] = jnp.where(low_half, 1.0, v_sw).astype(BF16)
    va_ref[:, 2 * LANES:3 * LANES] = jnp.where(low_half, v_sw, 1.0).astype(BF16)
    va_ref[:, 3 * LANES:4 * LANES] = jnp.where(low_half, 1.0, v).astype(BF16)
    wc_ref[...] = proj("b_c") * proj("b_u")
    bb_ref[...] = proj("b_b")
    qr_ref[...] = _rope(proj("r_q"), cos, sin, first_half)
    kr_ref[...] = _rope(proj("r_k"), cos, sin, first_half) * k_scale
    vr_ref[...] = proj("r_v")
    zf_ref[...] = jnp.dot(proj("f_u").astype(BF16), fd_ref[...].astype(BF16), preferred_element_type=F32)


def _layer_spec(arr, layer):
    return pl.BlockSpec((1,) + arr.shape[1:], lambda *_: (layer,) + (0,) * (arr.ndim - 1),
                        pipeline_mode=pl.Buffered(1))


def _inproj_call(x2, mod3, g_pre, w_all, layer, cos_t, sin_t, fd, *, tm, tiles_per_group):
    rows, d = x2.shape
    nt = rows // tm
    tiles_per_seq = cos_t.shape[0] // tm
    widths = (2 * A_WIDTH, 2 * A_KV_WIDTH, 4 * A_KV_WIDTH, B_WIDTH, B_WIDTH, R_WIDTH, R_WIDTH, R_WIDTH, 2 * F_WIDTH)
    dtypes = (BF16, BF16, BF16, F32, F32, F32, F32, F32, F32)
    row_spec = lambda w: pl.BlockSpec((tm, w), lambda i: (i, 0))
    return pl.pallas_call(
        _inproj_kernel,
        grid=(nt,),
        in_specs=[row_spec(d),
                  pl.BlockSpec((1, 1, 3 * d), lambda i: (i // tiles_per_group, 0, 0)),
                  _const_spec((1, d)),
                  _layer_spec(w_all, layer),
                  pl.BlockSpec((tm, LANES), lambda i: (i % tiles_per_seq, 0)),
                  pl.BlockSpec((tm, LANES), lambda i: (i % tiles_per_seq, 0)),
                  _const_spec(fd.shape)],
        out_specs=[row_spec(w) for w in widths],
        out_shape=[jax.ShapeDtypeStruct((rows, w), dt) for w, dt in zip(widths, dtypes)],
        compiler_params=_cparams(("arbitrary",)),
        name="in_proj",
    )(x2, mod3, g_pre, w_all, cos_t, sin_t, fd)


def _attn_kernel(*refs, local):
    if local:
        (sink_ref, q_ref, kp_ref, kc_ref, kn_ref, vp_ref, vc_ref, vn_ref, kx_ref, vx_ref, o_ref,
         kwin, vwin) = refs
    else:
        sink_ref, q_ref, kx_ref, vx_ref, o_ref = refs
    blk = A_BLOCK
    nsub = q_ref.shape[1] // blk
    n = pl.program_id(1)
    last = pl.num_programs(1) * nsub - 1
    contract_last = (((1,), (1,)), ((), ()))

    lane_q = lax.broadcasted_iota(jnp.int32, (blk, LANES), 1)
    nloc = 3 * blk
    if local:
        step = nsub * blk
        kwin[0:blk] = kp_ref[0]
        kwin[blk:blk + step] = kc_ref[0]
        kwin[blk + step:2 * blk + step] = kn_ref[0]
        vwin[0:blk] = vp_ref[0]
        vwin[blk:blk + step] = vc_ref[0]
        vwin[blk + step:2 * blk + step] = vn_ref[0]
        row = lax.broadcasted_iota(jnp.int32, (blk, nloc), 0)
        col = lax.broadcasted_iota(jnp.int32, (blk, nloc), 1)
        band = jnp.minimum(col - row, row + 2 * blk - col)

    for sub in range(nsub):
        q0 = sub * blk
        if local:
            g_blk = n * nsub + sub
            seq_lo = blk - g_blk * blk
            seq_hi = blk + (last - g_blk + 1) * blk
            valid = (jnp.minimum(band, jnp.minimum(col - seq_lo, seq_hi - 1 - col)) >= 0)[None]
        for j in range(A_KV_HEADS):
            kcol = slice(j * LANES, (j + 1) * LANES)
            ecol = slice(2 * j * LANES, (2 * j + 1) * LANES)
            ocol = slice((2 * j + 1) * LANES, (2 * j + 2) * LANES)
            qs = jnp.concatenate([q_ref[0, q0:q0 + blk, (4 * j + g) * LANES:(4 * j + g + 1) * LANES]
                                  for g in range(4)], axis=0)
            s = lax.dot_general(qs, kx_ref[0, :, kcol], contract_last, preferred_element_type=F32)
            if local:
                s_loc = lax.dot_general(qs, kwin[q0:q0 + nloc, kcol], contract_last, preferred_element_type=F32)
                s_loc = jnp.where(valid, s_loc.reshape(4, blk, nloc), NEG_INF).reshape(4 * blk, nloc)
                s = jnp.concatenate([s_loc, s], axis=1)
            sink = jnp.concatenate(
                [jnp.full((blk, 1), sink_ref[4 * j + g] * LOG2E, F32) for g in range(4)], axis=0)
            m = jnp.maximum(jnp.max(s, axis=1, keepdims=True), sink)
            pb = jnp.exp2((s - m).astype(BF16))
            e_sink = jnp.exp2(sink - m)

            vcol = slice(2 * j * LANES, (2 * j + 2) * LANES)
            if local:
                o_all = (jnp.dot(pb[:, :nloc], vwin[q0:q0 + nloc, vcol], preferred_element_type=F32)
                         + jnp.dot(pb[:, nloc:], vx_ref[0, :, vcol], preferred_element_type=F32))
            else:
                o_all = jnp.dot(pb, vx_ref[0, :, vcol], preferred_element_type=F32)

            for pair in range(2):
                r0 = 2 * pair * blk
                o_even = o_all[r0:r0 + blk, :LANES]
                o_odd = o_all[r0 + blk:r0 + 2 * blk, LANES:]
                num = jnp.where(lane_q < 64, o_even, o_odd)
                den = (pltpu.roll(jnp.where(lane_q < 64, o_odd, o_even), 64, 1)
                       + jnp.where(lane_q < 64, e_sink[r0:r0 + blk], e_sink[r0 + blk:r0 + 2 * blk]))
                c0 = (2 * j + pair) * LANES
                o_ref[0, q0:q0 + blk, c0:c0 + LANES] = num / den


def _attn_call(sink, q, k, v, kx, vx, *, local):
    b, sq, _ = q.shape
    blk = A_BLOCK
    nq = sq // blk
    nsub = min(ATTN_SUB, nq)
    step = nsub * blk
    lx = kx.shape[1]
    smem = pl.BlockSpec(memory_space=pltpu.SMEM)
    kw, vw = kx.shape[2], vx.shape[2]
    q_spec = pl.BlockSpec((1, step, q.shape[2]), lambda bi, n: (bi, n, 0))
    if local:
        def halo(w):
            return (pl.BlockSpec((1, blk, w), lambda bi, n: (bi, jnp.maximum(n * nsub - 1, 0), 0)),
                    pl.BlockSpec((1, step, w), lambda bi, n: (bi, n, 0)),
                    pl.BlockSpec((1, blk, w), lambda bi, n: (bi, jnp.minimum((n + 1) * nsub, nq - 1), 0)))
        in_specs = [smem, q_spec, *halo(kw), *halo(vw),
                    pl.BlockSpec((1, lx, kw), lambda bi, n: (bi, 0, 0)),
                    pl.BlockSpec((1, lx, vw), lambda bi, n: (bi, 0, 0))]
        args = (sink, q, k, k, k, v, v, v, kx, vx)
        scratch = [pltpu.VMEM((step + 2 * blk, kw), k.dtype), pltpu.VMEM((step + 2 * blk, vw), v.dtype)]
    else:
        in_specs = [smem, q_spec, pl.BlockSpec((1, lx, kw), lambda bi, n: (bi, 0, 0)),
                    pl.BlockSpec((1, lx, vw), lambda bi, n: (bi, 0, 0))]
        args = (sink, q, kx, vx)
        scratch = []
    return pl.pallas_call(
        functools.partial(_attn_kernel, local=local),
        grid=(b, nq // nsub),
        in_specs=in_specs,
        out_specs=pl.BlockSpec((1, step, A_WIDTH), lambda bi, n: (bi, n, 0)),
        out_shape=jax.ShapeDtypeStruct((b, sq, A_WIDTH), F32),
        scratch_shapes=scratch,
        compiler_params=_cparams(("arbitrary", "arbitrary")),
        name="win_attn" if local else "ctx_attn",
    )(*args)


def _head_of(shape, dim):
    return lax.broadcasted_iota(jnp.int32, shape, dim) // HEAD_DIM


def _group_mean(t, avg):
    hi = t.astype(BF16)
    lo = (t - hi.astype(F32)).astype(BF16)
    return (jnp.dot(hi, avg, preferred_element_type=F32) + jnp.dot(lo, avg, preferred_element_type=F32))


def _ret_kernel(*refs, reverse, finalize):
    if finalize:
        lg_ref, lgl_ref, q_ref, k_ref, v_ref, r0_ref, yin_ref, y_ref, rfin_ref, r_scr = refs
    else:
        lg_ref, lgl_ref, q_ref, k_ref, v_ref, r0_ref, y_ref, rfin_ref, r_scr = refs
    c = pl.program_id(0)
    nb, ch, w = q_ref.shape

    @pl.when(c == 0)
    def _():
        r_scr[...] = r0_ref[...]

    lgl = lgl_ref[...]
    pos = lax.broadcasted_iota(jnp.int32, (ch, 1), 0).astype(F32)
    if reverse:
        xi = jnp.exp(lgl * (ch - pos))
        zeta = jnp.exp(lgl * pos)
    else:
        xi = jnp.exp(lgl * (pos + 1.0))
        zeta = jnp.exp(lgl * (ch - 1.0 - pos))
    g_chunk = jnp.exp(lgl * float(ch))
    ri = lax.broadcasted_iota(jnp.int32, (ch, ch), 0)
    ci = lax.broadcasted_iota(jnp.int32, (ch, ch), 1)
    diff = (ci - ri) if reverse else (ri - ci)
    dist = jnp.maximum(diff, 0).astype(F32)
    decays = [jnp.where(diff >= 0, jnp.exp(lg_ref[h] * dist), 0.0) for h in range(R_HEADS)]
    lane_head = _head_of((ch, w), 1)
    same_head = _head_of((w, w), 0) == _head_of((w, w), 1)
    avg = jnp.where(same_head, 1.0 / HEAD_DIM, 0.0).astype(BF16)

    for bi in range(nb):
        q = q_ref[bi]
        k = k_ref[bi]
        v = v_ref[bi]
        r = r_scr[bi]
        cross = jnp.dot((q * xi).astype(BF16), r.astype(BF16), preferred_element_type=F32)
        q4 = jnp.concatenate([jnp.where(lane_head == h, q, 0.0) for h in range(R_HEADS)], axis=0).astype(BF16)
        sc = lax.dot_general(q4, k.astype(BF16), (((1,), (1,)), ((), ())), preferred_element_type=F32)
        s4 = jnp.concatenate([(sc[h * ch:(h + 1) * ch] * decays[h]).astype(BF16) for h in range(R_HEADS)], axis=1)
        v4 = jnp.concatenate([jnp.where(lane_head == h, v, 0.0) for h in range(R_HEADS)], axis=0).astype(BF16)
        y = jnp.dot(s4, v4, preferred_element_type=F32) + cross

        kz = (k * zeta).astype(BF16)
        ktv = lax.dot_general(kz, v.astype(BF16), (((0,), (0,)), ((), ())), preferred_element_type=F32)
        r_new = g_chunk * r + jnp.where(same_head, ktv, 0.0)
        r_scr[bi] = r_new
        rfin_ref[bi] = r_new

        if finalize:
            y = y + yin_ref[bi]
            mu = _group_mean(y, avg)
            d = y - mu
            var = _group_mean(d * d, avg)
            y = d * lax.rsqrt(var + EPS)
        y_ref[bi] = y


def _ret_call(lg, lgl, q, k, v, r0, y_in, *, reverse):
    b, s, w = q.shape
    ch = min(R_STEP, s)
    nc = s // ch
    finalize = y_in is not None
    cidx = (lambda c: (0, nc - 1 - c, 0)) if reverse else (lambda c: (0, c, 0))
    chunk = pl.BlockSpec((b, ch, w), cidx)
    state = pl.BlockSpec((b, w, w), lambda c: (0, 0, 0))
    in_specs = [pl.BlockSpec(memory_space=pltpu.SMEM), pl.BlockSpec((1, w), lambda c: (0, 0)),
                chunk, chunk, chunk, state]
    args = [lg, lgl, q, k, v, r0]
    if finalize:
        in_specs.append(chunk)
        args.append(y_in)
    return pl.pallas_call(
        functools.partial(_ret_kernel, reverse=reverse, finalize=finalize),
        grid=(nc,),
        in_specs=in_specs,
        out_specs=[chunk, state],
        out_shape=[jax.ShapeDtypeStruct((b, s, w), F32), jax.ShapeDtypeStruct((b, w, w), F32)],
        scratch_shapes=[pltpu.VMEM((b, w, w), F32)],
        compiler_params=_cparams(("arbitrary",)),
        name="retention_bwd" if reverse else "retention_fwd",
    )(*args)


def _dft_cs(n, scale):
    a = 2.0 * np.pi * np.outer(np.arange(n), np.arange(n)) / n
    return np.cos(a) * scale, np.sin(a) * scale


def _fourier_rows_kernel(m_ref, tc_ref, ts_ref, z_ref, o_ref):
    rows, cb = z_ref.shape[1], z_ref.shape[2]
    m = m_ref[...].astype(BF16)
    for ci in range(cb):
        z = z_ref[0, :, ci, :]
        zz = jnp.concatenate([z[:, :F_WIDTH], z[:, F_WIDTH:]], axis=0).astype(BF16)
        a = jnp.dot(m, zz, preferred_element_type=F32)
        a_re, a_im = a[:rows], a[rows:]
        tc = jnp.concatenate([tc_ref[ci]] * (F_WIDTH // LANES), axis=1)
        ts = jnp.concatenate([ts_ref[ci]] * (F_WIDTH // LANES), axis=1)
        o_ref[0, :, 0, ci, :] = (a_re * tc + a_im * ts).astype(o_ref.dtype)
        o_ref[0, :, 1, ci, :] = (a_im * tc - a_re * ts).astype(o_ref.dtype)


def _fourier_cols_kernel(g_ref, b_ref, o_ref):
    g = g_ref[...].astype(BF16)
    kt, _, cw, f = b_ref.shape[1:]
    for i in range(kt):
        bm = b_ref[0, i].reshape(2 * cw, f).astype(BF16)
        o_ref[0, :, i, :] = jnp.dot(g, bm, preferred_element_type=F32)


def _fourier_latent(zf, rows):
    b, s, _ = zf.shape
    cw = GRID_W
    c_r, s_r = _dft_cs(rows, rows ** -0.5)
    m1 = jnp.asarray(np.block([[c_r, s_r], [-s_r, c_r]]), F32)
    ang = 2.0 * np.pi * np.outer(np.arange(cw), np.arange(rows)) / s
    tc = jnp.asarray(np.repeat(np.cos(ang)[:, :, None], LANES, axis=2), F32)
    ts = jnp.asarray(np.repeat(np.sin(ang)[:, :, None], LANES, axis=2), F32)
    cb = 16
    bk = pl.pallas_call(
        _fourier_rows_kernel,
        grid=(cw // cb, b),
        in_specs=[_const_spec(m1.shape),
                  pl.BlockSpec((cb, rows, LANES), lambda j, bi: (j, 0, 0)),
                  pl.BlockSpec((cb, rows, LANES), lambda j, bi: (j, 0, 0)),
                  pl.BlockSpec((1, rows, cb, 2 * F_WIDTH), lambda j, bi: (bi, 0, j, 0))],
        out_specs=pl.BlockSpec((1, rows, 2, cb, F_WIDTH), lambda j, bi: (bi, 0, 0, j, 0)),
        out_shape=jax.ShapeDtypeStruct((b, rows, 2, cw, F_WIDTH), F32),
        compiler_params=_cparams(("arbitrary", "arbitrary")),
        name="fourier_rows",
    )(m1, tc, ts, zf.reshape(b, rows, cw, 2 * F_WIDTH))
    c_c, s_c = _dft_cs(cw, cw ** -0.5)
    g = jnp.asarray(np.concatenate([c_c, s_c], axis=1), F32)
    kt = 16
    out = pl.pallas_call(
        _fourier_cols_kernel,
        grid=(b, rows // kt),
        in_specs=[_const_spec(g.shape),
                  pl.BlockSpec((1, kt, 2, cw, F_WIDTH), lambda bi, i: (bi, i, 0, 0, 0))],
        out_specs=pl.BlockSpec((1, cw, kt, F_WIDTH), lambda bi, i: (bi, 0, i, 0)),
        out_shape=jax.ShapeDtypeStruct((b, cw, rows, F_WIDTH), F32),
        compiler_params=_cparams(("arbitrary", "arbitrary")),
        name="fourier_cols",
    )(g, bk)
    return out.reshape(b, s, F_WIDTH)


def _fourier_dense_kernel(m_ref, z_ref, o_ref):
    z = z_ref[0].astype(F32)
    zz = jnp.concatenate([z[:, :F_WIDTH], z[:, F_WIDTH:]], axis=0)
    o_ref[0] = jnp.dot(m_ref[...], zz, preferred_element_type=F32, precision=HI)


def _fourier_dense(zf):
    b, n, _ = zf.shape
    c_n, s_n = _dft_cs(n, n ** -0.5)
    m = jnp.asarray(np.concatenate([c_n, s_n], axis=1), F32)
    return pl.pallas_call(
        _fourier_dense_kernel,
        grid=(b,),
        in_specs=[_const_spec(m.shape), pl.BlockSpec((1, n, 2 * F_WIDTH), lambda bi: (bi, 0, 0))],
        out_specs=pl.BlockSpec((1, n, F_WIDTH), lambda bi: (bi, 0, 0)),
        out_shape=jax.ShapeDtypeStruct((b, n, F_WIDTH), F32),
        compiler_params=_cparams(("arbitrary",)),
        name="fourier_dense",
    )(m, zf)


def _merge_kernel(x_ref, mod_ref, gpre_ref, gpost_ref, ya_ref, wc_ref, wprev_ref, wnext_ref, bb_ref,
                  yr_ref, yf_ref, cw_ref, cb_ref, w_ref, woa_ref, wob_ref, wor_ref, wof_ref, wout_ref, o_ref,
                  *, tiles_per_seq):
    i = pl.program_id(0)
    x = x_ref[...]
    mod = mod_ref[0]
    h = _modulated_norm(x, gpre_ref[...], mod).astype(BF16)
    tm = x.shape[0]

    wc = wc_ref[...]
    t = i % tiles_per_seq
    prev_row = jnp.where(t > 0, wprev_ref[7:8, :], 0.0)
    next_row = jnp.where(t < tiles_per_seq - 1, wnext_ref[0:1, :], 0.0)
    row = lax.broadcasted_iota(jnp.int32, wc.shape, 0)
    up = jnp.where(row == 0, prev_row, pltpu.roll(wc, 1, 0))
    dn = jnp.where(row == tm - 1, next_row, pltpu.roll(wc, tm - 1, 0))
    conv = up * cw_ref[0:1, :] + wc * cw_ref[1:2, :] + dn * cw_ref[2:3, :] + cb_ref[...]
    yb = bb_ref[...] * conv

    ys = (ya_ref[...], yb, yr_ref[...], yf_ref[...])
    wo_refs = (woa_ref, wob_ref, wor_ref, wof_ref)
    acts = []
    for br in range(N_BRANCH):
        zo = _OFFS[Z_NAMES[br]]
        z = jnp.dot(h, w_ref[0, :, zo:zo + Z_SIZES[br]], preferred_element_type=F32)
        acts.append((ys[br] * (z * _sigmoid(z))).astype(BF16))
    y = jnp.zeros((tm, D_MODEL), F32)
    for c0 in range(0, D_MODEL, MERGE_CHUNK):
        total = jnp.zeros((tm, MERGE_CHUNK), F32)
        for br in range(N_BRANCH):
            proj = jnp.dot(acts[br], wo_refs[br][0, :, c0:c0 + MERGE_CHUNK], preferred_element_type=F32)
            g0 = _OFFS["merge"] + br * D_MODEL + c0
            gate = _sigmoid(jnp.dot(h, w_ref[0, :, g0:g0 + MERGE_CHUNK], preferred_element_type=F32))
            total = total + gate * proj
        y = y + jnp.dot(total.astype(BF16), wout_ref[0, c0:c0 + MERGE_CHUNK, :], preferred_element_type=F32)
    ms = jnp.mean(y * y, axis=-1, keepdims=True)
    yn = y * lax.rsqrt(ms + EPS) * gpost_ref[...]
    o_ref[...] = x + mod[:, 2 * D_MODEL:3 * D_MODEL] * yn


def _merge_call(x2, mod3, g_pre, g_post, ya, wc, bb, yr, yf, conv_w, conv_b, w_all, wos, wout, layer,
                *, tm, tiles_per_group, tiles_per_seq):
    rows, d = x2.shape
    nt = rows // tm
    hb = tm // 8
    nhb = rows // 8
    row_spec = lambda w: pl.BlockSpec((tm, w), lambda i: (i, 0))
    return pl.pallas_call(
        functools.partial(_merge_kernel, tiles_per_seq=tiles_per_seq),
        grid=(nt,),
        in_specs=[row_spec(d),
                  pl.BlockSpec((1, 1, 3 * d), lambda i: (i // tiles_per_group, 0, 0)),
                  _const_spec((1, d)), _const_spec((1, d)),
                  row_spec(A_WIDTH),
                  row_spec(B_WIDTH),
                  pl.BlockSpec((8, B_WIDTH), lambda i: (jnp.maximum(i * hb - 1, 0), 0)),
                  pl.BlockSpec((8, B_WIDTH), lambda i: (jnp.minimum((i + 1) * hb, nhb - 1), 0)),
                  row_spec(B_WIDTH), row_spec(R_WIDTH), row_spec(F_WIDTH),
                  _const_spec(conv_w.shape), _const_spec(conv_b.shape),
                  _layer_spec(w_all, layer), *[_layer_spec(w, layer) for w in wos],
                  _layer_spec(wout, layer)],
        out_specs=row_spec(d),
        out_shape=jax.ShapeDtypeStruct((rows, d), F32),
        compiler_params=_cparams(("arbitrary",)),
        name="merge",
    )(x2, mod3, g_pre, g_post, ya, wc, wc, wc, bb, yr, yf, conv_w, conv_b, w_all, *wos, wout)


def _rope_tables(n):
    rows = n // GRID_W
    row = jnp.broadcast_to(jnp.arange(rows)[:, None], (rows, GRID_W)).reshape(-1).astype(F32)
    col = jnp.broadcast_to(jnp.arange(GRID_W)[None, :], (rows, GRID_W)).reshape(-1).astype(F32)
    half = HEAD_DIM // 2
    inv = ROPE_BASE ** (-jnp.arange(0, half, 2, dtype=F32) / half)
    ang_r = row[:, None] * inv
    ang_c = col[:, None] * inv
    cos = jnp.concatenate([jnp.cos(ang_r), jnp.cos(ang_r), jnp.cos(ang_c), jnp.cos(ang_c)], axis=1)
    sin = jnp.concatenate([-jnp.sin(ang_r), jnp.sin(ang_r), -jnp.sin(ang_c), jnp.sin(ang_c)], axis=1)
    return jnp.tile(cos, (1, LANES // HEAD_DIM)), jnp.tile(sin, (1, LANES // HEAD_DIM))


def kernel(x, c, ctx, c_ctx, w_ada, b_ada, norm_pre, norm_post, w_in, attn_sink, conv_w, conv_b, ret_decay,
           w_o_attn, w_o_conv, w_o_ret, w_o_fourier, w_out):
    b, s, d = x.shape
    lc = ctx.shape[1]
    depth = w_in.shape[0]
    rows_grid = s // GRID_W

    cv = jnp.zeros((8, d), F32).at[:b].set(c).at[b].set(c_ctx)
    mods = _ada_call(cv, w_ada, b_ada)

    cos_x, sin_x = _rope_tables(s)
    cos_c = jnp.ones((lc, LANES), F32)
    sin_c = jnp.zeros((lc, LANES), F32)

    c64, s64 = _dft_cs(HEAD_DIM, HEAD_DIM ** -0.5)
    eye = np.eye(F_WIDTH // HEAD_DIM)
    fd = jnp.asarray(np.concatenate([np.kron(eye, c64), -np.kron(eye, s64)], axis=1), F32)

    lg_all = jax.nn.log_sigmoid(ret_decay.astype(F32))
    zero_state = jnp.zeros((b, R_WIDTH, R_WIDTH), F32)

    w_all = w_in.astype(BF16)
    wos = tuple(w.astype(BF16) for w in (w_o_attn, w_o_conv, w_o_ret, w_o_fourier))
    wout = w_out.astype(BF16)

    tm_x = 1024
    tm_m = 512
    x2 = x.reshape(b * s, d)
    xc2 = ctx.reshape(b * lc, d)
    for l in range(depth):
        update_ctx = l < depth - 1
        g_pre = norm_pre[l].reshape(1, d)
        g_post = norm_post[l].reshape(1, d)
        mod_x = mods[l, :b].reshape(b, 1, 3 * d)
        mod_c = mods[l, b:b + 1].reshape(1, 1, 3 * d)
        cb = conv_b[l].reshape(1, B_WIDTH)
        lg = lg_all[l]
        lgl = jnp.repeat(lg, HEAD_DIM, axis=1)

        px = _inproj_call(x2, mod_x, g_pre, w_all, l, cos_x, sin_x, fd, tm=tm_x, tiles_per_group=s // tm_x)
        pc = _inproj_call(xc2, mod_c, g_pre, w_all, l, cos_c, sin_c, fd, tm=lc, tiles_per_group=b)
        qa, ka, va, wcx, bbx, qr, kr, vr, zf = [t.reshape(b, s, -1) for t in px]
        qac, kac, vac, wcc, bbc, qrc, krc, vrc, zfc = [t.reshape(b, lc, -1) for t in pc]

        ya = _attn_call(attn_sink[l], qa, ka, va, kac, vac, local=True)

        ycf, st_f = _ret_call(lg[0], lgl[0:1], qrc, krc, vrc, zero_state, None, reverse=False)
        ycr, st_b = _ret_call(lg[1], lgl[1:2], qrc, krc, vrc, zero_state, ycf, reverse=True)
        yf_, _ = _ret_call(lg[0], lgl[0:1], qr, kr, vr, st_f, None, reverse=False)
        yr, _ = _ret_call(lg[1], lgl[1:2], qr, kr, vr, st_b, yf_, reverse=True)

        yfo = _fourier_latent(zf, rows_grid)

        x2_new = _merge_call(x2, mod_x, g_pre, g_post, ya.reshape(b * s, -1), wcx.reshape(b * s, -1),
                             bbx.reshape(b * s, -1), yr.reshape(b * s, -1), yfo.reshape(b * s, -1),
                             conv_w[l], cb, w_all, wos, wout, l,
                             tm=tm_m, tiles_per_group=s // tm_m, tiles_per_seq=s // tm_m)
        if update_ctx:
            yac = _attn_call(attn_sink[l], qac, None, None, kac, vac, local=False)
            yfc = _fourier_dense(zfc)
            xc2 = _merge_call(xc2, mod_c, g_pre, g_post, yac.reshape(b * lc, -1), wcc.reshape(b * lc, -1),
                              bbc.reshape(b * lc, -1), ycr.reshape(b * lc, -1), yfc.reshape(b * lc, -1),
                              conv_w[l], cb, w_all, wos, wout, l,
                              tm=lc, tiles_per_group=b, tiles_per_seq=1)
        x2 = x2_new
    return x2.reshape(b, s, d)
```

```python
import functools
import math

import numpy as np
import jax
import jax.numpy as jnp
from jax import lax
from jax.experimental import pallas as pl
from jax.experimental.pallas import tpu as pltpu

D_MODEL = 1024
GRID_W = 64
HEAD_DIM = 64
EPS = 1e-6
NEG_INF = -1e30
A_HEADS = 8
A_KV_HEADS = 2
A_BLOCK = 128
A_WIDTH = A_HEADS * HEAD_DIM
A_KV_WIDTH = A_KV_HEADS * HEAD_DIM
ROPE_BASE = 10000.0
B_WIDTH = 256
R_HEADS = 4
R_WIDTH = R_HEADS * HEAD_DIM
R_STEP = 256
F_WIDTH = 256
N_BRANCH = 4

IN_NAMES = ("a_q", "a_k", "a_v", "a_z", "b_u", "b_b", "b_c", "b_z", "r_q", "r_k", "r_v", "r_z", "f_u", "f_z", "merge")
IN_SIZES = (A_WIDTH, A_KV_WIDTH, A_KV_WIDTH, A_WIDTH, B_WIDTH, B_WIDTH, B_WIDTH, B_WIDTH,
            R_WIDTH, R_WIDTH, R_WIDTH, R_WIDTH, F_WIDTH, F_WIDTH, N_BRANCH * D_MODEL)
_OFFS = dict(zip(IN_NAMES, np.cumsum((0,) + IN_SIZES)[:-1].tolist()))
_SIZE = dict(zip(IN_NAMES, IN_SIZES))

Z_NAMES = ("a_z", "b_z", "r_z", "f_z")
Z_SIZES = tuple(_SIZE[n] for n in Z_NAMES)

MERGE_CHUNK = 256
ATTN_SUB = 8
LOG2E = math.log2(math.e)
LANES = 128
VMEM_LIMIT = 56 * 1024 * 1024

BF16 = jnp.bfloat16
F32 = jnp.float32
HI = lax.Precision.HIGHEST


def _cparams(sem):
    return pltpu.CompilerParams(dimension_semantics=sem, vmem_limit_bytes=VMEM_LIMIT)


def _const_spec(shape):
    nd = len(shape)
    return pl.BlockSpec(shape, lambda *_: (0,) * nd)


def _sigmoid(v):
    return 1.0 / (1.0 + jnp.exp(-v))


def _ada_kernel(cv_ref, w_ref, b_ref, o_ref):
    cv = cv_ref[...]
    s = cv * _sigmoid(cv)
    o_ref[0] = jnp.dot(s, w_ref[0], preferred_element_type=F32, precision=HI) + b_ref[0]


def _ada_call(cv, w_ada, b_ada):
    depth, d, d3 = w_ada.shape
    tn = 1024
    return pl.pallas_call(
        _ada_kernel,
        grid=(depth, d3 // tn),
        in_specs=[pl.BlockSpec((8, d), lambda l, j: (0, 0)),
                  pl.BlockSpec((1, d, tn), lambda l, j: (l, 0, j)),
                  pl.BlockSpec((1, 1, tn), lambda l, j: (l, 0, j))],
        out_specs=pl.BlockSpec((1, 8, tn), lambda l, j: (l, 0, j)),
        out_shape=jax.ShapeDtypeStruct((depth, 8, d3), F32),
        compiler_params=_cparams(("arbitrary", "arbitrary")),
        name="ada_mod",
    )(cv, w_ada, b_ada.reshape(depth, 1, d3))


def _modulated_norm(x, g, mod):
    ms = jnp.mean(x * x, axis=-1, keepdims=True)
    y = x * lax.rsqrt(ms + EPS) * g
    return y * (1.0 + mod[:, D_MODEL:2 * D_MODEL]) + mod[:, 0:D_MODEL]


def _rope(t, cos, sin_signed, first_half):
    outs = []
    for j in range(t.shape[1] // LANES):
        tj = t[:, j * LANES:(j + 1) * LANES]
        partner = jnp.where(first_half, pltpu.roll(tj, LANES - 16, 1), pltpu.roll(tj, 16, 1))
        outs.append(tj * cos + partner * sin_signed)
    return outs[0] if len(outs) == 1 else jnp.concatenate(outs, axis=1)


MIX_GROUPS = (("a_q", "a_v"), ("b_u", "b_c"), ("r_q", "r_v"), ("f_u", "f_u"))


def _inproj_kernel(x_ref, mod_ref, g_ref, wa_ref, wb_ref, wr_ref, wf_ref, cos_ref, sin_ref, fd_ref,
                   qa_ref, ka_ref, va_ref, wc_ref, bb_ref, qr_ref, kr_ref, vr_ref, zf_ref):
    h = _modulated_norm(x_ref[...], g_ref[...], mod_ref[0]).astype(BF16)
    w_refs = {"a": wa_ref, "b": wb_ref, "r": wr_ref, "f": wf_ref}
    starts = {first[0]: _OFFS[first] for first, _ in MIX_GROUPS}

    def proj(name):
        o = _OFFS[name] - starts[name[0]]
        return jnp.dot(h, w_refs[name[0]][0, :, o:o + _SIZE[name]], preferred_element_type=F32)

    cos = cos_ref[...]
    sin = sin_ref[...]
    lane = lax.broadcasted_iota(jnp.int32, cos.shape, 1)
    first_half = (lane % 32) < 16
    low_half = lane < HEAD_DIM
    k_scale = HEAD_DIM ** -0.5
    q = _rope(proj("a_q"), cos, sin, first_half) * (k_scale * LOG2E)
    for p in range(A_WIDTH // LANES):
        qp = q[:, p * LANES:(p + 1) * LANES]
        qa_ref[:, (2 * p) * LANES:(2 * p + 1) * LANES] = jnp.where(low_half, qp, 0.0).astype(BF16)
        qa_ref[:, (2 * p + 1) * LANES:(2 * p + 2) * LANES] = jnp.where(low_half, 0.0, qp).astype(BF16)
    k = _rope(proj("a_k"), cos, sin, first_half)
    k_sw = pltpu.roll(k, HEAD_DIM, 1)
    ka_ref[:, 0:LANES] = jnp.where(low_half, k, k_sw).astype(BF16)
    ka_ref[:, LANES:2 * LANES] = jnp.where(low_half, k_sw, k).astype(BF16)
    v = proj("a_v")
    v_sw = pltpu.roll(v, HEAD_DIM, 1)
    va_ref[:, 0:LANES] = jnp.where(low_half, v, 1.0).astype(BF16)
    va_ref[:, LANES:2 * LANES] = jnp.where(low_half, 1.0, v_sw).astype(BF16)
    va_ref[:, 2 * LANES:3 * LANES] = jnp.where(low_half, v_sw, 1.0).astype(BF16)
    va_ref[:, 3 * LANES:4 * LANES] = jnp.where(low_half, 1.0, v).astype(BF16)
    wc_ref[...] = proj("b_c") * proj("b_u")
    bb_ref[...] = proj("b_b")
    qr_ref[...] = _rope(proj("r_q"), cos, sin, first_half)
    kr_ref[...] = _rope(proj("r_k"), cos, sin, first_half) * k_scale
    vr_ref[...] = proj("r_v")
    zf_ref[...] = jnp.dot(proj("f_u").astype(BF16), fd_ref[...].astype(BF16), preferred_element_type=F32)


def _layer_spec(arr, layer):
    return pl.BlockSpec((1,) + arr.shape[1:], lambda *_: (layer,) + (0,) * (arr.ndim - 1),
                        pipeline_mode=pl.Buffered(1))


def _cols_spec(w_all, layer, first, last):
    lo = _OFFS[first]
    width = _OFFS[last] + _SIZE[last] - lo
    return pl.BlockSpec((pl.Element(1), pl.Element(w_all.shape[1]), pl.Element(width)),
                        lambda *_: (layer, 0, lo), pipeline_mode=pl.Buffered(1))


def _inproj_call(x2, mod3, g_pre, w_all, layer, cos_t, sin_t, fd, *, tm, tiles_per_group):
    rows, d = x2.shape
    nt = rows // tm
    tiles_per_seq = cos_t.shape[0] // tm
    widths = (2 * A_WIDTH, 2 * A_KV_WIDTH, 4 * A_KV_WIDTH, B_WIDTH, B_WIDTH, R_WIDTH, R_WIDTH, R_WIDTH, 2 * F_WIDTH)
    dtypes = (BF16, BF16, BF16, F32, F32, F32, F32, F32, F32)
    row_spec = lambda w: pl.BlockSpec((tm, w), lambda i: (i, 0))
    return pl.pallas_call(
        _inproj_kernel,
        grid=(nt,),
        in_specs=[row_spec(d),
                  pl.BlockSpec((1, 1, 3 * d), lambda i: (i // tiles_per_group, 0, 0)),
                  _const_spec((1, d)),
                  *[_cols_spec(w_all, layer, first, last) for first, last in MIX_GROUPS],
                  pl.BlockSpec((tm, LANES), lambda i: (i % tiles_per_seq, 0)),
                  pl.BlockSpec((tm, LANES), lambda i: (i % tiles_per_seq, 0)),
                  _const_spec(fd.shape)],
        out_specs=[row_spec(w) for w in widths],
        out_shape=[jax.ShapeDtypeStruct((rows, w), dt) for w, dt in zip(widths, dtypes)],
        compiler_params=_cparams(("arbitrary",)),
        name="in_proj",
    )(x2, mod3, g_pre, w_all, w_all, w_all, w_all, cos_t, sin_t, fd)


def _attn_kernel(*refs, local):
    if local:
        (sink_ref, q_ref, kp_ref, kc_ref, kn_ref, vp_ref, vc_ref, vn_ref, kx_ref, vx_ref, o_ref,
         kwin, vwin) = refs
    else:
        sink_ref, q_ref, kx_ref, vx_ref, o_ref = refs
    blk = A_BLOCK
    nsub = q_ref.shape[1] // blk
    n = pl.program_id(1)
    last = pl.num_programs(1) * nsub - 1
    contract_last = (((1,), (1,)), ((), ()))

    lane_q = lax.broadcasted_iota(jnp.int32, (blk, LANES), 1)
    nloc = 3 * blk
    if local:
        step = nsub * blk
        kwin[0:blk] = kp_ref[0]
        kwin[blk:blk + step] = kc_ref[0]
        kwin[blk + step:2 * blk + step] = kn_ref[0]
        vwin[0:blk] = vp_ref[0]
        vwin[blk:blk + step] = vc_ref[0]
        vwin[blk + step:2 * blk + step] = vn_ref[0]
        row = lax.broadcasted_iota(jnp.int32, (blk, nloc), 0)
        col = lax.broadcasted_iota(jnp.int32, (blk, nloc), 1)
        band = jnp.minimum(col - row, row + 2 * blk - col)

    for sub in range(nsub):
        q0 = sub * blk
        if local:
            g_blk = n * nsub + sub
            seq_lo = blk - g_blk * blk
            seq_hi = blk + (last - g_blk + 1) * blk
            valid = (jnp.minimum(band, jnp.minimum(col - seq_lo, seq_hi - 1 - col)) >= 0)[None]
        for j in range(A_KV_HEADS):
            kcol = slice(j * LANES, (j + 1) * LANES)
            ecol = slice(2 * j * LANES, (2 * j + 1) * LANES)
            ocol = slice((2 * j + 1) * LANES, (2 * j + 2) * LANES)
            qs = jnp.concatenate([q_ref[0, q0:q0 + blk, (4 * j + g) * LANES:(4 * j + g + 1) * LANES]
                                  for g in range(4)], axis=0)
            s = lax.dot_general(qs, kx_ref[0, :, kcol], contract_last, preferred_element_type=F32)
            if local:
                s_loc = lax.dot_general(qs, kwin[q0:q0 + nloc, kcol], contract_last, preferred_element_type=F32)
                s_loc = jnp.where(valid, s_loc.reshape(4, blk, nloc), NEG_INF).reshape(4 * blk, nloc)
                s = jnp.concatenate([s_loc, s], axis=1)
            sink = jnp.concatenate(
                [jnp.full((blk, 1), sink_ref[4 * j + g] * LOG2E, F32) for g in range(4)], axis=0)
            m = jnp.maximum(jnp.max(s, axis=1, keepdims=True), sink)
            pb = jnp.exp2((s - m).astype(BF16))
            e_sink = jnp.exp2(sink - m)

            vcol = slice(2 * j * LANES, (2 * j + 2) * LANES)
            if local:
                o_all = (jnp.dot(pb[:, :nloc], vwin[q0:q0 + nloc, vcol], preferred_element_type=F32)
                         + jnp.dot(pb[:, nloc:], vx_ref[0, :, vcol], preferred_element_type=F32))
            else:
                o_all = jnp.dot(pb, vx_ref[0, :, vcol], preferred_element_type=F32)

            for pair in range(2):
                r0 = 2 * pair * blk
                o_even = o_all[r0:r0 + blk, :LANES]
                o_odd = o_all[r0 + blk:r0 + 2 * blk, LANES:]
                num = jnp.where(lane_q < 64, o_even, o_odd)
                den = (pltpu.roll(jnp.where(lane_q < 64, o_odd, o_even), 64, 1)
                       + jnp.where(lane_q < 64, e_sink[r0:r0 + blk], e_sink[r0 + blk:r0 + 2 * blk]))
                c0 = (2 * j + pair) * LANES
                o_ref[0, q0:q0 + blk, c0:c0 + LANES] = num / den


def _attn_call(sink, q, k, v, kx, vx, *, local):
    b, sq, _ = q.shape
    blk = A_BLOCK
    nq = sq // blk
    nsub = min(ATTN_SUB, nq)
    step = nsub * blk
    lx = kx.shape[1]
    smem = pl.BlockSpec(memory_space=pltpu.SMEM)
    kw, vw = kx.shape[2], vx.shape[2]
    q_spec = pl.BlockSpec((1, step, q.shape[2]), lambda bi, n: (bi, n, 0))
    if local:
        def halo(w):
            return (pl.BlockSpec((1, blk, w), lambda bi, n: (bi, jnp.maximum(n * nsub - 1, 0), 0)),
                    pl.BlockSpec((1, step, w), lambda bi, n: (bi, n, 0)),
                    pl.BlockSpec((1, blk, w), lambda bi, n: (bi, jnp.minimum((n + 1) * nsub, nq - 1), 0)))
        in_specs = [smem, q_spec, *halo(kw), *halo(vw),
                    pl.BlockSpec((1, lx, kw), lambda bi, n: (bi, 0, 0)),
                    pl.BlockSpec((1, lx, vw), lambda bi, n: (bi, 0, 0))]
        args = (sink, q, k, k, k, v, v, v, kx, vx)
        scratch = [pltpu.VMEM((step + 2 * blk, kw), k.dtype), pltpu.VMEM((step + 2 * blk, vw), v.dtype)]
    else:
        in_specs = [smem, q_spec, pl.BlockSpec((1, lx, kw), lambda bi, n: (bi, 0, 0)),
                    pl.BlockSpec((1, lx, vw), lambda bi, n: (bi, 0, 0))]
        args = (sink, q, kx, vx)
        scratch = []
    return pl.pallas_call(
        functools.partial(_attn_kernel, local=local),
        grid=(b, nq // nsub),
        in_specs=in_specs,
        out_specs=pl.BlockSpec((1, step, A_WIDTH), lambda bi, n: (bi, n, 0)),
        out_shape=jax.ShapeDtypeStruct((b, sq, A_WIDTH), F32),
        scratch_shapes=scratch,
        compiler_params=_cparams(("arbitrary", "arbitrary")),
        name="win_attn" if local else "ctx_attn",
    )(*args)


def _head_of(shape, dim):
    return lax.broadcasted_iota(jnp.int32, shape, dim) // HEAD_DIM


def _group_mean(t, avg):
    hi = t.astype(BF16)
    lo = (t - hi.astype(F32)).astype(BF16)
    return (jnp.dot(hi, avg, preferred_element_type=F32) + jnp.dot(lo, avg, preferred_element_type=F32))


def _ret_kernel(*refs, reverse, finalize):
    if finalize:
        lg_ref, lgl_ref, q_ref, k_ref, v_ref, r0_ref, yin_ref, y_ref, rfin_ref, r_scr = refs
    else:
        lg_ref, lgl_ref, q_ref, k_ref, v_ref, r0_ref, y_ref, rfin_ref, r_scr = refs
    c = pl.program_id(0)
    nb, ch, w = q_ref.shape

    @pl.when(c == 0)
    def _():
        r_scr[...] = r0_ref[...]

    lgl = lgl_ref[...]
    pos = lax.broadcasted_iota(jnp.int32, (ch, 1), 0).astype(F32)
    if reverse:
        xi = jnp.exp(lgl * (ch - pos))
        zeta = jnp.exp(lgl * pos)
    else:
        xi = jnp.exp(lgl * (pos + 1.0))
        zeta = jnp.exp(lgl * (ch - 1.0 - pos))
    g_chunk = jnp.exp(lgl * float(ch))
    ri = lax.broadcasted_iota(jnp.int32, (ch, ch), 0)
    ci = lax.broadcasted_iota(jnp.int32, (ch, ch), 1)
    diff = (ci - ri) if reverse else (ri - ci)
    dist = jnp.maximum(diff, 0).astype(F32)
    decays = [jnp.where(diff >= 0, jnp.exp(lg_ref[h] * dist), 0.0) for h in range(R_HEADS)]
    lane_head = _head_of((ch, w), 1)
    same_head = _head_of((w, w), 0) == _head_of((w, w), 1)
    avg = jnp.where(same_head, 1.0 / HEAD_DIM, 0.0).astype(BF16)

    for bi in range(nb):
        q = q_ref[bi]
        k = k_ref[bi]
        v = v_ref[bi]
        r = r_scr[bi]
        cross = jnp.dot((q * xi).astype(BF16), r.astype(BF16), preferred_element_type=F32)
        q4 = jnp.concatenate([jnp.where(lane_head == h, q, 0.0) for h in range(R_HEADS)], axis=0).astype(BF16)
        sc = lax.dot_general(q4, k.astype(BF16), (((1,), (1,)), ((), ())), preferred_element_type=F32)
        s4 = jnp.concatenate([(sc[h * ch:(h + 1) * ch] * decays[h]).astype(BF16) for h in range(R_HEADS)], axis=1)
        v4 = jnp.concatenate([jnp.where(lane_head == h, v, 0.0) for h in range(R_HEADS)], axis=0).astype(BF16)
        y = jnp.dot(s4, v4, preferred_element_type=F32) + cross

        kz = (k * zeta).astype(BF16)
        ktv = lax.dot_general(kz, v.astype(BF16), (((0,), (0,)), ((), ())), preferred_element_type=F32)
        r_new = g_chunk * r + jnp.where(same_head, ktv, 0.0)
        r_scr[bi] = r_new
        rfin_ref[bi] = r_new

        if finalize:
            y = y + yin_ref[bi]
            mu = _group_mean(y, avg)
            d = y - mu
            var = _group_mean(d * d, avg)
            y = d * lax.rsqrt(var + EPS)
        y_ref[bi] = y


def _ret_call(lg, lgl, q, k, v, r0, y_in, *, reverse):
    b, s, w = q.shape
    ch = min(R_STEP, s)
    nc = s // ch
    finalize = y_in is not None
    cidx = (lambda c: (0, nc - 1 - c, 0)) if reverse else (lambda c: (0, c, 0))
    chunk = pl.BlockSpec((b, ch, w), cidx)
    state = pl.BlockSpec((b, w, w), lambda c: (0, 0, 0))
    in_specs = [pl.BlockSpec(memory_space=pltpu.SMEM), pl.BlockSpec((1, w), lambda c: (0, 0)),
                chunk, chunk, chunk, state]
    args = [lg, lgl, q, k, v, r0]
    if finalize:
        in_specs.append(chunk)
        args.append(y_in)
    return pl.pallas_call(
        functools.partial(_ret_kernel, reverse=reverse, finalize=finalize),
        grid=(nc,),
        in_specs=in_specs,
        out_specs=[chunk, state],
        out_shape=[jax.ShapeDtypeStruct((b, s, w), F32), jax.ShapeDtypeStruct((b, w, w), F32)],
        scratch_shapes=[pltpu.VMEM((b, w, w), F32)],
        compiler_params=_cparams(("arbitrary",)),
        name="retention_bwd" if reverse else "retention_fwd",
    )(*args)


def _dft_cs(n, scale):
    a = 2.0 * np.pi * np.outer(np.arange(n), np.arange(n)) / n
    return np.cos(a) * scale, np.sin(a) * scale


def _fourier_rows_kernel(m_ref, tc_ref, ts_ref, z_ref, o_ref):
    rows, cb = z_ref.shape[1], z_ref.shape[2]
    m = m_ref[...].astype(BF16)
    for ci in range(cb):
        z = z_ref[0, :, ci, :]
        zz = jnp.concatenate([z[:, :F_WIDTH], z[:, F_WIDTH:]], axis=0).astype(BF16)
        a = jnp.dot(m, zz, preferred_element_type=F32)
        a_re, a_im = a[:rows], a[rows:]
        tc = jnp.concatenate([tc_ref[ci]] * (F_WIDTH // LANES), axis=1)
        ts = jnp.concatenate([ts_ref[ci]] * (F_WIDTH // LANES), axis=1)
        o_ref[0, :, 0, ci, :] = (a_re * tc + a_im * ts).astype(o_ref.dtype)
        o_ref[0, :, 1, ci, :] = (a_im * tc - a_re * ts).astype(o_ref.dtype)


def _fourier_cols_kernel(g_ref, b_ref, o_ref):
    g = g_ref[...].astype(BF16)
    kt, _, cw, f = b_ref.shape[1:]
    for i in range(kt):
        bm = b_ref[0, i].reshape(2 * cw, f).astype(BF16)
        o_ref[0, :, i, :] = jnp.dot(g, bm, preferred_element_type=F32)


def _fourier_latent(zf, rows):
    b, s, _ = zf.shape
    cw = GRID_W
    c_r, s_r = _dft_cs(rows, rows ** -0.5)
    m1 = jnp.asarray(np.block([[c_r, s_r], [-s_r, c_r]]), F32)
    ang = 2.0 * np.pi * np.outer(np.arange(cw), np.arange(rows)) / s
    tc = jnp.asarray(np.repeat(np.cos(ang)[:, :, None], LANES, axis=2), F32)
    ts = jnp.asarray(np.repeat(np.sin(ang)[:, :, None], LANES, axis=2), F32)
    cb = 16
    bk = pl.pallas_call(
        _fourier_rows_kernel,
        grid=(cw // cb, b),
        in_specs=[_const_spec(m1.shape),
                  pl.BlockSpec((cb, rows, LANES), lambda j, bi: (j, 0, 0)),
                  pl.BlockSpec((cb, rows, LANES), lambda j, bi: (j, 0, 0)),
                  pl.BlockSpec((1, rows, cb, 2 * F_WIDTH), lambda j, bi: (bi, 0, j, 0))],
        out_specs=pl.BlockSpec((1, rows, 2, cb, F_WIDTH), lambda j, bi: (bi, 0, 0, j, 0)),
        out_shape=jax.ShapeDtypeStruct((b, rows, 2, cw, F_WIDTH), F32),
        compiler_params=_cparams(("arbitrary", "arbitrary")),
        name="fourier_rows",
    )(m1, tc, ts, zf.reshape(b, rows, cw, 2 * F_WIDTH))
    c_c, s_c = _dft_cs(cw, cw ** -0.5)
    g = jnp.asarray(np.concatenate([c_c, s_c], axis=1), F32)
    kt = 16
    out = pl.pallas_call(
        _fourier_cols_kernel,
        grid=(b, rows // kt),
        in_specs=[_const_spec(g.shape),
                  pl.BlockSpec((1, kt, 2, cw, F_WIDTH), lambda bi, i: (bi, i, 0, 0, 0))],
        out_specs=pl.BlockSpec((1, cw, kt, F_WIDTH), lambda bi, i: (bi, 0, i, 0)),
        out_shape=jax.ShapeDtypeStruct((b, cw, rows, F_WIDTH), F32),
        compiler_params=_cparams(("arbitrary", "arbitrary")),
        name="fourier_cols",
    )(g, bk)
    return out.reshape(b, s, F_WIDTH)


def _fourier_dense_kernel(m_ref, z_ref, o_ref):
    z = z_ref[0].astype(F32)
    zz = jnp.concatenate([z[:, :F_WIDTH], z[:, F_WIDTH:]], axis=0)
    o_ref[0] = jnp.dot(m_ref[...], zz, preferred_element_type=F32, precision=HI)


def _fourier_dense(zf):
    b, n, _ = zf.shape
    c_n, s_n = _dft_cs(n, n ** -0.5)
    m = jnp.asarray(np.concatenate([c_n, s_n], axis=1), F32)
    return pl.pallas_call(
        _fourier_dense_kernel,
        grid=(b,),
        in_specs=[_const_spec(m.shape), pl.BlockSpec((1, n, 2 * F_WIDTH), lambda bi: (bi, 0, 0))],
        out_specs=pl.BlockSpec((1, n, F_WIDTH), lambda bi: (bi, 0, 0)),
        out_shape=jax.ShapeDtypeStruct((b, n, F_WIDTH), F32),
        compiler_params=_cparams(("arbitrary",)),
        name="fourier_dense",
    )(m, zf)


def _merge_kernel(x_ref, mod_ref, gpre_ref, gpost_ref, ya_ref, wc_ref, wprev_ref, wnext_ref, bb_ref,
                  yr_ref, yf_ref, cw_ref, cb_ref, wza_ref, wzb_ref, wzr_ref, wfm_ref,
                  woa_ref, wob_ref, wor_ref, wof_ref, wout_ref, o_ref, *, tiles_per_seq):
    i = pl.program_id(0)
    x = x_ref[...]
    mod = mod_ref[0]
    h = _modulated_norm(x, gpre_ref[...], mod).astype(BF16)
    tm = x.shape[0]

    wc = wc_ref[...]
    t = i % tiles_per_seq
    prev_row = jnp.where(t > 0, wprev_ref[7:8, :], 0.0)
    next_row = jnp.where(t < tiles_per_seq - 1, wnext_ref[0:1, :], 0.0)
    row = lax.broadcasted_iota(jnp.int32, wc.shape, 0)
    up = jnp.where(row == 0, prev_row, pltpu.roll(wc, 1, 0))
    dn = jnp.where(row == tm - 1, next_row, pltpu.roll(wc, tm - 1, 0))
    conv = up * cw_ref[0:1, :] + wc * cw_ref[1:2, :] + dn * cw_ref[2:3, :] + cb_ref[...]
    yb = bb_ref[...] * conv

    ys = (ya_ref[...], yb, yr_ref[...], yf_ref[...])
    wo_refs = (woa_ref, wob_ref, wor_ref, wof_ref)
    fz = _SIZE["f_z"]
    wz = (wza_ref[0], wzb_ref[0], wzr_ref[0], wfm_ref[0, :, 0:fz])
    acts = []
    for br in range(N_BRANCH):
        z = jnp.dot(h, wz[br], preferred_element_type=F32)
        acts.append((ys[br] * (z * _sigmoid(z))).astype(BF16))
    y = jnp.zeros((tm, D_MODEL), F32)
    for c0 in range(0, D_MODEL, MERGE_CHUNK):
        total = jnp.zeros((tm, MERGE_CHUNK), F32)
        for br in range(N_BRANCH):
            proj = jnp.dot(acts[br], wo_refs[br][0, :, c0:c0 + MERGE_CHUNK], preferred_element_type=F32)
            g0 = fz + br * D_MODEL + c0
            gate = _sigmoid(jnp.dot(h, wfm_ref[0, :, g0:g0 + MERGE_CHUNK], preferred_element_type=F32))
            total = total + gate * proj
        y = y + jnp.dot(total.astype(BF16), wout_ref[0, c0:c0 + MERGE_CHUNK, :], preferred_element_type=F32)
    ms = jnp.mean(y * y, axis=-1, keepdims=True)
    yn = y * lax.rsqrt(ms + EPS) * gpost_ref[...]
    o_ref[...] = x + mod[:, 2 * D_MODEL:3 * D_MODEL] * yn


def _merge_call(x2, mod3, g_pre, g_post, ya, wc, bb, yr, yf, conv_w, conv_b, w_all, wos, wout, layer,
                *, tm, tiles_per_group, tiles_per_seq):
    rows, d = x2.shape
    nt = rows // tm
    hb = tm // 8
    nhb = rows // 8
    row_spec = lambda w: pl.BlockSpec((tm, w), lambda i: (i, 0))
    cols = functools.partial(_cols_spec, w_all, layer)
    return pl.pallas_call(
        functools.partial(_merge_kernel, tiles_per_seq=tiles_per_seq),
        grid=(nt,),
        in_specs=[row_spec(d),
                  pl.BlockSpec((1, 1, 3 * d), lambda i: (i // tiles_per_group, 0, 0)),
                  _const_spec((1, d)), _const_spec((1, d)),
                  row_spec(A_WIDTH),
                  row_spec(B_WIDTH),
                  pl.BlockSpec((8, B_WIDTH), lambda i: (jnp.maximum(i * hb - 1, 0), 0)),
                  pl.BlockSpec((8, B_WIDTH), lambda i: (jnp.minimum((i + 1) * hb, nhb - 1), 0)),
                  row_spec(B_WIDTH), row_spec(R_WIDTH), row_spec(F_WIDTH),
                  _const_spec(conv_w.shape), _const_spec(conv_b.shape),
                  cols("a_z", "a_z"), cols("b_z", "b_z"), cols("r_z", "r_z"), cols("f_z", "merge"),
                  *[_layer_spec(w, layer) for w in wos], _layer_spec(wout, layer)],
        out_specs=row_spec(d),
        out_shape=jax.ShapeDtypeStruct((rows, d), F32),
        compiler_params=_cparams(("arbitrary",)),
        name="merge",
    )(x2, mod3, g_pre, g_post, ya, wc, wc, wc, bb, yr, yf, conv_w, conv_b, w_all, w_all, w_all, w_all, *wos, wout)


def _rope_tables(n):
    rows = n // GRID_W
    row = jnp.broadcast_to(jnp.arange(rows)[:, None], (rows, GRID_W)).reshape(-1).astype(F32)
    col = jnp.broadcast_to(jnp.arange(GRID_W)[None, :], (rows, GRID_W)).reshape(-1).astype(F32)
    half = HEAD_DIM // 2
    inv = ROPE_BASE ** (-jnp.arange(0, half, 2, dtype=F32) / half)
    ang_r = row[:, None] * inv
    ang_c = col[:, None] * inv
    cos = jnp.concatenate([jnp.cos(ang_r), jnp.cos(ang_r), jnp.cos(ang_c), jnp.cos(ang_c)], axis=1)
    sin = jnp.concatenate([-jnp.sin(ang_r), jnp.sin(ang_r), -jnp.sin(ang_c), jnp.sin(ang_c)], axis=1)
    return jnp.tile(cos, (1, LANES // HEAD_DIM)), jnp.tile(sin, (1, LANES // HEAD_DIM))


def kernel(x, c, ctx, c_ctx, w_ada, b_ada, norm_pre, norm_post, w_in, attn_sink, conv_w, conv_b, ret_decay,
           w_o_attn, w_o_conv, w_o_ret, w_o_fourier, w_out):
    b, s, d = x.shape
    lc = ctx.shape[1]
    depth = w_in.shape[0]
    rows_grid = s // GRID_W

    cv = jnp.zeros((8, d), F32).at[:b].set(c).at[b].set(c_ctx)
    mods = _ada_call(cv, w_ada, b_ada)

    cos_x, sin_x = _rope_tables(s)
    cos_c = jnp.ones((lc, LANES), F32)
    sin_c = jnp.zeros((lc, LANES), F32)

    c64, s64 = _dft_cs(HEAD_DIM, HEAD_DIM ** -0.5)
    eye = np.eye(F_WIDTH // HEAD_DIM)
    fd = jnp.asarray(np.concatenate([np.kron(eye, c64), -np.kron(eye, s64)], axis=1), F32)

    lg_all = jax.nn.log_sigmoid(ret_decay.astype(F32))
    zero_state = jnp.zeros((b, R_WIDTH, R_WIDTH), F32)

    w_all = w_in.astype(BF16)
    wos = tuple(w.astype(BF16) for w in (w_o_attn, w_o_conv, w_o_ret, w_o_fourier))
    wout = w_out.astype(BF16)

    tm_x = 1024
    tm_m = 1024
    x2 = x.reshape(b * s, d)
    xc2 = ctx.reshape(b * lc, d)
    for l in range(depth):
        update_ctx = l < depth - 1
        g_pre = norm_pre[l].reshape(1, d)
        g_post = norm_post[l].reshape(1, d)
        mod_x = mods[l, :b].reshape(b, 1, 3 * d)
        mod_c = mods[l, b:b + 1].reshape(1, 1, 3 * d)
        cb = conv_b[l].reshape(1, B_WIDTH)
        lg = lg_all[l]
        lgl = jnp.repeat(lg, HEAD_DIM, axis=1)

        px = _inproj_call(x2, mod_x, g_pre, w_all, l, cos_x, sin_x, fd, tm=tm_x, tiles_per_group=s // tm_x)
        pc = _inproj_call(xc2, mod_c, g_pre, w_all, l, cos_c, sin_c, fd, tm=lc, tiles_per_group=b)
        qa, ka, va, wcx, bbx, qr, kr, vr, zf = [t.reshape(b, s, -1) for t in px]
        qac, kac, vac, wcc, bbc, qrc, krc, vrc, zfc = [t.reshape(b, lc, -1) for t in pc]

        ya = _attn_call(attn_sink[l], qa, ka, va, kac, vac, local=True)

        ycf, st_f = _ret_call(lg[0], lgl[0:1], qrc, krc, vrc, zero_state, None, reverse=False)
        ycr, st_b = _ret_call(lg[1], lgl[1:2], qrc, krc, vrc, zero_state, ycf, reverse=True)
        yf_, _ = _ret_call(lg[0], lgl[0:1], qr, kr, vr, st_f, None, reverse=False)
        yr, _ = _ret_call(lg[1], lgl[1:2], qr, kr, vr, st_b, yf_, reverse=True)

        yfo = _fourier_latent(zf, rows_grid)

        x2_new = _merge_call(x2, mod_x, g_pre, g_post, ya.reshape(b * s, -1), wcx.reshape(b * s, -1),
                             bbx.reshape(b * s, -1), yr.reshape(b * s, -1), yfo.reshape(b * s, -1),
                             conv_w[l], cb, w_all, wos, wout, l,
                             tm=tm_m, tiles_per_group=s // tm_m, tiles_per_seq=s // tm_m)
        if update_ctx:
            yac = _attn_call(attn_sink[l], qac, None, None, kac, vac, local=False)
            yfc = _fourier_dense(zfc)
            xc2 = _merge_call(xc2, mod_c, g_pre, g_post, yac.reshape(b * lc, -1), wcc.reshape(b * lc, -1),
                              bbc.reshape(b * lc, -1), ycr.reshape(b * lc, -1), yfc.reshape(b * lc, -1),
                              conv_w[l], cb, w_all, wos, wout, l,
                              tm=lc, tiles_per_group=b, tiles_per_seq=1)
        x2 = x2_new
    return x2.reshape(b, s, d)
```

```python
import functools
import math

import numpy as np
import jax
import jax.numpy as jnp
from jax import lax
from jax.experimental import pallas as pl
from jax.experimental.pallas import tpu as pltpu

D_MODEL = 1024
GRID_W = 64
HEAD_DIM = 64
EPS = 1e-6
NEG_INF = -1e30
A_HEADS = 8
A_KV_HEADS = 2
A_BLOCK = 128
A_WIDTH = A_HEADS * HEAD_DIM
A_KV_WIDTH = A_KV_HEADS * HEAD_DIM
ROPE_BASE = 10000.0
B_WIDTH = 256
R_HEADS = 4
R_WIDTH = R_HEADS * HEAD_DIM
R_STEP = 256
F_WIDTH = 256
N_BRANCH = 4

IN_NAMES = ("a_q", "a_k", "a_v", "a_z", "b_u", "b_b", "b_c", "b_z", "r_q", "r_k", "r_v", "r_z", "f_u", "f_z", "merge")
IN_SIZES = (A_WIDTH, A_KV_WIDTH, A_KV_WIDTH, A_WIDTH, B_WIDTH, B_WIDTH, B_WIDTH, B_WIDTH,
            R_WIDTH, R_WIDTH, R_WIDTH, R_WIDTH, F_WIDTH, F_WIDTH, N_BRANCH * D_MODEL)
_OFFS = dict(zip(IN_NAMES, np.cumsum((0,) + IN_SIZES)[:-1].tolist()))
_SIZE = dict(zip(IN_NAMES, IN_SIZES))

Z_NAMES = ("a_z", "b_z", "r_z", "f_z")
Z_SIZES = tuple(_SIZE[n] for n in Z_NAMES)

MERGE_CHUNK = 1024
ATTN_SUB = 16
LOG2E = math.log2(math.e)
LANES = 128
VMEM_LIMIT = 56 * 1024 * 1024

BF16 = jnp.bfloat16
F32 = jnp.float32
HI = lax.Precision.HIGHEST


def _cparams(sem):
    return pltpu.CompilerParams(dimension_semantics=sem, vmem_limit_bytes=VMEM_LIMIT)


def _const_spec(shape):
    nd = len(shape)
    return pl.BlockSpec(shape, lambda *_: (0,) * nd)


def _sigmoid(v):
    return 1.0 / (1.0 + jnp.exp(-v))


def _ada_kernel(cv_ref, w_ref, b_ref, o_ref):
    cv = cv_ref[...]
    s = cv * _sigmoid(cv)
    o_ref[0] = jnp.dot(s, w_ref[0], preferred_element_type=F32, precision=HI) + b_ref[0]


def _ada_call(cv, w_ada, b_ada):
    depth, d, d3 = w_ada.shape
    tn = 1024
    return pl.pallas_call(
        _ada_kernel,
        grid=(depth, d3 // tn),
        in_specs=[pl.BlockSpec((8, d), lambda l, j: (0, 0)),
                  pl.BlockSpec((1, d, tn), lambda l, j: (l, 0, j)),
                  pl.BlockSpec((1, 1, tn), lambda l, j: (l, 0, j))],
        out_specs=pl.BlockSpec((1, 8, tn), lambda l, j: (l, 0, j)),
        out_shape=jax.ShapeDtypeStruct((depth, 8, d3), F32),
        compiler_params=_cparams(("arbitrary", "arbitrary")),
        name="ada_mod",
    )(cv, w_ada, b_ada.reshape(depth, 1, d3))


def _modulated_norm(x, g, mod):
    ms = jnp.mean(x * x, axis=-1, keepdims=True)
    y = x * lax.rsqrt(ms + EPS) * g
    return y * (1.0 + mod[:, D_MODEL:2 * D_MODEL]) + mod[:, 0:D_MODEL]


def _rope(t, cos, sin_signed, first_half):
    outs = []
    for j in range(t.shape[1] // LANES):
        tj = t[:, j * LANES:(j + 1) * LANES]
        partner = jnp.where(first_half, pltpu.roll(tj, LANES - 16, 1), pltpu.roll(tj, 16, 1))
        outs.append(tj * cos + partner * sin_signed)
    return outs[0] if len(outs) == 1 else jnp.concatenate(outs, axis=1)


MIX_GROUPS = (("a_q", "a_v"), ("b_u", "b_c"), ("r_q", "r_v"), ("f_u", "f_u"))


def _inproj_kernel(x_ref, mod_ref, g_ref, wa_ref, wb_ref, wr_ref, wf_ref, cos_ref, sin_ref, fd_ref,
                   qa_ref, ka_ref, va_ref, wc_ref, bb_ref, qr_ref, kr_ref, vr_ref, zf_ref):
    h = _modulated_norm(x_ref[...], g_ref[...], mod_ref[0]).astype(BF16)
    w_refs = {"a": wa_ref, "b": wb_ref, "r": wr_ref, "f": wf_ref}
    starts = {first[0]: _OFFS[first] for first, _ in MIX_GROUPS}

    def proj(name):
        o = _OFFS[name] - starts[name[0]]
        return jnp.dot(h, w_refs[name[0]][0, :, o:o + _SIZE[name]], preferred_element_type=F32)

    cos = cos_ref[...]
    sin = sin_ref[...]
    lane = lax.broadcasted_iota(jnp.int32, cos.shape, 1)
    first_half = (lane % 32) < 16
    low_half = lane < HEAD_DIM
    k_scale = HEAD_DIM ** -0.5
    q = _rope(proj("a_q"), cos, sin, first_half) * (k_scale * LOG2E)
    for p in range(A_WIDTH // LANES):
        qp = q[:, p * LANES:(p + 1) * LANES]
        qa_ref[:, (2 * p) * LANES:(2 * p + 1) * LANES] = jnp.where(low_half, qp, 0.0).astype(BF16)
        qa_ref[:, (2 * p + 1) * LANES:(2 * p + 2) * LANES] = jnp.where(low_half, 0.0, qp).astype(BF16)
    k = _rope(proj("a_k"), cos, sin, first_half)
    k_sw = pltpu.roll(k, HEAD_DIM, 1)
    ka_ref[:, 0:LANES] = jnp.where(low_half, k, k_sw).astype(BF16)
    ka_ref[:, LANES:2 * LANES] = jnp.where(low_half, k_sw, k).astype(BF16)
    v = proj("a_v")
    v_sw = pltpu.roll(v, HEAD_DIM, 1)
    va_ref[:, 0:LANES] = jnp.where(low_half, v, 1.0).astype(BF16)
    va_ref[:, LANES:2 * LANES] = jnp.where(low_half, 1.0, v_sw).astype(BF16)
    va_ref[:, 2 * LANES:3 * LANES] = jnp.where(low_half, v_sw, 1.0).astype(BF16)
    va_ref[:, 3 * LANES:4 * LANES] = jnp.where(low_half, 1.0, v).astype(BF16)
    wc_ref[...] = proj("b_c") * proj("b_u")
    bb_ref[...] = proj("b_b")
    qr_ref[...] = _rope(proj("r_q"), cos, sin, first_half)
    kr_ref[...] = _rope(proj("r_k"), cos, sin, first_half) * k_scale
    vr_ref[...] = proj("r_v")
    zf_ref[...] = jnp.dot(proj("f_u").astype(BF16), fd_ref[...].astype(BF16), preferred_element_type=F32)


def _layer_spec(arr, layer):
    return pl.BlockSpec((1,) + arr.shape[1:], lambda *_: (layer,) + (0,) * (arr.ndim - 1),
                        pipeline_mode=pl.Buffered(1))


def _cols_spec(w_all, layer, first, last):
    lo = _OFFS[first]
    width = _OFFS[last] + _SIZE[last] - lo
    return pl.BlockSpec((pl.Element(1), pl.Element(w_all.shape[1]), pl.Element(width)),
                        lambda *_: (layer, 0, lo), pipeline_mode=pl.Buffered(1))


def _inproj_call(x2, mod3, g_pre, w_all, layer, cos_t, sin_t, fd, *, tm, tiles_per_group):
    rows, d = x2.shape
    nt = rows // tm
    tiles_per_seq = cos_t.shape[0] // tm
    widths = (2 * A_WIDTH, 2 * A_KV_WIDTH, 4 * A_KV_WIDTH, B_WIDTH, B_WIDTH, R_WIDTH, R_WIDTH, R_WIDTH, 2 * F_WIDTH)
    dtypes = (BF16, BF16, BF16, F32, F32, F32, F32, F32, F32)
    row_spec = lambda w: pl.BlockSpec((tm, w), lambda i: (i, 0))
    return pl.pallas_call(
        _inproj_kernel,
        grid=(nt,),
        in_specs=[row_spec(d),
                  pl.BlockSpec((1, 1, 3 * d), lambda i: (i // tiles_per_group, 0, 0)),
                  _const_spec((1, d)),
                  *[_cols_spec(w_all, layer, first, last) for first, last in MIX_GROUPS],
                  pl.BlockSpec((tm, LANES), lambda i: (i % tiles_per_seq, 0)),
                  pl.BlockSpec((tm, LANES), lambda i: (i % tiles_per_seq, 0)),
                  _const_spec(fd.shape)],
        out_specs=[row_spec(w) for w in widths],
        out_shape=[jax.ShapeDtypeStruct((rows, w), dt) for w, dt in zip(widths, dtypes)],
        compiler_params=_cparams(("arbitrary",)),
        name="in_proj",
    )(x2, mod3, g_pre, w_all, w_all, w_all, w_all, cos_t, sin_t, fd)


def _attn_kernel(*refs, local):
    if local:
        (sink_ref, q_ref, kp_ref, kc_ref, kn_ref, vp_ref, vc_ref, vn_ref, kx_ref, vx_ref, o_ref,
         kwin, vwin) = refs
    else:
        sink_ref, q_ref, kx_ref, vx_ref, o_ref = refs
    blk = A_BLOCK
    nsub = q_ref.shape[1] // blk
    n = pl.program_id(1)
    last = pl.num_programs(1) * nsub - 1
    contract_last = (((1,), (1,)), ((), ()))

    lane_q = lax.broadcasted_iota(jnp.int32, (blk, LANES), 1)
    nloc = 3 * blk
    if local:
        step = nsub * blk
        kwin[0:blk] = kp_ref[0]
        kwin[blk:blk + step] = kc_ref[0]
        kwin[blk + step:2 * blk + step] = kn_ref[0]
        vwin[0:blk] = vp_ref[0]
        vwin[blk:blk + step] = vc_ref[0]
        vwin[blk + step:2 * blk + step] = vn_ref[0]
        row = lax.broadcasted_iota(jnp.int32, (blk, nloc), 0)
        col = lax.broadcasted_iota(jnp.int32, (blk, nloc), 1)
        band = jnp.minimum(col - row, row + 2 * blk - col)

    for sub in range(nsub):
        q0 = sub * blk
        if local:
            g_blk = n * nsub + sub
            seq_lo = blk - g_blk * blk
            seq_hi = blk + (last - g_blk + 1) * blk
            valid = (jnp.minimum(band, jnp.minimum(col - seq_lo, seq_hi - 1 - col)) >= 0)[None]
        for j in range(A_KV_HEADS):
            kcol = slice(j * LANES, (j + 1) * LANES)
            ecol = slice(2 * j * LANES, (2 * j + 1) * LANES)
            ocol = slice((2 * j + 1) * LANES, (2 * j + 2) * LANES)
            qs = jnp.concatenate([q_ref[0, q0:q0 + blk, (4 * j + g) * LANES:(4 * j + g + 1) * LANES]
                                  for g in range(4)], axis=0)
            s = lax.dot_general(qs, kx_ref[0, :, kcol], contract_last, preferred_element_type=F32)
            if local:
                s_loc = lax.dot_general(qs, kwin[q0:q0 + nloc, kcol], contract_last, preferred_element_type=F32)
                s_loc = jnp.where(valid, s_loc.reshape(4, blk, nloc), NEG_INF).reshape(4 * blk, nloc)
                s = jnp.concatenate([s_loc, s], axis=1)
            sink = jnp.concatenate(
                [jnp.full((blk, 1), sink_ref[4 * j + g] * LOG2E, F32) for g in range(4)], axis=0)
            m = jnp.maximum(jnp.max(s, axis=1, keepdims=True), sink)
            pb = jnp.exp2((s - m).astype(BF16))
            e_sink = jnp.exp2(sink - m)

            vcol = slice(2 * j * LANES, (2 * j + 2) * LANES)
            if local:
                o_all = (jnp.dot(pb[:, :nloc], vwin[q0:q0 + nloc, vcol], preferred_element_type=F32)
                         + jnp.dot(pb[:, nloc:], vx_ref[0, :, vcol], preferred_element_type=F32))
            else:
                o_all = jnp.dot(pb, vx_ref[0, :, vcol], preferred_element_type=F32)

            for pair in range(2):
                r0 = 2 * pair * blk
                o_even = o_all[r0:r0 + blk, :LANES]
                o_odd = o_all[r0 + blk:r0 + 2 * blk, LANES:]
                num = jnp.where(lane_q < 64, o_even, o_odd)
                den = (pltpu.roll(jnp.where(lane_q < 64, o_odd, o_even), 64, 1)
                       + jnp.where(lane_q < 64, e_sink[r0:r0 + blk], e_sink[r0 + blk:r0 + 2 * blk]))
                c0 = (2 * j + pair) * LANES
                o_ref[0, q0:q0 + blk, c0:c0 + LANES] = num / den


def _attn_call(sink, q, k, v, kx, vx, *, local):
    b, sq, _ = q.shape
    blk = A_BLOCK
    nq = sq // blk
    nsub = min(ATTN_SUB, nq)
    step = nsub * blk
    lx = kx.shape[1]
    smem = pl.BlockSpec(memory_space=pltpu.SMEM)
    kw, vw = kx.shape[2], vx.shape[2]
    q_spec = pl.BlockSpec((1, step, q.shape[2]), lambda bi, n: (bi, n, 0))
    if local:
        def halo(w):
            return (pl.BlockSpec((1, blk, w), lambda bi, n: (bi, jnp.maximum(n * nsub - 1, 0), 0)),
                    pl.BlockSpec((1, step, w), lambda bi, n: (bi, n, 0)),
                    pl.BlockSpec((1, blk, w), lambda bi, n: (bi, jnp.minimum((n + 1) * nsub, nq - 1), 0)))
        in_specs = [smem, q_spec, *halo(kw), *halo(vw),
                    pl.BlockSpec((1, lx, kw), lambda bi, n: (bi, 0, 0)),
                    pl.BlockSpec((1, lx, vw), lambda bi, n: (bi, 0, 0))]
        args = (sink, q, k, k, k, v, v, v, kx, vx)
        scratch = [pltpu.VMEM((step + 2 * blk, kw), k.dtype), pltpu.VMEM((step + 2 * blk, vw), v.dtype)]
    else:
        in_specs = [smem, q_spec, pl.BlockSpec((1, lx, kw), lambda bi, n: (bi, 0, 0)),
                    pl.BlockSpec((1, lx, vw), lambda bi, n: (bi, 0, 0))]
        args = (sink, q, kx, vx)
        scratch = []
    return pl.pallas_call(
        functools.partial(_attn_kernel, local=local),
        grid=(b, nq // nsub),
        in_specs=in_specs,
        out_specs=pl.BlockSpec((1, step, A_WIDTH), lambda bi, n: (bi, n, 0)),
        out_shape=jax.ShapeDtypeStruct((b, sq, A_WIDTH), F32),
        scratch_shapes=scratch,
        compiler_params=_cparams(("arbitrary", "arbitrary")),
        name="win_attn" if local else "ctx_attn",
    )(*args)


def _head_of(shape, dim):
    return lax.broadcasted_iota(jnp.int32, shape, dim) // HEAD_DIM


def _group_mean(t, avg):
    hi = t.astype(BF16)
    lo = (t - hi.astype(F32)).astype(BF16)
    return (jnp.dot(hi, avg, preferred_element_type=F32) + jnp.dot(lo, avg, preferred_element_type=F32))


def _ret_kernel(*refs, reverse, finalize):
    if finalize:
        lg_ref, lgl_ref, q_ref, k_ref, v_ref, r0_ref, yin_ref, y_ref, rfin_ref, r_scr = refs
    else:
        lg_ref, lgl_ref, q_ref, k_ref, v_ref, r0_ref, y_ref, rfin_ref, r_scr = refs
    c = pl.program_id(0)
    nb, ch, w = q_ref.shape

    @pl.when(c == 0)
    def _():
        r_scr[...] = r0_ref[...]

    lgl = lgl_ref[...]
    pos = lax.broadcasted_iota(jnp.int32, (ch, 1), 0).astype(F32)
    if reverse:
        xi = jnp.exp(lgl * (ch - pos))
        zeta = jnp.exp(lgl * pos)
    else:
        xi = jnp.exp(lgl * (pos + 1.0))
        zeta = jnp.exp(lgl * (ch - 1.0 - pos))
    g_chunk = jnp.exp(lgl * float(ch))
    ri = lax.broadcasted_iota(jnp.int32, (ch, ch), 0)
    ci = lax.broadcasted_iota(jnp.int32, (ch, ch), 1)
    diff = (ci - ri) if reverse else (ri - ci)
    dist = jnp.maximum(diff, 0).astype(F32)
    decays = [jnp.where(diff >= 0, jnp.exp(lg_ref[h] * dist), 0.0) for h in range(R_HEADS)]
    lane_head = _head_of((ch, w), 1)
    same_head = _head_of((w, w), 0) == _head_of((w, w), 1)
    avg = jnp.where(same_head, 1.0 / HEAD_DIM, 0.0).astype(BF16)

    for bi in range(nb):
        q = q_ref[bi]
        k = k_ref[bi]
        v = v_ref[bi]
        r = r_scr[bi]
        cross = jnp.dot((q * xi).astype(BF16), r.astype(BF16), preferred_element_type=F32)
        q4 = jnp.concatenate([jnp.where(lane_head == h, q, 0.0) for h in range(R_HEADS)], axis=0).astype(BF16)
        sc = lax.dot_general(q4, k.astype(BF16), (((1,), (1,)), ((), ())), preferred_element_type=F32)
        s4 = jnp.concatenate([(sc[h * ch:(h + 1) * ch] * decays[h]).astype(BF16) for h in range(R_HEADS)], axis=1)
        v4 = jnp.concatenate([jnp.where(lane_head == h, v, 0.0) for h in range(R_HEADS)], axis=0).astype(BF16)
        y = jnp.dot(s4, v4, preferred_element_type=F32) + cross

        kz = (k * zeta).astype(BF16)
        ktv = lax.dot_general(kz, v.astype(BF16), (((0,), (0,)), ((), ())), preferred_element_type=F32)
        r_new = g_chunk * r + jnp.where(same_head, ktv, 0.0)
        r_scr[bi] = r_new
        rfin_ref[bi] = r_new

        if finalize:
            y = y + yin_ref[bi]
            mu = _group_mean(y, avg)
            d = y - mu
            var = _group_mean(d * d, avg)
            y = d * lax.rsqrt(var + EPS)
        y_ref[bi] = y


def _ret_call(lg, lgl, q, k, v, r0, y_in, *, reverse):
    b, s, w = q.shape
    ch = min(R_STEP, s)
    nc = s // ch
    finalize = y_in is not None
    cidx = (lambda c: (0, nc - 1 - c, 0)) if reverse else (lambda c: (0, c, 0))
    chunk = pl.BlockSpec((b, ch, w), cidx)
    state = pl.BlockSpec((b, w, w), lambda c: (0, 0, 0))
    in_specs = [pl.BlockSpec(memory_space=pltpu.SMEM), pl.BlockSpec((1, w), lambda c: (0, 0)),
                chunk, chunk, chunk, state]
    args = [lg, lgl, q, k, v, r0]
    if finalize:
        in_specs.append(chunk)
        args.append(y_in)
    return pl.pallas_call(
        functools.partial(_ret_kernel, reverse=reverse, finalize=finalize),
        grid=(nc,),
        in_specs=in_specs,
        out_specs=[chunk, state],
        out_shape=[jax.ShapeDtypeStruct((b, s, w), F32), jax.ShapeDtypeStruct((b, w, w), F32)],
        scratch_shapes=[pltpu.VMEM((b, w, w), F32)],
        compiler_params=_cparams(("arbitrary",)),
        name="retention_bwd" if reverse else "retention_fwd",
    )(*args)


def _dft_cs(n, scale):
    a = 2.0 * np.pi * np.outer(np.arange(n), np.arange(n)) / n
    return np.cos(a) * scale, np.sin(a) * scale


def _fourier_rows_kernel(m_ref, tc_ref, ts_ref, z_ref, o_ref):
    rows, cb = z_ref.shape[1], z_ref.shape[2]
    m = m_ref[...].astype(BF16)
    for ci in range(cb):
        z = z_ref[0, :, ci, :]
        zz = jnp.concatenate([z[:, :F_WIDTH], z[:, F_WIDTH:]], axis=0).astype(BF16)
        a = jnp.dot(m, zz, preferred_element_type=F32)
        a_re, a_im = a[:rows], a[rows:]
        tc = jnp.concatenate([tc_ref[ci]] * (F_WIDTH // LANES), axis=1)
        ts = jnp.concatenate([ts_ref[ci]] * (F_WIDTH // LANES), axis=1)
        o_ref[0, :, 0, ci, :] = (a_re * tc + a_im * ts).astype(o_ref.dtype)
        o_ref[0, :, 1, ci, :] = (a_im * tc - a_re * ts).astype(o_ref.dtype)


def _fourier_cols_kernel(g_ref, b_ref, o_ref):
    g = g_ref[...].astype(BF16)
    kt, _, cw, f = b_ref.shape[1:]
    for i in range(kt):
        bm = b_ref[0, i].reshape(2 * cw, f).astype(BF16)
        o_ref[0, :, i, :] = jnp.dot(g, bm, preferred_element_type=F32)


def _fourier_latent(zf, rows):
    b, s, _ = zf.shape
    cw = GRID_W
    c_r, s_r = _dft_cs(rows, rows ** -0.5)
    m1 = jnp.asarray(np.block([[c_r, s_r], [-s_r, c_r]]), F32)
    ang = 2.0 * np.pi * np.outer(np.arange(cw), np.arange(rows)) / s
    tc = jnp.asarray(np.repeat(np.cos(ang)[:, :, None], LANES, axis=2), F32)
    ts = jnp.asarray(np.repeat(np.sin(ang)[:, :, None], LANES, axis=2), F32)
    cb = 16
    bk = pl.pallas_call(
        _fourier_rows_kernel,
        grid=(cw // cb, b),
        in_specs=[_const_spec(m1.shape),
                  pl.BlockSpec((cb, rows, LANES), lambda j, bi: (j, 0, 0)),
                  pl.BlockSpec((cb, rows, LANES), lambda j, bi: (j, 0, 0)),
                  pl.BlockSpec((1, rows, cb, 2 * F_WIDTH), lambda j, bi: (bi, 0, j, 0))],
        out_specs=pl.BlockSpec((1, rows, 2, cb, F_WIDTH), lambda j, bi: (bi, 0, 0, j, 0)),
        out_shape=jax.ShapeDtypeStruct((b, rows, 2, cw, F_WIDTH), F32),
        compiler_params=_cparams(("arbitrary", "arbitrary")),
        name="fourier_rows",
    )(m1, tc, ts, zf.reshape(b, rows, cw, 2 * F_WIDTH))
    c_c, s_c = _dft_cs(cw, cw ** -0.5)
    g = jnp.asarray(np.concatenate([c_c, s_c], axis=1), F32)
    kt = 16
    out = pl.pallas_call(
        _fourier_cols_kernel,
        grid=(b, rows // kt),
        in_specs=[_const_spec(g.shape),
                  pl.BlockSpec((1, kt, 2, cw, F_WIDTH), lambda bi, i: (bi, i, 0, 0, 0))],
        out_specs=pl.BlockSpec((1, cw, kt, F_WIDTH), lambda bi, i: (bi, 0, i, 0)),
        out_shape=jax.ShapeDtypeStruct((b, cw, rows, F_WIDTH), F32),
        compiler_params=_cparams(("arbitrary", "arbitrary")),
        name="fourier_cols",
    )(g, bk)
    return out.reshape(b, s, F_WIDTH)


def _fourier_dense_kernel(m_ref, z_ref, o_ref):
    z = z_ref[0].astype(F32)
    zz = jnp.concatenate([z[:, :F_WIDTH], z[:, F_WIDTH:]], axis=0)
    o_ref[0] = jnp.dot(m_ref[...], zz, preferred_element_type=F32, precision=HI)


def _fourier_dense(zf):
    b, n, _ = zf.shape
    c_n, s_n = _dft_cs(n, n ** -0.5)
    m = jnp.asarray(np.concatenate([c_n, s_n], axis=1), F32)
    return pl.pallas_call(
        _fourier_dense_kernel,
        grid=(b,),
        in_specs=[_const_spec(m.shape), pl.BlockSpec((1, n, 2 * F_WIDTH), lambda bi: (bi, 0, 0))],
        out_specs=pl.BlockSpec((1, n, F_WIDTH), lambda bi: (bi, 0, 0)),
        out_shape=jax.ShapeDtypeStruct((b, n, F_WIDTH), F32),
        compiler_params=_cparams(("arbitrary",)),
        name="fourier_dense",
    )(m, zf)


def _merge_kernel(x_ref, mod_ref, gpre_ref, gpost_ref, ya_ref, wc_ref, wprev_ref, wnext_ref, bb_ref,
                  yr_ref, yf_ref, cw_ref, cb_ref, wza_ref, wzb_ref, wzr_ref, wfm_ref,
                  woa_ref, wob_ref, wor_ref, wof_ref, wout_ref, o_ref, *, tiles_per_seq):
    i = pl.program_id(0)
    x = x_ref[...]
    mod = mod_ref[0]
    h = _modulated_norm(x, gpre_ref[...], mod).astype(BF16)
    tm = x.shape[0]

    wc = wc_ref[...]
    t = i % tiles_per_seq
    prev_row = jnp.where(t > 0, wprev_ref[7:8, :], 0.0)
    next_row = jnp.where(t < tiles_per_seq - 1, wnext_ref[0:1, :], 0.0)
    row = lax.broadcasted_iota(jnp.int32, wc.shape, 0)
    up = jnp.where(row == 0, prev_row, pltpu.roll(wc, 1, 0))
    dn = jnp.where(row == tm - 1, next_row, pltpu.roll(wc, tm - 1, 0))
    conv = up * cw_ref[0:1, :] + wc * cw_ref[1:2, :] + dn * cw_ref[2:3, :] + cb_ref[...]
    yb = bb_ref[...] * conv

    ys = (ya_ref[...], yb, yr_ref[...], yf_ref[...])
    wo_refs = (woa_ref, wob_ref, wor_ref, wof_ref)
    fz = _SIZE["f_z"]
    wz = (wza_ref[0], wzb_ref[0], wzr_ref[0], wfm_ref[0, :, 0:fz])
    acts = []
    for br in range(N_BRANCH):
        z = jnp.dot(h, wz[br], preferred_element_type=F32)
        acts.append((ys[br] * (z * _sigmoid(z))).astype(BF16))
    y = jnp.zeros((tm, D_MODEL), F32)
    for c0 in range(0, D_MODEL, MERGE_CHUNK):
        total = jnp.zeros((tm, MERGE_CHUNK), F32)
        for br in range(N_BRANCH):
            proj = jnp.dot(acts[br], wo_refs[br][0, :, c0:c0 + MERGE_CHUNK], preferred_element_type=F32)
            g0 = fz + br * D_MODEL + c0
            gate = _sigmoid(jnp.dot(h, wfm_ref[0, :, g0:g0 + MERGE_CHUNK], preferred_element_type=F32))
            total = total + gate * proj
        y = y + jnp.dot(total.astype(BF16), wout_ref[0, c0:c0 + MERGE_CHUNK, :], preferred_element_type=F32)
    ms = jnp.mean(y * y, axis=-1, keepdims=True)
    yn = y * lax.rsqrt(ms + EPS) * gpost_ref[...]
    o_ref[...] = x + mod[:, 2 * D_MODEL:3 * D_MODEL] * yn


def _merge_call(x2, mod3, g_pre, g_post, ya, wc, bb, yr, yf, conv_w, conv_b, w_all, wos, wout, layer,
                *, tm, tiles_per_group, tiles_per_seq):
    rows, d = x2.shape
    nt = rows // tm
    hb = tm // 8
    nhb = rows // 8
    row_spec = lambda w: pl.BlockSpec((tm, w), lambda i: (i, 0))
    cols = functools.partial(_cols_spec, w_all, layer)
    return pl.pallas_call(
        functools.partial(_merge_kernel, tiles_per_seq=tiles_per_seq),
        grid=(nt,),
        in_specs=[row_spec(d),
                  pl.BlockSpec((1, 1, 3 * d), lambda i: (i // tiles_per_group, 0, 0)),
                  _const_spec((1, d)), _const_spec((1, d)),
                  row_spec(A_WIDTH),
                  row_spec(B_WIDTH),
                  pl.BlockSpec((8, B_WIDTH), lambda i: (jnp.maximum(i * hb - 1, 0), 0)),
                  pl.BlockSpec((8, B_WIDTH), lambda i: (jnp.minimum((i + 1) * hb, nhb - 1), 0)),
                  row_spec(B_WIDTH), row_spec(R_WIDTH), row_spec(F_WIDTH),
                  _const_spec(conv_w.shape), _const_spec(conv_b.shape),
                  cols("a_z", "a_z"), cols("b_z", "b_z"), cols("r_z", "r_z"), cols("f_z", "merge"),
                  *[_layer_spec(w, layer) for w in wos], _layer_spec(wout, layer)],
        out_specs=row_spec(d),
        out_shape=jax.ShapeDtypeStruct((rows, d), F32),
        compiler_params=_cparams(("arbitrary",)),
        name="merge",
    )(x2, mod3, g_pre, g_post, ya, wc, wc, wc, bb, yr, yf, conv_w, conv_b, w_all, w_all, w_all, w_all, *wos, wout)


def _rope_tables(n):
    rows = n // GRID_W
    row = jnp.broadcast_to(jnp.arange(rows)[:, None], (rows, GRID_W)).reshape(-1).astype(F32)
    col = jnp.broadcast_to(jnp.arange(GRID_W)[None, :], (rows, GRID_W)).reshape(-1).astype(F32)
    half = HEAD_DIM // 2
    inv = ROPE_BASE ** (-jnp.arange(0, half, 2, dtype=F32) / half)
    ang_r = row[:, None] * inv
    ang_c = col[:, None] * inv
    cos = jnp.concatenate([jnp.cos(ang_r), jnp.cos(ang_r), jnp.cos(ang_c), jnp.cos(ang_c)], axis=1)
    sin = jnp.concatenate([-jnp.sin(ang_r), jnp.sin(ang_r), -jnp.sin(ang_c), jnp.sin(ang_c)], axis=1)
    return jnp.tile(cos, (1, LANES // HEAD_DIM)), jnp.tile(sin, (1, LANES // HEAD_DIM))


def kernel(x, c, ctx, c_ctx, w_ada, b_ada, norm_pre, norm_post, w_in, attn_sink, conv_w, conv_b, ret_decay,
           w_o_attn, w_o_conv, w_o_ret, w_o_fourier, w_out):
    b, s, d = x.shape
    lc = ctx.shape[1]
    depth = w_in.shape[0]
    rows_grid = s // GRID_W

    cv = jnp.zeros((8, d), F32).at[:b].set(c).at[b].set(c_ctx)
    mods = _ada_call(cv, w_ada, b_ada)

    cos_x, sin_x = _rope_tables(s)
    cos_c = jnp.ones((lc, LANES), F32)
    sin_c = jnp.zeros((lc, LANES), F32)

    c64, s64 = _dft_cs(HEAD_DIM, HEAD_DIM ** -0.5)
    eye = np.eye(F_WIDTH // HEAD_DIM)
    fd = jnp.asarray(np.concatenate([np.kron(eye, c64), -np.kron(eye, s64)], axis=1), F32)

    lg_all = jax.nn.log_sigmoid(ret_decay.astype(F32))
    zero_state = jnp.zeros((b, R_WIDTH, R_WIDTH), F32)

    w_all = w_in.astype(BF16)
    wos = tuple(w.astype(BF16) for w in (w_o_attn, w_o_conv, w_o_ret, w_o_fourier))
    wout = w_out.astype(BF16)

    tm_x = 1024
    tm_m = 1024
    x2 = x.reshape(b * s, d)
    xc2 = ctx.reshape(b * lc, d)
    for l in range(depth):
        update_ctx = l < depth - 1
        g_pre = norm_pre[l].reshape(1, d)
        g_post = norm_post[l].reshape(1, d)
        mod_x = mods[l, :b].reshape(b, 1, 3 * d)
        mod_c = mods[l, b:b + 1].reshape(1, 1, 3 * d)
        cb = conv_b[l].reshape(1, B_WIDTH)
        lg = lg_all[l]
        lgl = jnp.repeat(lg, HEAD_DIM, axis=1)

        px = _inproj_call(x2, mod_x, g_pre, w_all, l, cos_x, sin_x, fd, tm=tm_x, tiles_per_group=s // tm_x)
        pc = _inproj_call(xc2, mod_c, g_pre, w_all, l, cos_c, sin_c, fd, tm=lc, tiles_per_group=b)
        qa, ka, va, wcx, bbx, qr, kr, vr, zf = [t.reshape(b, s, -1) for t in px]
        qac, kac, vac, wcc, bbc, qrc, krc, vrc, zfc = [t.reshape(b, lc, -1) for t in pc]

        ya = _attn_call(attn_sink[l], qa, ka, va, kac, vac, local=True)

        ycf, st_f = _ret_call(lg[0], lgl[0:1], qrc, krc, vrc, zero_state, None, reverse=False)
        ycr, st_b = _ret_call(lg[1], lgl[1:2], qrc, krc, vrc, zero_state, ycf, reverse=True)
        yf_, _ = _ret_call(lg[0], lgl[0:1], qr, kr, vr, st_f, None, reverse=False)
        yr, _ = _ret_call(lg[1], lgl[1:2], qr, kr, vr, st_b, yf_, reverse=True)

        yfo = _fourier_latent(zf, rows_grid)

        x2_new = _merge_call(x2, mod_x, g_pre, g_post, ya.reshape(b * s, -1), wcx.reshape(b * s, -1),
                             bbx.reshape(b * s, -1), yr.reshape(b * s, -1), yfo.reshape(b * s, -1),
                             conv_w[l], cb, w_all, wos, wout, l,
                             tm=tm_m, tiles_per_group=s // tm_m, tiles_per_seq=s // tm_m)
        if update_ctx:
            yac = _attn_call(attn_sink[l], qac, None, None, kac, vac, local=False)
            yfc = _fourier_dense(zfc)
            xc2 = _merge_call(xc2, mod_c, g_pre, g_post, yac.reshape(b * lc, -1), wcc.reshape(b * lc, -1),
                              bbc.reshape(b * lc, -1), ycr.reshape(b * lc, -1), yfc.reshape(b * lc, -1),
                              conv_w[l], cb, w_all, wos, wout, l,
                              tm=lc, tiles_per_group=b, tiles_per_seq=1)
        x2 = x2_new
    return x2.reshape(b, s, d)
```

```python
import functools
import math

import numpy as np
import jax
import jax.numpy as jnp
from jax import lax
from jax.experimental import pallas as pl
from jax.experimental.pallas import tpu as pltpu

D_MODEL = 1024
GRID_W = 64
HEAD_DIM = 64
EPS = 1e-6
NEG_INF = -1e30
A_HEADS = 8
A_KV_HEADS = 2
A_BLOCK = 128
A_WIDTH = A_HEADS * HEAD_DIM
A_KV_WIDTH = A_KV_HEADS * HEAD_DIM
ROPE_BASE = 10000.0
B_WIDTH = 256
R_HEADS = 4
R_WIDTH = R_HEADS * HEAD_DIM
R_STEP = 256
F_WIDTH = 256
N_BRANCH = 4

IN_NAMES = ("a_q", "a_k", "a_v", "a_z", "b_u", "b_b", "b_c", "b_z", "r_q", "r_k", "r_v", "r_z", "f_u", "f_z", "merge")
IN_SIZES = (A_WIDTH, A_KV_WIDTH, A_KV_WIDTH, A_WIDTH, B_WIDTH, B_WIDTH, B_WIDTH, B_WIDTH,
            R_WIDTH, R_WIDTH, R_WIDTH, R_WIDTH, F_WIDTH, F_WIDTH, N_BRANCH * D_MODEL)
_OFFS = dict(zip(IN_NAMES, np.cumsum((0,) + IN_SIZES)[:-1].tolist()))
_SIZE = dict(zip(IN_NAMES, IN_SIZES))

Z_NAMES = ("a_z", "b_z", "r_z", "f_z")
Z_SIZES = tuple(_SIZE[n] for n in Z_NAMES)

MERGE_CHUNK = 256
ATTN_SUB = 8
LOG2E = math.log2(math.e)
LANES = 128
VMEM_LIMIT = 56 * 1024 * 1024

BF16 = jnp.bfloat16
F32 = jnp.float32
HI = lax.Precision.HIGHEST


def _cparams(sem):
    return pltpu.CompilerParams(dimension_semantics=sem, vmem_limit_bytes=VMEM_LIMIT)


def _const_spec(shape):
    nd = len(shape)
    return pl.BlockSpec(shape, lambda *_: (0,) * nd)


def _sigmoid(v):
    return 1.0 / (1.0 + jnp.exp(-v))


def _ada_kernel(cv_ref, w_ref, b_ref, o_ref):
    cv = cv_ref[...]
    s = cv * _sigmoid(cv)
    o_ref[0] = jnp.dot(s, w_ref[0], preferred_element_type=F32, precision=HI) + b_ref[0]


def _ada_call(cv, w_ada, b_ada):
    depth, d, d3 = w_ada.shape
    tn = 1024
    return pl.pallas_call(
        _ada_kernel,
        grid=(depth, d3 // tn),
        in_specs=[pl.BlockSpec((8, d), lambda l, j: (0, 0)),
                  pl.BlockSpec((1, d, tn), lambda l, j: (l, 0, j)),
                  pl.BlockSpec((1, 1, tn), lambda l, j: (l, 0, j))],
        out_specs=pl.BlockSpec((1, 8, tn), lambda l, j: (l, 0, j)),
        out_shape=jax.ShapeDtypeStruct((depth, 8, d3), F32),
        compiler_params=_cparams(("arbitrary", "arbitrary")),
        name="ada_mod",
    )(cv, w_ada, b_ada.reshape(depth, 1, d3))


def _modulated_norm(x, g, mod):
    ms = jnp.mean(x * x, axis=-1, keepdims=True)
    y = x * lax.rsqrt(ms + EPS) * g
    return y * (1.0 + mod[:, D_MODEL:2 * D_MODEL]) + mod[:, 0:D_MODEL]


def _rope(t, cos, sin_signed, first_half):
    outs = []
    for j in range(t.shape[1] // LANES):
        tj = t[:, j * LANES:(j + 1) * LANES]
        partner = jnp.where(first_half, pltpu.roll(tj, LANES - 16, 1), pltpu.roll(tj, 16, 1))
        outs.append(tj * cos + partner * sin_signed)
    return outs[0] if len(outs) == 1 else jnp.concatenate(outs, axis=1)


MIX_GROUPS = (("a_q", "a_v"), ("b_u", "b_c"), ("r_q", "r_v"), ("f_u", "f_u"))


def _inproj_kernel(x_ref, mod_ref, g_ref, wa_ref, wb_ref, wr_ref, wf_ref, cos_ref, sin_ref, fd_ref,
                   qa_ref, ka_ref, va_ref, wc_ref, bb_ref, qr_ref, kr_ref, vr_ref, zf_ref):
    h = _modulated_norm(x_ref[...], g_ref[...], mod_ref[0]).astype(BF16)
    w_refs = {"a": wa_ref, "b": wb_ref, "r": wr_ref, "f": wf_ref}
    starts = {first[0]: _OFFS[first] for first, _ in MIX_GROUPS}

    def proj(name):
        o = _OFFS[name] - starts[name[0]]
        return jnp.dot(h, w_refs[name[0]][0, :, o:o + _SIZE[name]], preferred_element_type=F32)

    cos = cos_ref[...]
    sin = sin_ref[...]
    lane = lax.broadcasted_iota(jnp.int32, cos.shape, 1)
    first_half = (lane % 32) < 16
    low_half = lane < HEAD_DIM
    k_scale = HEAD_DIM ** -0.5
    q = _rope(proj("a_q"), cos, sin, first_half) * (k_scale * LOG2E)
    for p in range(A_WIDTH // LANES):
        qp = q[:, p * LANES:(p + 1) * LANES]
        qa_ref[:, (2 * p) * LANES:(2 * p + 1) * LANES] = jnp.where(low_half, qp, 0.0).astype(BF16)
        qa_ref[:, (2 * p + 1) * LANES:(2 * p + 2) * LANES] = jnp.where(low_half, 0.0, qp).astype(BF16)
    k = _rope(proj("a_k"), cos, sin, first_half)
    k_sw = pltpu.roll(k, HEAD_DIM, 1)
    ka_ref[:, 0:LANES] = jnp.where(low_half, k, k_sw).astype(BF16)
    ka_ref[:, LANES:2 * LANES] = jnp.where(low_half, k_sw, k).astype(BF16)
    v = proj("a_v")
    v_sw = pltpu.roll(v, HEAD_DIM, 1)
    va_ref[:, 0:LANES] = jnp.where(low_half, v, 1.0).astype(BF16)
    va_ref[:, LANES:2 * LANES] = jnp.where(low_half, 1.0, v_sw).astype(BF16)
    va_ref[:, 2 * LANES:3 * LANES] = jnp.where(low_half, v_sw, 1.0).astype(BF16)
    va_ref[:, 3 * LANES:4 * LANES] = jnp.where(low_half, 1.0, v).astype(BF16)
    wc_ref[...] = proj("b_c") * proj("b_u")
    bb_ref[...] = proj("b_b")
    qr_ref[...] = _rope(proj("r_q"), cos, sin, first_half)
    kr_ref[...] = _rope(proj("r_k"), cos, sin, first_half) * k_scale
    vr_ref[...] = proj("r_v")
    zf_ref[...] = jnp.dot(proj("f_u").astype(BF16), fd_ref[...].astype(BF16), preferred_element_type=F32)


def _layer_spec(arr, layer):
    return pl.BlockSpec((1,) + arr.shape[1:], lambda *_: (layer,) + (0,) * (arr.ndim - 1),
                        pipeline_mode=pl.Buffered(1))


def _cols_spec(w_all, layer, first, last):
    lo = _OFFS[first]
    width = _OFFS[last] + _SIZE[last] - lo
    return pl.BlockSpec((pl.Element(1), pl.Element(w_all.shape[1]), pl.Element(width)),
                        lambda *_: (layer, 0, lo), pipeline_mode=pl.Buffered(1))


def _inproj_call(x2, mod3, g_pre, w_all, layer, cos_t, sin_t, fd, *, tm, tiles_per_group):
    rows, d = x2.shape
    nt = rows // tm
    tiles_per_seq = cos_t.shape[0] // tm
    widths = (2 * A_WIDTH, 2 * A_KV_WIDTH, 4 * A_KV_WIDTH, B_WIDTH, B_WIDTH, R_WIDTH, R_WIDTH, R_WIDTH, 2 * F_WIDTH)
    dtypes = (BF16, BF16, BF16, F32, F32, F32, F32, F32, F32)
    row_spec = lambda w: pl.BlockSpec((tm, w), lambda i: (i, 0))
    return pl.pallas_call(
        _inproj_kernel,
        grid=(nt,),
        in_specs=[row_spec(d),
                  pl.BlockSpec((1, 1, 3 * d), lambda i: (i // tiles_per_group, 0, 0)),
                  _const_spec((1, d)),
                  *[_cols_spec(w_all, layer, first, last) for first, last in MIX_GROUPS],
                  pl.BlockSpec((tm, LANES), lambda i: (i % tiles_per_seq, 0)),
                  pl.BlockSpec((tm, LANES), lambda i: (i % tiles_per_seq, 0)),
                  _const_spec(fd.shape)],
        out_specs=[row_spec(w) for w in widths],
        out_shape=[jax.ShapeDtypeStruct((rows, w), dt) for w, dt in zip(widths, dtypes)],
        compiler_params=_cparams(("arbitrary",)),
        name="in_proj",
    )(x2, mod3, g_pre, w_all, w_all, w_all, w_all, cos_t, sin_t, fd)


def _attn_kernel(*refs, local):
    if local:
        (sink_ref, q_ref, kp_ref, kc_ref, kn_ref, vp_ref, vc_ref, vn_ref, kx_ref, vx_ref, o_ref,
         kwin, vwin) = refs
    else:
        sink_ref, q_ref, kx_ref, vx_ref, o_ref = refs
    blk = A_BLOCK
    nsub = q_ref.shape[1] // blk
    n = pl.program_id(1)
    last = pl.num_programs(1) * nsub - 1
    contract_last = (((1,), (1,)), ((), ()))

    lane_q = lax.broadcasted_iota(jnp.int32, (blk, LANES), 1)
    nloc = 3 * blk
    if local:
        step = nsub * blk
        kwin[0:blk] = kp_ref[0]
        kwin[blk:blk + step] = kc_ref[0]
        kwin[blk + step:2 * blk + step] = kn_ref[0]
        vwin[0:blk] = vp_ref[0]
        vwin[blk:blk + step] = vc_ref[0]
        vwin[blk + step:2 * blk + step] = vn_ref[0]
        row = lax.broadcasted_iota(jnp.int32, (blk, nloc), 0)
        col = lax.broadcasted_iota(jnp.int32, (blk, nloc), 1)
        band = jnp.minimum(col - row, row + 2 * blk - col)

    for sub in range(nsub):
        q0 = sub * blk
        if local:
            g_blk = n * nsub + sub
            seq_lo = blk - g_blk * blk
            seq_hi = blk + (last - g_blk + 1) * blk
            valid = (jnp.minimum(band, jnp.minimum(col - seq_lo, seq_hi - 1 - col)) >= 0)[None]
        for j in range(A_KV_HEADS):
            kcol = slice(j * LANES, (j + 1) * LANES)
            ecol = slice(2 * j * LANES, (2 * j + 1) * LANES)
            ocol = slice((2 * j + 1) * LANES, (2 * j + 2) * LANES)
            qs = jnp.concatenate([q_ref[0, q0:q0 + blk, (4 * j + g) * LANES:(4 * j + g + 1) * LANES]
                                  for g in range(4)], axis=0)
            s = lax.dot_general(qs, kx_ref[0, :, kcol], contract_last, preferred_element_type=F32)
            if local:
                s_loc = lax.dot_general(qs, kwin[q0:q0 + nloc, kcol], contract_last, preferred_element_type=F32)
                s_loc = jnp.where(valid, s_loc.reshape(4, blk, nloc), NEG_INF).reshape(4 * blk, nloc)
                s = jnp.concatenate([s_loc, s], axis=1)
            sink = jnp.concatenate(
                [jnp.full((blk, 1), sink_ref[4 * j + g] * LOG2E, F32) for g in range(4)], axis=0)
            m = jnp.maximum(jnp.max(s, axis=1, keepdims=True), sink)
            pb = jnp.exp2((s - m).astype(BF16))
            e_sink = jnp.exp2(sink - m)

            vcol = slice(2 * j * LANES, (2 * j + 2) * LANES)
            if local:
                o_all = (jnp.dot(pb[:, :nloc], vwin[q0:q0 + nloc, vcol], preferred_element_type=F32)
                         + jnp.dot(pb[:, nloc:], vx_ref[0, :, vcol], preferred_element_type=F32))
            else:
                o_all = jnp.dot(pb, vx_ref[0, :, vcol], preferred_element_type=F32)

            for pair in range(2):
                r0 = 2 * pair * blk
                o_even = o_all[r0:r0 + blk, :LANES]
                o_odd = o_all[r0 + blk:r0 + 2 * blk, LANES:]
                num = jnp.where(lane_q < 64, o_even, o_odd)
                den = (pltpu.roll(jnp.where(lane_q < 64, o_odd, o_even), 64, 1)
                       + jnp.where(lane_q < 64, e_sink[r0:r0 + blk], e_sink[r0 + blk:r0 + 2 * blk]))
                c0 = (2 * j + pair) * LANES
                o_ref[0, q0:q0 + blk, c0:c0 + LANES] = num / den


def _attn_call(sink, q, k, v, kx, vx, *, local):
    b, sq, _ = q.shape
    blk = A_BLOCK
    nq = sq // blk
    nsub = min(ATTN_SUB, nq)
    step = nsub * blk
    lx = kx.shape[1]
    smem = pl.BlockSpec(memory_space=pltpu.SMEM)
    kw, vw = kx.shape[2], vx.shape[2]
    q_spec = pl.BlockSpec((1, step, q.shape[2]), lambda bi, n: (bi, n, 0))
    if local:
        def halo(w):
            return (pl.BlockSpec((1, blk, w), lambda bi, n: (bi, jnp.maximum(n * nsub - 1, 0), 0)),
                    pl.BlockSpec((1, step, w), lambda bi, n: (bi, n, 0)),
                    pl.BlockSpec((1, blk, w), lambda bi, n: (bi, jnp.minimum((n + 1) * nsub, nq - 1), 0)))
        in_specs = [smem, q_spec, *halo(kw), *halo(vw),
                    pl.BlockSpec((1, lx, kw), lambda bi, n: (bi, 0, 0)),
                    pl.BlockSpec((1, lx, vw), lambda bi, n: (bi, 0, 0))]
        args = (sink, q, k, k, k, v, v, v, kx, vx)
        scratch = [pltpu.VMEM((step + 2 * blk, kw), k.dtype), pltpu.VMEM((step + 2 * blk, vw), v.dtype)]
    else:
        in_specs = [smem, q_spec, pl.BlockSpec((1, lx, kw), lambda bi, n: (bi, 0, 0)),
                    pl.BlockSpec((1, lx, vw), lambda bi, n: (bi, 0, 0))]
        args = (sink, q, kx, vx)
        scratch = []
    return pl.pallas_call(
        functools.partial(_attn_kernel, local=local),
        grid=(b, nq // nsub),
        in_specs=in_specs,
        out_specs=pl.BlockSpec((1, step, A_WIDTH), lambda bi, n: (bi, n, 0)),
        out_shape=jax.ShapeDtypeStruct((b, sq, A_WIDTH), F32),
        scratch_shapes=scratch,
        compiler_params=_cparams(("arbitrary", "arbitrary")),
        name="win_attn" if local else "ctx_attn",
    )(*args)


def _head_of(shape, dim):
    return lax.broadcasted_iota(jnp.int32, shape, dim) // HEAD_DIM


def _group_mean(t, avg):
    hi = t.astype(BF16)
    lo = (t - hi.astype(F32)).astype(BF16)
    return (jnp.dot(hi, avg, preferred_element_type=F32) + jnp.dot(lo, avg, preferred_element_type=F32))


def _ret_kernel(*refs, reverse, finalize):
    if finalize:
        lg_ref, lgl_ref, q_ref, k_ref, v_ref, r0_ref, yin_ref, y_ref, rfin_ref, r_scr = refs
    else:
        lg_ref, lgl_ref, q_ref, k_ref, v_ref, r0_ref, y_ref, rfin_ref, r_scr = refs
    c = pl.program_id(0)
    nb, ch, w = q_ref.shape

    @pl.when(c == 0)
    def _():
        r_scr[...] = r0_ref[...]

    lgl = lgl_ref[...]
    pos = lax.broadcasted_iota(jnp.int32, (ch, 1), 0).astype(F32)
    if reverse:
        xi = jnp.exp(lgl * (ch - pos))
        zeta = jnp.exp(lgl * pos)
    else:
        xi = jnp.exp(lgl * (pos + 1.0))
        zeta = jnp.exp(lgl * (ch - 1.0 - pos))
    g_chunk = jnp.exp(lgl * float(ch))
    ri = lax.broadcasted_iota(jnp.int32, (ch, ch), 0)
    ci = lax.broadcasted_iota(jnp.int32, (ch, ch), 1)
    diff = (ci - ri) if reverse else (ri - ci)
    dist = jnp.maximum(diff, 0).astype(F32)
    decays = [jnp.where(diff >= 0, jnp.exp(lg_ref[h] * dist), 0.0) for h in range(R_HEADS)]
    lane_head = _head_of((ch, w), 1)
    same_head = _head_of((w, w), 0) == _head_of((w, w), 1)
    avg = jnp.where(same_head, 1.0 / HEAD_DIM, 0.0).astype(BF16)

    for bi in range(nb):
        q = q_ref[bi]
        k = k_ref[bi]
        v = v_ref[bi]
        r = r_scr[bi]
        cross = jnp.dot((q * xi).astype(BF16), r.astype(BF16), preferred_element_type=F32)
        q4 = jnp.concatenate([jnp.where(lane_head == h, q, 0.0) for h in range(R_HEADS)], axis=0).astype(BF16)
        sc = lax.dot_general(q4, k.astype(BF16), (((1,), (1,)), ((), ())), preferred_element_type=F32)
        s4 = jnp.concatenate([(sc[h * ch:(h + 1) * ch] * decays[h]).astype(BF16) for h in range(R_HEADS)], axis=1)
        v4 = jnp.concatenate([jnp.where(lane_head == h, v, 0.0) for h in range(R_HEADS)], axis=0).astype(BF16)
        y = jnp.dot(s4, v4, preferred_element_type=F32) + cross

        kz = (k * zeta).astype(BF16)
        ktv = lax.dot_general(kz, v.astype(BF16), (((0,), (0,)), ((), ())), preferred_element_type=F32)
        r_new = g_chunk * r + jnp.where(same_head, ktv, 0.0)
        r_scr[bi] = r_new
        rfin_ref[bi] = r_new

        if finalize:
            y = y + yin_ref[bi]
            mu = _group_mean(y, avg)
            d = y - mu
            var = _group_mean(d * d, avg)
            y = d * lax.rsqrt(var + EPS)
        y_ref[bi] = y


def _ret_call(lg, lgl, q, k, v, r0, y_in, *, reverse):
    b, s, w = q.shape
    ch = min(R_STEP, s)
    nc = s // ch
    finalize = y_in is not None
    cidx = (lambda c: (0, nc - 1 - c, 0)) if reverse else (lambda c: (0, c, 0))
    chunk = pl.BlockSpec((b, ch, w), cidx)
    state = pl.BlockSpec((b, w, w), lambda c: (0, 0, 0))
    in_specs = [pl.BlockSpec(memory_space=pltpu.SMEM), pl.BlockSpec((1, w), lambda c: (0, 0)),
                chunk, chunk, chunk, state]
    args = [lg, lgl, q, k, v, r0]
    if finalize:
        in_specs.append(chunk)
        args.append(y_in)
    return pl.pallas_call(
        functools.partial(_ret_kernel, reverse=reverse, finalize=finalize),
        grid=(nc,),
        in_specs=in_specs,
        out_specs=[chunk, state],
        out_shape=[jax.ShapeDtypeStruct((b, s, w), F32), jax.ShapeDtypeStruct((b, w, w), F32)],
        scratch_shapes=[pltpu.VMEM((b, w, w), F32)],
        compiler_params=_cparams(("arbitrary",)),
        name="retention_bwd" if reverse else "retention_fwd",
    )(*args)


def _dft_cs(n, scale):
    a = 2.0 * np.pi * np.outer(np.arange(n), np.arange(n)) / n
    return np.cos(a) * scale, np.sin(a) * scale


def _fourier_rows_kernel(m_ref, tc_ref, ts_ref, z_ref, o_ref):
    rows, cb = z_ref.shape[1], z_ref.shape[2]
    m = m_ref[...].astype(BF16)
    for ci in range(cb):
        z = z_ref[0, :, ci, :]
        zz = jnp.concatenate([z[:, :F_WIDTH], z[:, F_WIDTH:]], axis=0).astype(BF16)
        a = jnp.dot(m, zz, preferred_element_type=F32)
        a_re, a_im = a[:rows], a[rows:]
        tc = jnp.concatenate([tc_ref[ci]] * (F_WIDTH // LANES), axis=1)
        ts = jnp.concatenate([ts_ref[ci]] * (F_WIDTH // LANES), axis=1)
        o_ref[0, :, 0, ci, :] = (a_re * tc + a_im * ts).astype(o_ref.dtype)
        o_ref[0, :, 1, ci, :] = (a_im * tc - a_re * ts).astype(o_ref.dtype)


def _fourier_cols_kernel(g_ref, b_ref, o_ref):
    g = g_ref[...].astype(BF16)
    kt, _, cw, f = b_ref.shape[1:]
    for i in range(kt):
        bm = b_ref[0, i].reshape(2 * cw, f).astype(BF16)
        o_ref[0, :, i, :] = jnp.dot(g, bm, preferred_element_type=F32)


def _fourier_latent(zf, rows):
    b, s, _ = zf.shape
    cw = GRID_W
    c_r, s_r = _dft_cs(rows, rows ** -0.5)
    m1 = jnp.asarray(np.block([[c_r, s_r], [-s_r, c_r]]), F32)
    ang = 2.0 * np.pi * np.outer(np.arange(cw), np.arange(rows)) / s
    tc = jnp.asarray(np.repeat(np.cos(ang)[:, :, None], LANES, axis=2), F32)
    ts = jnp.asarray(np.repeat(np.sin(ang)[:, :, None], LANES, axis=2), F32)
    cb = 32
    bk = pl.pallas_call(
        _fourier_rows_kernel,
        grid=(cw // cb, b),
        in_specs=[_const_spec(m1.shape),
                  pl.BlockSpec((cb, rows, LANES), lambda j, bi: (j, 0, 0)),
                  pl.BlockSpec((cb, rows, LANES), lambda j, bi: (j, 0, 0)),
                  pl.BlockSpec((1, rows, cb, 2 * F_WIDTH), lambda j, bi: (bi, 0, j, 0))],
        out_specs=pl.BlockSpec((1, rows, 2, cb, F_WIDTH), lambda j, bi: (bi, 0, 0, j, 0)),
        out_shape=jax.ShapeDtypeStruct((b, rows, 2, cw, F_WIDTH), F32),
        compiler_params=_cparams(("arbitrary", "arbitrary")),
        name="fourier_rows",
    )(m1, tc, ts, zf.reshape(b, rows, cw, 2 * F_WIDTH))
    c_c, s_c = _dft_cs(cw, cw ** -0.5)
    g = jnp.asarray(np.concatenate([c_c, s_c], axis=1), F32)
    kt = 32
    out = pl.pallas_call(
        _fourier_cols_kernel,
        grid=(b, rows // kt),
        in_specs=[_const_spec(g.shape),
                  pl.BlockSpec((1, kt, 2, cw, F_WIDTH), lambda bi, i: (bi, i, 0, 0, 0))],
        out_specs=pl.BlockSpec((1, cw, kt, F_WIDTH), lambda bi, i: (bi, 0, i, 0)),
        out_shape=jax.ShapeDtypeStruct((b, cw, rows, F_WIDTH), F32),
        compiler_params=_cparams(("arbitrary", "arbitrary")),
        name="fourier_cols",
    )(g, bk)
    return out.reshape(b, s, F_WIDTH)


def _fourier_dense_kernel(m_ref, z_ref, o_ref):
    z = z_ref[0].astype(F32)
    zz = jnp.concatenate([z[:, :F_WIDTH], z[:, F_WIDTH:]], axis=0)
    o_ref[0] = jnp.dot(m_ref[...], zz, preferred_element_type=F32, precision=HI)


def _fourier_dense(zf):
    b, n, _ = zf.shape
    c_n, s_n = _dft_cs(n, n ** -0.5)
    m = jnp.asarray(np.concatenate([c_n, s_n], axis=1), F32)
    return pl.pallas_call(
        _fourier_dense_kernel,
        grid=(b,),
        in_specs=[_const_spec(m.shape), pl.BlockSpec((1, n, 2 * F_WIDTH), lambda bi: (bi, 0, 0))],
        out_specs=pl.BlockSpec((1, n, F_WIDTH), lambda bi: (bi, 0, 0)),
        out_shape=jax.ShapeDtypeStruct((b, n, F_WIDTH), F32),
        compiler_params=_cparams(("arbitrary",)),
        name="fourier_dense",
    )(m, zf)


def _merge_kernel(x_ref, mod_ref, gpre_ref, gpost_ref, ya_ref, wc_ref, wprev_ref, wnext_ref, bb_ref,
                  yr_ref, yf_ref, cw_ref, cb_ref, wza_ref, wzb_ref, wzr_ref, wfm_ref,
                  woa_ref, wob_ref, wor_ref, wof_ref, wout_ref, o_ref, *, tiles_per_seq):
    i = pl.program_id(0)
    x = x_ref[...]
    mod = mod_ref[0]
    h = _modulated_norm(x, gpre_ref[...], mod).astype(BF16)
    tm = x.shape[0]

    wc = wc_ref[...]
    t = i % tiles_per_seq
    prev_row = jnp.where(t > 0, wprev_ref[7:8, :], 0.0)
    next_row = jnp.where(t < tiles_per_seq - 1, wnext_ref[0:1, :], 0.0)
    row = lax.broadcasted_iota(jnp.int32, wc.shape, 0)
    up = jnp.where(row == 0, prev_row, pltpu.roll(wc, 1, 0))
    dn = jnp.where(row == tm - 1, next_row, pltpu.roll(wc, tm - 1, 0))
    conv = up * cw_ref[0:1, :] + wc * cw_ref[1:2, :] + dn * cw_ref[2:3, :] + cb_ref[...]
    yb = bb_ref[...] * conv

    ys = (ya_ref[...], yb, yr_ref[...], yf_ref[...])
    wo_refs = (woa_ref, wob_ref, wor_ref, wof_ref)
    fz = _SIZE["f_z"]
    wz = (wza_ref[0], wzb_ref[0], wzr_ref[0], wfm_ref[0, :, 0:fz])
    acts = []
    for br in range(N_BRANCH):
        z = jnp.dot(h, wz[br], preferred_element_type=F32)
        acts.append((ys[br] * (z * _sigmoid(z))).astype(BF16))
    y = jnp.zeros((tm, D_MODEL), F32)
    for c0 in range(0, D_MODEL, MERGE_CHUNK):
        total = jnp.zeros((tm, MERGE_CHUNK), F32)
        for br in range(N_BRANCH):
            proj = jnp.dot(acts[br], wo_refs[br][0, :, c0:c0 + MERGE_CHUNK], preferred_element_type=F32)
            g0 = fz + br * D_MODEL + c0
            gate = _sigmoid(jnp.dot(h, wfm_ref[0, :, g0:g0 + MERGE_CHUNK], preferred_element_type=F32))
            total = total + gate * proj
        y = y + jnp.dot(total.astype(BF16), wout_ref[0, c0:c0 + MERGE_CHUNK, :], preferred_element_type=F32)
    ms = jnp.mean(y * y, axis=-1, keepdims=True)
    yn = y * lax.rsqrt(ms + EPS) * gpost_ref[...]
    o_ref[...] = x + mod[:, 2 * D_MODEL:3 * D_MODEL] * yn


def _merge_call(x2, mod3, g_pre, g_post, ya, wc, bb, yr, yf, conv_w, conv_b, w_all, wos, wout, layer,
                *, tm, tiles_per_group, tiles_per_seq):
    rows, d = x2.shape
    nt = rows // tm
    hb = tm // 8
    nhb = rows // 8
    row_spec = lambda w: pl.BlockSpec((tm, w), lambda i: (i, 0))
    cols = functools.partial(_cols_spec, w_all, layer)
    return pl.pallas_call(
        functools.partial(_merge_kernel, tiles_per_seq=tiles_per_seq),
        grid=(nt,),
        in_specs=[row_spec(d),
                  pl.BlockSpec((1, 1, 3 * d), lambda i: (i // tiles_per_group, 0, 0)),
                  _const_spec((1, d)), _const_spec((1, d)),
                  row_spec(A_WIDTH),
                  row_spec(B_WIDTH),
                  pl.BlockSpec((8, B_WIDTH), lambda i: (jnp.maximum(i * hb - 1, 0), 0)),
                  pl.BlockSpec((8, B_WIDTH), lambda i: (jnp.minimum((i + 1) * hb, nhb - 1), 0)),
                  row_spec(B_WIDTH), row_spec(R_WIDTH), row_spec(F_WIDTH),
                  _const_spec(conv_w.shape), _const_spec(conv_b.shape),
                  cols("a_z", "a_z"), cols("b_z", "b_z"), cols("r_z", "r_z"), cols("f_z", "merge"),
                  *[_layer_spec(w, layer) for w in wos], _layer_spec(wout, layer)],
        out_specs=row_spec(d),
        out_shape=jax.ShapeDtypeStruct((rows, d), F32),
        compiler_params=_cparams(("arbitrary",)),
        name="merge",
    )(x2, mod3, g_pre, g_post, ya, wc, wc, wc, bb, yr, yf, conv_w, conv_b, w_all, w_all, w_all, w_all, *wos, wout)


def _rope_tables(n):
    rows = n // GRID_W
    row = jnp.broadcast_to(jnp.arange(rows)[:, None], (rows, GRID_W)).reshape(-1).astype(F32)
    col = jnp.broadcast_to(jnp.arange(GRID_W)[None, :], (rows, GRID_W)).reshape(-1).astype(F32)
    half = HEAD_DIM // 2
    inv = ROPE_BASE ** (-jnp.arange(0, half, 2, dtype=F32) / half)
    ang_r = row[:, None] * inv
    ang_c = col[:, None] * inv
    cos = jnp.concatenate([jnp.cos(ang_r), jnp.cos(ang_r), jnp.cos(ang_c), jnp.cos(ang_c)], axis=1)
    sin = jnp.concatenate([-jnp.sin(ang_r), jnp.sin(ang_r), -jnp.sin(ang_c), jnp.sin(ang_c)], axis=1)
    return jnp.tile(cos, (1, LANES // HEAD_DIM)), jnp.tile(sin, (1, LANES // HEAD_DIM))


def kernel(x, c, ctx, c_ctx, w_ada, b_ada, norm_pre, norm_post, w_in, attn_sink, conv_w, conv_b, ret_decay,
           w_o_attn, w_o_conv, w_o_ret, w_o_fourier, w_out):
    b, s, d = x.shape
    lc = ctx.shape[1]
    depth = w_in.shape[0]
    rows_grid = s // GRID_W

    cv = jnp.zeros((8, d), F32).at[:b].set(c).at[b].set(c_ctx)
    mods = _ada_call(cv, w_ada, b_ada)

    cos_x, sin_x = _rope_tables(s)
    cos_c = jnp.ones((lc, LANES), F32)
    sin_c = jnp.zeros((lc, LANES), F32)

    c64, s64 = _dft_cs(HEAD_DIM, HEAD_DIM ** -0.5)
    eye = np.eye(F_WIDTH // HEAD_DIM)
    fd = jnp.asarray(np.concatenate([np.kron(eye, c64), -np.kron(eye, s64)], axis=1), F32)

    lg_all = jax.nn.log_sigmoid(ret_decay.astype(F32))
    zero_state = jnp.zeros((b, R_WIDTH, R_WIDTH), F32)

    w_all = w_in.astype(BF16)
    wos = tuple(w.astype(BF16) for w in (w_o_attn, w_o_conv, w_o_ret, w_o_fourier))
    wout = w_out.astype(BF16)

    tm_x = 1024
    tm_m = 1024
    x2 = x.reshape(b * s, d)
    xc2 = ctx.reshape(b * lc, d)
    for l in range(depth):
        update_ctx = l < depth - 1
        g_pre = norm_pre[l].reshape(1, d)
        g_post = norm_post[l].reshape(1, d)
        mod_x = mods[l, :b].reshape(b, 1, 3 * d)
        mod_c = mods[l, b:b + 1].reshape(1, 1, 3 * d)
        cb = conv_b[l].reshape(1, B_WIDTH)
        lg = lg_all[l]
        lgl = jnp.repeat(lg, HEAD_DIM, axis=1)

        px = _inproj_call(x2, mod_x, g_pre, w_all, l, cos_x, sin_x, fd, tm=tm_x, tiles_per_group=s // tm_x)
        pc = _inproj_call(xc2, mod_c, g_pre, w_all, l, cos_c, sin_c, fd, tm=lc, tiles_per_group=b)
        qa, ka, va, wcx, bbx, qr, kr, vr, zf = [t.reshape(b, s, -1) for t in px]
        qac, kac, vac, wcc, bbc, qrc, krc, vrc, zfc = [t.reshape(b, lc, -1) for t in pc]

        ya = _attn_call(attn_sink[l], qa, ka, va, kac, vac, local=True)

        ycf, st_f = _ret_call(lg[0], lgl[0:1], qrc, krc, vrc, zero_state, None, reverse=False)
        ycr, st_b = _ret_call(lg[1], lgl[1:2], qrc, krc, vrc, zero_state, ycf, reverse=True)
        yf_, _ = _ret_call(lg[0], lgl[0:1], qr, kr, vr, st_f, None, reverse=False)
        yr, _ = _ret_call(lg[1], lgl[1:2], qr, kr, vr, st_b, yf_, reverse=True)

        yfo = _fourier_latent(zf, rows_grid)

        x2_new = _merge_call(x2, mod_x, g_pre, g_post, ya.reshape(b * s, -1), wcx.reshape(b * s, -1),
                             bbx.reshape(b * s, -1), yr.reshape(b * s, -1), yfo.reshape(b * s, -1),
                             conv_w[l], cb, w_all, wos, wout, l,
                             tm=tm_m, tiles_per_group=s // tm_m, tiles_per_seq=s // tm_m)
        if update_ctx:
            yac = _attn_call(attn_sink[l], qac, None, None, kac, vac, local=False)
            yfc = _fourier_dense(zfc)
            xc2 = _merge_call(xc2, mod_c, g_pre, g_post, yac.reshape(b * lc, -1), wcc.reshape(b * lc, -1),
                              bbc.reshape(b * lc, -1), ycr.reshape(b * lc, -1), yfc.reshape(b * lc, -1),
                              conv_w[l], cb, w_all, wos, wout, l,
                              tm=lc, tiles_per_group=b, tiles_per_seq=1)
        x2 = x2_new
    return x2.reshape(b, s, d)
```

```python
import functools
import math

import numpy as np
import jax
import jax.numpy as jnp
from jax import lax
from jax.experimental import pallas as pl
from jax.experimental.pallas import tpu as pltpu

D_MODEL = 1024
GRID_W = 64
HEAD_DIM = 64
EPS = 1e-6
NEG_INF = -1e30
A_HEADS = 8
A_KV_HEADS = 2
A_BLOCK = 128
A_WIDTH = A_HEADS * HEAD_DIM
A_KV_WIDTH = A_KV_HEADS * HEAD_DIM
ROPE_BASE = 10000.0
B_WIDTH = 256
R_HEADS = 4
R_WIDTH = R_HEADS * HEAD_DIM
R_STEP = 256
F_WIDTH = 256
N_BRANCH = 4

IN_NAMES = ("a_q", "a_k", "a_v", "a_z", "b_u", "b_b", "b_c", "b_z", "r_q", "r_k", "r_v", "r_z", "f_u", "f_z", "merge")
IN_SIZES = (A_WIDTH, A_KV_WIDTH, A_KV_WIDTH, A_WIDTH, B_WIDTH, B_WIDTH, B_WIDTH, B_WIDTH,
            R_WIDTH, R_WIDTH, R_WIDTH, R_WIDTH, F_WIDTH, F_WIDTH, N_BRANCH * D_MODEL)
_OFFS = dict(zip(IN_NAMES, np.cumsum((0,) + IN_SIZES)[:-1].tolist()))
_SIZE = dict(zip(IN_NAMES, IN_SIZES))

Z_NAMES = ("a_z", "b_z", "r_z", "f_z")
Z_SIZES = tuple(_SIZE[n] for n in Z_NAMES)

MERGE_CHUNK = 256
ATTN_SUB = 8
LOG2E = math.log2(math.e)
LANES = 128
VMEM_LIMIT = 56 * 1024 * 1024

BF16 = jnp.bfloat16
F32 = jnp.float32
HI = lax.Precision.HIGHEST


def _cparams(sem):
    return pltpu.CompilerParams(dimension_semantics=sem, vmem_limit_bytes=VMEM_LIMIT)


def _const_spec(shape):
    nd = len(shape)
    return pl.BlockSpec(shape, lambda *_: (0,) * nd)


def _sigmoid(v):
    return 1.0 / (1.0 + jnp.exp(-v))


def _ada_kernel(cv_ref, w_ref, b_ref, o_ref):
    cv = cv_ref[...]
    s = cv * _sigmoid(cv)
    o_ref[0] = jnp.dot(s, w_ref[0], preferred_element_type=F32, precision=HI) + b_ref[0]


def _ada_call(cv, w_ada, b_ada):
    depth, d, d3 = w_ada.shape
    tn = 1024
    return pl.pallas_call(
        _ada_kernel,
        grid=(depth, d3 // tn),
        in_specs=[pl.BlockSpec((8, d), lambda l, j: (0, 0)),
                  pl.BlockSpec((1, d, tn), lambda l, j: (l, 0, j)),
                  pl.BlockSpec((1, 1, tn), lambda l, j: (l, 0, j))],
        out_specs=pl.BlockSpec((1, 8, tn), lambda l, j: (l, 0, j)),
        out_shape=jax.ShapeDtypeStruct((depth, 8, d3), F32),
        compiler_params=_cparams(("arbitrary", "arbitrary")),
        name="ada_mod",
    )(cv, w_ada, b_ada.reshape(depth, 1, d3))


def _modulated_norm(x, g, mod):
    ms = jnp.mean(x * x, axis=-1, keepdims=True)
    y = x * lax.rsqrt(ms + EPS) * g
    return y * (1.0 + mod[:, D_MODEL:2 * D_MODEL]) + mod[:, 0:D_MODEL]


def _rope(t, cos, sin_signed, first_half):
    outs = []
    for j in range(t.shape[1] // LANES):
        tj = t[:, j * LANES:(j + 1) * LANES]
        partner = jnp.where(first_half, pltpu.roll(tj, LANES - 16, 1), pltpu.roll(tj, 16, 1))
        outs.append(tj * cos + partner * sin_signed)
    return outs[0] if len(outs) == 1 else jnp.concatenate(outs, axis=1)


MIX_GROUPS = (("a_q", "a_v"), ("b_u", "b_c"), ("r_q", "r_v"), ("f_u", "f_u"))


def _inproj_kernel(x_ref, mod_ref, g_ref, wa_ref, wb_ref, wr_ref, wf_ref, cos_ref, sin_ref, fd_ref,
                   qa_ref, ka_ref, va_ref, wc_ref, bb_ref, qr_ref, kr_ref, vr_ref, zf_ref):
    h = _modulated_norm(x_ref[...], g_ref[...], mod_ref[0]).astype(BF16)
    w_refs = {"a": wa_ref, "b": wb_ref, "r": wr_ref, "f": wf_ref}
    starts = {first[0]: _OFFS[first] for first, _ in MIX_GROUPS}

    def proj(name):
        o = _OFFS[name] - starts[name[0]]
        return jnp.dot(h, w_refs[name[0]][0, :, o:o + _SIZE[name]], preferred_element_type=F32)

    cos = cos_ref[...]
    sin = sin_ref[...]
    lane = lax.broadcasted_iota(jnp.int32, cos.shape, 1)
    first_half = (lane % 32) < 16
    low_half = lane < HEAD_DIM
    k_scale = HEAD_DIM ** -0.5
    q = _rope(proj("a_q"), cos, sin, first_half) * (k_scale * LOG2E)
    for p in range(A_WIDTH // LANES):
        qp = q[:, p * LANES:(p + 1) * LANES]
        qa_ref[:, (2 * p) * LANES:(2 * p + 1) * LANES] = jnp.where(low_half, qp, 0.0).astype(BF16)
        qa_ref[:, (2 * p + 1) * LANES:(2 * p + 2) * LANES] = jnp.where(low_half, 0.0, qp).astype(BF16)
    k = _rope(proj("a_k"), cos, sin, first_half)
    k_sw = pltpu.roll(k, HEAD_DIM, 1)
    ka_ref[:, 0:LANES] = jnp.where(low_half, k, k_sw).astype(BF16)
    ka_ref[:, LANES:2 * LANES] = jnp.where(low_half, k_sw, k).astype(BF16)
    v = proj("a_v")
    v_sw = pltpu.roll(v, HEAD_DIM, 1)
    va_ref[:, 0:LANES] = jnp.where(low_half, v, 1.0).astype(BF16)
    va_ref[:, LANES:2 * LANES] = jnp.where(low_half, 1.0, v_sw).astype(BF16)
    va_ref[:, 2 * LANES:3 * LANES] = jnp.where(low_half, v_sw, 1.0).astype(BF16)
    va_ref[:, 3 * LANES:4 * LANES] = jnp.where(low_half, 1.0, v).astype(BF16)
    wc_ref[...] = proj("b_c") * proj("b_u")
    bb_ref[...] = proj("b_b")
    qr_ref[...] = _rope(proj("r_q"), cos, sin, first_half)
    kr_ref[...] = _rope(proj("r_k"), cos, sin, first_half) * k_scale
    vr_ref[...] = proj("r_v")
    zf_ref[...] = jnp.dot(proj("f_u").astype(BF16), fd_ref[...].astype(BF16), preferred_element_type=F32)


def _layer_spec(arr, layer):
    return pl.BlockSpec((1,) + arr.shape[1:], lambda *_: (layer,) + (0,) * (arr.ndim - 1),
                        pipeline_mode=pl.Buffered(1))


def _cols_spec(w_all, layer, first, last):
    lo = _OFFS[first]
    width = _OFFS[last] + _SIZE[last] - lo
    return pl.BlockSpec((pl.Element(1), pl.Element(w_all.shape[1]), pl.Element(width)),
                        lambda *_: (layer, 0, lo), pipeline_mode=pl.Buffered(1))


def _inproj_call(x2, mod3, g_pre, w_all, layer, cos_t, sin_t, fd, *, tm, tiles_per_group):
    rows, d = x2.shape
    nt = rows // tm
    tiles_per_seq = cos_t.shape[0] // tm
    widths = (2 * A_WIDTH, 2 * A_KV_WIDTH, 4 * A_KV_WIDTH, B_WIDTH, B_WIDTH, R_WIDTH, R_WIDTH, R_WIDTH, 2 * F_WIDTH)
    dtypes = (BF16, BF16, BF16, F32, F32, F32, F32, F32, F32)
    row_spec = lambda w: pl.BlockSpec((tm, w), lambda i: (i, 0))
    return pl.pallas_call(
        _inproj_kernel,
        grid=(nt,),
        in_specs=[row_spec(d),
                  pl.BlockSpec((1, 1, 3 * d), lambda i: (i // tiles_per_group, 0, 0)),
                  _const_spec((1, d)),
                  *[_cols_spec(w_all, layer, first, last) for first, last in MIX_GROUPS],
                  pl.BlockSpec((tm, LANES), lambda i: (i % tiles_per_seq, 0)),
                  pl.BlockSpec((tm, LANES), lambda i: (i % tiles_per_seq, 0)),
                  _const_spec(fd.shape)],
        out_specs=[row_spec(w) for w in widths],
        out_shape=[jax.ShapeDtypeStruct((rows, w), dt) for w, dt in zip(widths, dtypes)],
        compiler_params=_cparams(("arbitrary",)),
        name="in_proj",
    )(x2, mod3, g_pre, w_all, w_all, w_all, w_all, cos_t, sin_t, fd)


def _attn_kernel(*refs, local):
    if local:
        (sink_ref, q_ref, kp_ref, kc_ref, kn_ref, vp_ref, vc_ref, vn_ref, kx_ref, vx_ref, o_ref,
         kwin, vwin) = refs
    else:
        sink_ref, q_ref, kx_ref, vx_ref, o_ref = refs
    blk = A_BLOCK
    nsub = q_ref.shape[1] // blk
    n = pl.program_id(1)
    last = pl.num_programs(1) * nsub - 1
    contract_last = (((1,), (1,)), ((), ()))

    lane_q = lax.broadcasted_iota(jnp.int32, (blk, LANES), 1)
    nloc = 3 * blk
    if local:
        step = nsub * blk
        kwin[0:blk] = kp_ref[0]
        kwin[blk:blk + step] = kc_ref[0]
        kwin[blk + step:2 * blk + step] = kn_ref[0]
        vwin[0:blk] = vp_ref[0]
        vwin[blk:blk + step] = vc_ref[0]
        vwin[blk + step:2 * blk + step] = vn_ref[0]
        row = lax.broadcasted_iota(jnp.int32, (blk, nloc), 0)
        col = lax.broadcasted_iota(jnp.int32, (blk, nloc), 1)
        band = jnp.minimum(col - row, row + 2 * blk - col)

    for sub in range(nsub):
        q0 = sub * blk
        if local:
            g_blk = n * nsub + sub
            seq_lo = blk - g_blk * blk
            seq_hi = blk + (last - g_blk + 1) * blk
            valid = (jnp.minimum(band, jnp.minimum(col - seq_lo, seq_hi - 1 - col)) >= 0)[None]
        for j in range(A_KV_HEADS):
            kcol = slice(j * LANES, (j + 1) * LANES)
            ecol = slice(2 * j * LANES, (2 * j + 1) * LANES)
            ocol = slice((2 * j + 1) * LANES, (2 * j + 2) * LANES)
            qs = jnp.concatenate([q_ref[0, q0:q0 + blk, (4 * j + g) * LANES:(4 * j + g + 1) * LANES]
                                  for g in range(4)], axis=0)
            s = lax.dot_general(qs, kx_ref[0, :, kcol], contract_last, preferred_element_type=F32)
            if local:
                s_loc = lax.dot_general(qs, kwin[q0:q0 + nloc, kcol], contract_last, preferred_element_type=F32)
                s_loc = jnp.where(valid, s_loc.reshape(4, blk, nloc), NEG_INF).reshape(4 * blk, nloc)
                s = jnp.concatenate([s_loc, s], axis=1)
            sink = jnp.concatenate(
                [jnp.full((blk, 1), sink_ref[4 * j + g] * LOG2E, F32) for g in range(4)], axis=0)
            m = jnp.maximum(jnp.max(s, axis=1, keepdims=True), sink)
            pb = jnp.exp2((s - m).astype(BF16))
            e_sink = jnp.exp2(sink - m)

            vcol = slice(2 * j * LANES, (2 * j + 2) * LANES)
            if local:
                o_all = (jnp.dot(pb[:, :nloc], vwin[q0:q0 + nloc, vcol], preferred_element_type=F32)
                         + jnp.dot(pb[:, nloc:], vx_ref[0, :, vcol], preferred_element_type=F32))
            else:
                o_all = jnp.dot(pb, vx_ref[0, :, vcol], preferred_element_type=F32)

            for pair in range(2):
                r0 = 2 * pair * blk
                o_even = o_all[r0:r0 + blk, :LANES]
                o_odd = o_all[r0 + blk:r0 + 2 * blk, LANES:]
                num = jnp.where(lane_q < 64, o_even, o_odd)
                den = (pltpu.roll(jnp.where(lane_q < 64, o_odd, o_even), 64, 1)
                       + jnp.where(lane_q < 64, e_sink[r0:r0 + blk], e_sink[r0 + blk:r0 + 2 * blk]))
                c0 = (2 * j + pair) * LANES
                o_ref[0, q0:q0 + blk, c0:c0 + LANES] = num / den


def _attn_call(sink, q, k, v, kx, vx, *, local):
    b, sq, _ = q.shape
    blk = A_BLOCK
    nq = sq // blk
    nsub = min(ATTN_SUB, nq)
    step = nsub * blk
    lx = kx.shape[1]
    smem = pl.BlockSpec(memory_space=pltpu.SMEM)
    kw, vw = kx.shape[2], vx.shape[2]
    q_spec = pl.BlockSpec((1, step, q.shape[2]), lambda bi, n: (bi, n, 0))
    if local:
        def halo(w):
            return (pl.BlockSpec((1, blk, w), lambda bi, n: (bi, jnp.maximum(n * nsub - 1, 0), 0)),
                    pl.BlockSpec((1, step, w), lambda bi, n: (bi, n, 0)),
                    pl.BlockSpec((1, blk, w), lambda bi, n: (bi, jnp.minimum((n + 1) * nsub, nq - 1), 0)))
        in_specs = [smem, q_spec, *halo(kw), *halo(vw),
                    pl.BlockSpec((1, lx, kw), lambda bi, n: (bi, 0, 0)),
                    pl.BlockSpec((1, lx, vw), lambda bi, n: (bi, 0, 0))]
        args = (sink, q, k, k, k, v, v, v, kx, vx)
        scratch = [pltpu.VMEM((step + 2 * blk, kw), k.dtype), pltpu.VMEM((step + 2 * blk, vw), v.dtype)]
    else:
        in_specs = [smem, q_spec, pl.BlockSpec((1, lx, kw), lambda bi, n: (bi, 0, 0)),
                    pl.BlockSpec((1, lx, vw), lambda bi, n: (bi, 0, 0))]
        args = (sink, q, kx, vx)
        scratch = []
    return pl.pallas_call(
        functools.partial(_attn_kernel, local=local),
        grid=(b, nq // nsub),
        in_specs=in_specs,
        out_specs=pl.BlockSpec((1, step, A_WIDTH), lambda bi, n: (bi, n, 0)),
        out_shape=jax.ShapeDtypeStruct((b, sq, A_WIDTH), F32),
        scratch_shapes=scratch,
        compiler_params=_cparams(("arbitrary", "arbitrary")),
        name="win_attn" if local else "ctx_attn",
    )(*args)


def _head_of(shape, dim):
    return lax.broadcasted_iota(jnp.int32, shape, dim) // HEAD_DIM


def _group_mean(t, avg):
    hi = t.astype(BF16)
    lo = (t - hi.astype(F32)).astype(BF16)
    return (jnp.dot(hi, avg, preferred_element_type=F32) + jnp.dot(lo, avg, preferred_element_type=F32))


def _ret_kernel(*refs, reverse, finalize):
    if finalize:
        lg_ref, lgl_ref, q_ref, k_ref, v_ref, r0_ref, yin_ref, y_ref, rfin_ref, r_scr = refs
    else:
        lg_ref, lgl_ref, q_ref, k_ref, v_ref, r0_ref, y_ref, rfin_ref, r_scr = refs
    c = pl.program_id(0)
    nb, ch, w = q_ref.shape

    @pl.when(c == 0)
    def _():
        r_scr[...] = r0_ref[...]

    lgl = lgl_ref[...]
    pos = lax.broadcasted_iota(jnp.int32, (ch, 1), 0).astype(F32)
    if reverse:
        xi = jnp.exp(lgl * (ch - pos))
        zeta = jnp.exp(lgl * pos)
    else:
        xi = jnp.exp(lgl * (pos + 1.0))
        zeta = jnp.exp(lgl * (ch - 1.0 - pos))
    g_chunk = jnp.exp(lgl * float(ch))
    ri = lax.broadcasted_iota(jnp.int32, (ch, ch), 0)
    ci = lax.broadcasted_iota(jnp.int32, (ch, ch), 1)
    diff = (ci - ri) if reverse else (ri - ci)
    dist = jnp.maximum(diff, 0).astype(F32)
    decays = [jnp.where(diff >= 0, jnp.exp(lg_ref[h] * dist), 0.0) for h in range(R_HEADS)]
    lane_head = _head_of((ch, w), 1)
    same_head = _head_of((w, w), 0) == _head_of((w, w), 1)
    avg = jnp.where(same_head, 1.0 / HEAD_DIM, 0.0).astype(BF16)

    for bi in range(nb):
        q = q_ref[bi]
        k = k_ref[bi]
        v = v_ref[bi]
        r = r_scr[bi]
        cross = jnp.dot((q * xi).astype(BF16), r.astype(BF16), preferred_element_type=F32)
        q4 = jnp.concatenate([jnp.where(lane_head == h, q, 0.0) for h in range(R_HEADS)], axis=0).astype(BF16)
        sc = lax.dot_general(q4, k.astype(BF16), (((1,), (1,)), ((), ())), preferred_element_type=F32)
        s4 = jnp.concatenate([(sc[h * ch:(h + 1) * ch] * decays[h]).astype(BF16) for h in range(R_HEADS)], axis=1)
        v4 = jnp.concatenate([jnp.where(lane_head == h, v, 0.0) for h in range(R_HEADS)], axis=0).astype(BF16)
        y = jnp.dot(s4, v4, preferred_element_type=F32) + cross

        kz = (k * zeta).astype(BF16)
        ktv = lax.dot_general(kz, v.astype(BF16), (((0,), (0,)), ((), ())), preferred_element_type=F32)
        r_new = g_chunk * r + jnp.where(same_head, ktv, 0.0)
        r_scr[bi] = r_new
        rfin_ref[bi] = r_new

        if finalize:
            y = y + yin_ref[bi]
            mu = _group_mean(y, avg)
            d = y - mu
            var = _group_mean(d * d, avg)
            y = d * lax.rsqrt(var + EPS)
        y_ref[bi] = y


def _ret_call(lg, lgl, q, k, v, r0, y_in, *, reverse):
    b, s, w = q.shape
    ch = min(R_STEP, s)
    nc = s // ch
    finalize = y_in is not None
    cidx = (lambda c: (0, nc - 1 - c, 0)) if reverse else (lambda c: (0, c, 0))
    chunk = pl.BlockSpec((b, ch, w), cidx)
    state = pl.BlockSpec((b, w, w), lambda c: (0, 0, 0))
    in_specs = [pl.BlockSpec(memory_space=pltpu.SMEM), pl.BlockSpec((1, w), lambda c: (0, 0)),
                chunk, chunk, chunk, state]
    args = [lg, lgl, q, k, v, r0]
    if finalize:
        in_specs.append(chunk)
        args.append(y_in)
    return pl.pallas_call(
        functools.partial(_ret_kernel, reverse=reverse, finalize=finalize),
        grid=(nc,),
        in_specs=in_specs,
        out_specs=[chunk, state],
        out_shape=[jax.ShapeDtypeStruct((b, s, w), F32), jax.ShapeDtypeStruct((b, w, w), F32)],
        scratch_shapes=[pltpu.VMEM((b, w, w), F32)],
        compiler_params=_cparams(("arbitrary",)),
        name="retention_bwd" if reverse else "retention_fwd",
    )(*args)


def _dft_cs(n, scale):
    a = 2.0 * np.pi * np.outer(np.arange(n), np.arange(n)) / n
    return np.cos(a) * scale, np.sin(a) * scale


def _fourier_rows_kernel(m_ref, tc_ref, ts_ref, z_ref, o_ref):
    rows, cb = z_ref.shape[1], z_ref.shape[2]
    m = m_ref[...].astype(BF16)
    for ci in range(cb):
        z = z_ref[0, :, ci, :]
        zz = jnp.concatenate([z[:, :F_WIDTH], z[:, F_WIDTH:]], axis=0).astype(BF16)
        a = jnp.dot(m, zz, preferred_element_type=F32)
        a_re, a_im = a[:rows], a[rows:]
        tc = jnp.concatenate([tc_ref[ci]] * (F_WIDTH // LANES), axis=1)
        ts = jnp.concatenate([ts_ref[ci]] * (F_WIDTH // LANES), axis=1)
        o_ref[0, :, 0, ci, :] = (a_re * tc + a_im * ts).astype(o_ref.dtype)
        o_ref[0, :, 1, ci, :] = (a_im * tc - a_re * ts).astype(o_ref.dtype)


def _fourier_cols_kernel(g_ref, b_ref, o_ref):
    g = g_ref[...].astype(BF16)
    kt, _, cw, f = b_ref.shape[1:]
    for i in range(kt):
        bm = b_ref[0, i].reshape(2 * cw, f).astype(BF16)
        o_ref[0, :, i, :] = jnp.dot(g, bm, preferred_element_type=F32)


def _fourier_latent(zf, rows):
    b, s, _ = zf.shape
    cw = GRID_W
    c_r, s_r = _dft_cs(rows, rows ** -0.5)
    m1 = jnp.asarray(np.block([[c_r, s_r], [-s_r, c_r]]), F32)
    ang = 2.0 * np.pi * np.outer(np.arange(cw), np.arange(rows)) / s
    tc = jnp.asarray(np.repeat(np.cos(ang)[:, :, None], LANES, axis=2), F32)
    ts = jnp.asarray(np.repeat(np.sin(ang)[:, :, None], LANES, axis=2), F32)
    cb = 64
    bk = pl.pallas_call(
        _fourier_rows_kernel,
        grid=(cw // cb, b),
        in_specs=[_const_spec(m1.shape),
                  pl.BlockSpec((cb, rows, LANES), lambda j, bi: (j, 0, 0)),
                  pl.BlockSpec((cb, rows, LANES), lambda j, bi: (j, 0, 0)),
                  pl.BlockSpec((1, rows, cb, 2 * F_WIDTH), lambda j, bi: (bi, 0, j, 0))],
        out_specs=pl.BlockSpec((1, rows, 2, cb, F_WIDTH), lambda j, bi: (bi, 0, 0, j, 0)),
        out_shape=jax.ShapeDtypeStruct((b, rows, 2, cw, F_WIDTH), F32),
        compiler_params=_cparams(("arbitrary", "arbitrary")),
        name="fourier_rows",
    )(m1, tc, ts, zf.reshape(b, rows, cw, 2 * F_WIDTH))
    c_c, s_c = _dft_cs(cw, cw ** -0.5)
    g = jnp.asarray(np.concatenate([c_c, s_c], axis=1), F32)
    kt = 64
    out = pl.pallas_call(
        _fourier_cols_kernel,
        grid=(b, rows // kt),
        in_specs=[_const_spec(g.shape),
                  pl.BlockSpec((1, kt, 2, cw, F_WIDTH), lambda bi, i: (bi, i, 0, 0, 0))],
        out_specs=pl.BlockSpec((1, cw, kt, F_WIDTH), lambda bi, i: (bi, 0, i, 0)),
        out_shape=jax.ShapeDtypeStruct((b, cw, rows, F_WIDTH), F32),
        compiler_params=_cparams(("arbitrary", "arbitrary")),
        name="fourier_cols",
    )(g, bk)
    return out.reshape(b, s, F_WIDTH)


def _fourier_dense_kernel(m_ref, z_ref, o_ref):
    z = z_ref[0].astype(F32)
    zz = jnp.concatenate([z[:, :F_WIDTH], z[:, F_WIDTH:]], axis=0)
    o_ref[0] = jnp.dot(m_ref[...], zz, preferred_element_type=F32, precision=HI)


def _fourier_dense(zf):
    b, n, _ = zf.shape
    c_n, s_n = _dft_cs(n, n ** -0.5)
    m = jnp.asarray(np.concatenate([c_n, s_n], axis=1), F32)
    return pl.pallas_call(
        _fourier_dense_kernel,
        grid=(b,),
        in_specs=[_const_spec(m.shape), pl.BlockSpec((1, n, 2 * F_WIDTH), lambda bi: (bi, 0, 0))],
        out_specs=pl.BlockSpec((1, n, F_WIDTH), lambda bi: (bi, 0, 0)),
        out_shape=jax.ShapeDtypeStruct((b, n, F_WIDTH), F32),
        compiler_params=_cparams(("arbitrary",)),
        name="fourier_dense",
    )(m, zf)


def _merge_kernel(x_ref, mod_ref, gpre_ref, gpost_ref, ya_ref, wc_ref, wprev_ref, wnext_ref, bb_ref,
                  yr_ref, yf_ref, cw_ref, cb_ref, wza_ref, wzb_ref, wzr_ref, wfm_ref,
                  woa_ref, wob_ref, wor_ref, wof_ref, wout_ref, o_ref, *, tiles_per_seq):
    i = pl.program_id(0)
    x = x_ref[...]
    mod = mod_ref[0]
    h = _modulated_norm(x, gpre_ref[...], mod).astype(BF16)
    tm = x.shape[0]

    wc = wc_ref[...]
    t = i % tiles_per_seq
    prev_row = jnp.where(t > 0, wprev_ref[7:8, :], 0.0)
    next_row = jnp.where(t < tiles_per_seq - 1, wnext_ref[0:1, :], 0.0)
    row = lax.broadcasted_iota(jnp.int32, wc.shape, 0)
    up = jnp.where(row == 0, prev_row, pltpu.roll(wc, 1, 0))
    dn = jnp.where(row == tm - 1, next_row, pltpu.roll(wc, tm - 1, 0))
    conv = up * cw_ref[0:1, :] + wc * cw_ref[1:2, :] + dn * cw_ref[2:3, :] + cb_ref[...]
    yb = bb_ref[...] * conv

    ys = (ya_ref[...], yb, yr_ref[...], yf_ref[...])
    wo_refs = (woa_ref, wob_ref, wor_ref, wof_ref)
    fz = _SIZE["f_z"]
    wz = (wza_ref[0], wzb_ref[0], wzr_ref[0], wfm_ref[0, :, 0:fz])
    acts = []
    for br in range(N_BRANCH):
        z = jnp.dot(h, wz[br], preferred_element_type=F32)
        acts.append((ys[br] * (z * _sigmoid(z))).astype(BF16))
    y = jnp.zeros((tm, D_MODEL), F32)
    for c0 in range(0, D_MODEL, MERGE_CHUNK):
        total = jnp.zeros((tm, MERGE_CHUNK), F32)
        for br in range(N_BRANCH):
            proj = jnp.dot(acts[br], wo_refs[br][0, :, c0:c0 + MERGE_CHUNK], preferred_element_type=F32)
            g0 = fz + br * D_MODEL + c0
            gate = _sigmoid(jnp.dot(h, wfm_ref[0, :, g0:g0 + MERGE_CHUNK], preferred_element_type=F32))
            total = total + gate * proj
        y = y + jnp.dot(total.astype(BF16), wout_ref[0, c0:c0 + MERGE_CHUNK, :], preferred_element_type=F32)
    ms = jnp.mean(y * y, axis=-1, keepdims=True)
    yn = y * lax.rsqrt(ms + EPS) * gpost_ref[...]
    o_ref[...] = x + mod[:, 2 * D_MODEL:3 * D_MODEL] * yn


def _merge_call(x2, mod3, g_pre, g_post, ya, wc, bb, yr, yf, conv_w, conv_b, w_all, wos, wout, layer,
                *, tm, tiles_per_group, tiles_per_seq):
    rows, d = x2.shape
    nt = rows // tm
    hb = tm // 8
    nhb = rows // 8
    row_spec = lambda w: pl.BlockSpec((tm, w), lambda i: (i, 0))
    cols = functools.partial(_cols_spec, w_all, layer)
    return pl.pallas_call(
        functools.partial(_merge_kernel, tiles_per_seq=tiles_per_seq),
        grid=(nt,),
        in_specs=[row_spec(d),
                  pl.BlockSpec((1, 1, 3 * d), lambda i: (i // tiles_per_group, 0, 0)),
                  _const_spec((1, d)), _const_spec((1, d)),
                  row_spec(A_WIDTH),
                  row_spec(B_WIDTH),
                  pl.BlockSpec((8, B_WIDTH), lambda i: (jnp.maximum(i * hb - 1, 0), 0)),
                  pl.BlockSpec((8, B_WIDTH), lambda i: (jnp.minimum((i + 1) * hb, nhb - 1), 0)),
                  row_spec(B_WIDTH), row_spec(R_WIDTH), row_spec(F_WIDTH),
                  _const_spec(conv_w.shape), _const_spec(conv_b.shape),
                  cols("a_z", "a_z"), cols("b_z", "b_z"), cols("r_z", "r_z"), cols("f_z", "merge"),
                  *[_layer_spec(w, layer) for w in wos], _layer_spec(wout, layer)],
        out_specs=row_spec(d),
        out_shape=jax.ShapeDtypeStruct((rows, d), F32),
        compiler_params=_cparams(("arbitrary",)),
        name="merge",
    )(x2, mod3, g_pre, g_post, ya, wc, wc, wc, bb, yr, yf, conv_w, conv_b, w_all, w_all, w_all, w_all, *wos, wout)


def _rope_tables(n):
    rows = n // GRID_W
    row = jnp.broadcast_to(jnp.arange(rows)[:, None], (rows, GRID_W)).reshape(-1).astype(F32)
    col = jnp.broadcast_to(jnp.arange(GRID_W)[None, :], (rows, GRID_W)).reshape(-1).astype(F32)
    half = HEAD_DIM // 2
    inv = ROPE_BASE ** (-jnp.arange(0, half, 2, dtype=F32) / half)
    ang_r = row[:, None] * inv
    ang_c = col[:, None] * inv
    cos = jnp.concatenate([jnp.cos(ang_r), jnp.cos(ang_r), jnp.cos(ang_c), jnp.cos(ang_c)], axis=1)
    sin = jnp.concatenate([-jnp.sin(ang_r), jnp.sin(ang_r), -jnp.sin(ang_c), jnp.sin(ang_c)], axis=1)
    return jnp.tile(cos, (1, LANES // HEAD_DIM)), jnp.tile(sin, (1, LANES // HEAD_DIM))


def kernel(x, c, ctx, c_ctx, w_ada, b_ada, norm_pre, norm_post, w_in, attn_sink, conv_w, conv_b, ret_decay,
           w_o_attn, w_o_conv, w_o_ret, w_o_fourier, w_out):
    b, s, d = x.shape
    lc = ctx.shape[1]
    depth = w_in.shape[0]
    rows_grid = s // GRID_W

    cv = jnp.zeros((8, d), F32).at[:b].set(c).at[b].set(c_ctx)
    mods = _ada_call(cv, w_ada, b_ada)

    cos_x, sin_x = _rope_tables(s)
    cos_c = jnp.ones((b * lc, LANES), F32)
    sin_c = jnp.zeros((b * lc, LANES), F32)

    c64, s64 = _dft_cs(HEAD_DIM, HEAD_DIM ** -0.5)
    eye = np.eye(F_WIDTH // HEAD_DIM)
    fd = jnp.asarray(np.concatenate([np.kron(eye, c64), -np.kron(eye, s64)], axis=1), F32)

    lg_all = jax.nn.log_sigmoid(ret_decay.astype(F32))
    zero_state = jnp.zeros((b, R_WIDTH, R_WIDTH), F32)

    w_all = w_in.astype(BF16)
    wos = tuple(w.astype(BF16) for w in (w_o_attn, w_o_conv, w_o_ret, w_o_fourier))
    wout = w_out.astype(BF16)

    tm_x = 1024
    tm_m = 1024
    x2 = x.reshape(b * s, d)
    xc2 = ctx.reshape(b * lc, d)
    for l in range(depth):
        update_ctx = l < depth - 1
        g_pre = norm_pre[l].reshape(1, d)
        g_post = norm_post[l].reshape(1, d)
        mod_x = mods[l, :b].reshape(b, 1, 3 * d)
        mod_c = mods[l, b:b + 1].reshape(1, 1, 3 * d)
        cb = conv_b[l].reshape(1, B_WIDTH)
        lg = lg_all[l]
        lgl = jnp.repeat(lg, HEAD_DIM, axis=1)

        px = _inproj_call(x2, mod_x, g_pre, w_all, l, cos_x, sin_x, fd, tm=tm_x, tiles_per_group=s // tm_x)
        pc = _inproj_call(xc2, mod_c, g_pre, w_all, l, cos_c, sin_c, fd, tm=b * lc, tiles_per_group=1)
        qa, ka, va, wcx, bbx, qr, kr, vr, zf = [t.reshape(b, s, -1) for t in px]
        qac, kac, vac, wcc, bbc, qrc, krc, vrc, zfc = [t.reshape(b, lc, -1) for t in pc]

        ya = _attn_call(attn_sink[l], qa, ka, va, kac, vac, local=True)

        ycf, st_f = _ret_call(lg[0], lgl[0:1], qrc, krc, vrc, zero_state, None, reverse=False)
        ycr, st_b = _ret_call(lg[1], lgl[1:2], qrc, krc, vrc, zero_state, ycf, reverse=True)
        yf_, _ = _ret_call(lg[0], lgl[0:1], qr, kr, vr, st_f, None, reverse=False)
        yr, _ = _ret_call(lg[1], lgl[1:2], qr, kr, vr, st_b, yf_, reverse=True)

        yfo = _fourier_latent(zf, rows_grid)

        x2_new = _merge_call(x2, mod_x, g_pre, g_post, ya.reshape(b * s, -1), wcx.reshape(b * s, -1),
                             bbx.reshape(b * s, -1), yr.reshape(b * s, -1), yfo.reshape(b * s, -1),
                             conv_w[l], cb, w_all, wos, wout, l,
                             tm=tm_m, tiles_per_group=s // tm_m, tiles_per_seq=s // tm_m)
        if update_ctx:
            yac = _attn_call(attn_sink[l], qac, None, None, kac, vac, local=False)
            yfc = _fourier_dense(zfc)
            xc2 = _merge_call(xc2, mod_c, g_pre, g_post, yac.reshape(b * lc, -1), wcc.reshape(b * lc, -1),
                              bbc.reshape(b * lc, -1), ycr.reshape(b * lc, -1), yfc.reshape(b * lc, -1),
                              conv_w[l], cb, w_all, wos, wout, l,
                              tm=lc, tiles_per_group=b, tiles_per_seq=1)
        x2 = x2_new
    return x2.reshape(b, s, d)
```

```python
import functools
import math

import numpy as np
import jax
import jax.numpy as jnp
from jax import lax
from jax.experimental import pallas as pl
from jax.experimental.pallas import tpu as pltpu

D_MODEL = 1024
GRID_W = 64
HEAD_DIM = 64
EPS = 1e-6
NEG_INF = -1e30
A_HEADS = 8
A_KV_HEADS = 2
A_BLOCK = 128
A_WIDTH = A_HEADS * HEAD_DIM
A_KV_WIDTH = A_KV_HEADS * HEAD_DIM
ROPE_BASE = 10000.0
B_WIDTH = 256
R_HEADS = 4
R_WIDTH = R_HEADS * HEAD_DIM
R_STEP = 256
F_WIDTH = 256
N_BRANCH = 4

IN_NAMES = ("a_q", "a_k", "a_v", "a_z", "b_u", "b_b", "b_c", "b_z", "r_q", "r_k", "r_v", "r_z", "f_u", "f_z", "merge")
IN_SIZES = (A_WIDTH, A_KV_WIDTH, A_KV_WIDTH, A_WIDTH, B_WIDTH, B_WIDTH, B_WIDTH, B_WIDTH,
            R_WIDTH, R_WIDTH, R_WIDTH, R_WIDTH, F_WIDTH, F_WIDTH, N_BRANCH * D_MODEL)
_OFFS = dict(zip(IN_NAMES, np.cumsum((0,) + IN_SIZES)[:-1].tolist()))
_SIZE = dict(zip(IN_NAMES, IN_SIZES))

Z_NAMES = ("a_z", "b_z", "r_z", "f_z")
Z_SIZES = tuple(_SIZE[n] for n in Z_NAMES)

MERGE_CHUNK = 256
ATTN_SUB = 8
LOG2E = math.log2(math.e)
LANES = 128
VMEM_LIMIT = 56 * 1024 * 1024

BF16 = jnp.bfloat16
F32 = jnp.float32
HI = lax.Precision.HIGHEST


def _cparams(sem):
    return pltpu.CompilerParams(dimension_semantics=sem, vmem_limit_bytes=VMEM_LIMIT)


def _const_spec(shape):
    nd = len(shape)
    return pl.BlockSpec(shape, lambda *_: (0,) * nd)


def _sigmoid(v):
    return 1.0 / (1.0 + jnp.exp(-v))


def _ada_kernel(cv_ref, w_ref, b_ref, o_ref):
    cv = cv_ref[...]
    s = cv * _sigmoid(cv)
    o_ref[0] = jnp.dot(s, w_ref[0], preferred_element_type=F32, precision=HI) + b_ref[0]


def _ada_call(cv, w_ada, b_ada):
    depth, d, d3 = w_ada.shape
    tn = 1024
    return pl.pallas_call(
        _ada_kernel,
        grid=(depth, d3 // tn),
        in_specs=[pl.BlockSpec((8, d), lambda l, j: (0, 0)),
                  pl.BlockSpec((1, d, tn), lambda l, j: (l, 0, j)),
                  pl.BlockSpec((1, 1, tn), lambda l, j: (l, 0, j))],
        out_specs=pl.BlockSpec((1, 8, tn), lambda l, j: (l, 0, j)),
        out_shape=jax.ShapeDtypeStruct((depth, 8, d3), F32),
        compiler_params=_cparams(("arbitrary", "arbitrary")),
        name="ada_mod",
    )(cv, w_ada, b_ada.reshape(depth, 1, d3))


def _modulated_norm(x, g, mod):
    ms = jnp.mean(x * x, axis=-1, keepdims=True)
    y = x * lax.rsqrt(ms + EPS) * g
    return y * (1.0 + mod[:, D_MODEL:2 * D_MODEL]) + mod[:, 0:D_MODEL]


def _rope(t, cos, sin_signed, first_half):
    outs = []
    for j in range(t.shape[1] // LANES):
        tj = t[:, j * LANES:(j + 1) * LANES]
        partner = jnp.where(first_half, pltpu.roll(tj, LANES - 16, 1), pltpu.roll(tj, 16, 1))
        outs.append(tj * cos + partner * sin_signed)
    return outs[0] if len(outs) == 1 else jnp.concatenate(outs, axis=1)


MIX_GROUPS = (("a_q", "a_v"), ("b_u", "b_c"), ("r_q", "r_v"), ("f_u", "f_u"))


def _inproj_kernel(x_ref, mod_ref, g_ref, wa_ref, wb_ref, wr_ref, wf_ref, cos_ref, sin_ref, fd_ref,
                   qa_ref, ka_ref, va_ref, wc_ref, bb_ref, qr_ref, kr_ref, vr_ref, zf_ref):
    h = _modulated_norm(x_ref[...], g_ref[...], mod_ref[0]).astype(BF16)
    w_refs = {"a": wa_ref, "b": wb_ref, "r": wr_ref, "f": wf_ref}
    starts = {first[0]: _OFFS[first] for first, _ in MIX_GROUPS}

    def proj(name):
        o = _OFFS[name] - starts[name[0]]
        return jnp.dot(h, w_refs[name[0]][0, :, o:o + _SIZE[name]], preferred_element_type=F32)

    cos = cos_ref[...]
    sin = sin_ref[...]
    lane = lax.broadcasted_iota(jnp.int32, cos.shape, 1)
    first_half = (lane % 32) < 16
    low_half = lane < HEAD_DIM
    k_scale = HEAD_DIM ** -0.5
    q = _rope(proj("a_q"), cos, sin, first_half) * (k_scale * LOG2E)
    for p in range(A_WIDTH // LANES):
        qp = q[:, p * LANES:(p + 1) * LANES]
        qa_ref[:, (2 * p) * LANES:(2 * p + 1) * LANES] = jnp.where(low_half, qp, 0.0).astype(BF16)
        qa_ref[:, (2 * p + 1) * LANES:(2 * p + 2) * LANES] = jnp.where(low_half, 0.0, qp).astype(BF16)
    k = _rope(proj("a_k"), cos, sin, first_half)
    k_sw = pltpu.roll(k, HEAD_DIM, 1)
    ka_ref[:, 0:LANES] = jnp.where(low_half, k, k_sw).astype(BF16)
    ka_ref[:, LANES:2 * LANES] = jnp.where(low_half, k_sw, k).astype(BF16)
    v = proj("a_v")
    v_sw = pltpu.roll(v, HEAD_DIM, 1)
    va_ref[:, 0:LANES] = jnp.where(low_half, v, 1.0).astype(BF16)
    va_ref[:, LANES:2 * LANES] = jnp.where(low_half, 1.0, v_sw).astype(BF16)
    va_ref[:, 2 * LANES:3 * LANES] = jnp.where(low_half, v_sw, 1.0).astype(BF16)
    va_ref[:, 3 * LANES:4 * LANES] = jnp.where(low_half, 1.0, v).astype(BF16)
    wc_ref[...] = proj("b_c") * proj("b_u")
    bb_ref[...] = proj("b_b")
    qr_ref[...] = _rope(proj("r_q"), cos, sin, first_half)
    kr_ref[...] = _rope(proj("r_k"), cos, sin, first_half) * k_scale
    vr_ref[...] = proj("r_v")
    zf_ref[...] = jnp.dot(proj("f_u").astype(BF16), fd_ref[...].astype(BF16), preferred_element_type=F32)


def _layer_spec(arr, layer):
    return pl.BlockSpec((1,) + arr.shape[1:], lambda *_: (layer,) + (0,) * (arr.ndim - 1),
                        pipeline_mode=pl.Buffered(1))


def _cols_spec(w_all, layer, first, last):
    lo = _OFFS[first]
    width = _OFFS[last] + _SIZE[last] - lo
    return pl.BlockSpec((pl.Element(1), pl.Element(w_all.shape[1]), pl.Element(width)),
                        lambda *_: (layer, 0, lo), pipeline_mode=pl.Buffered(1))


def _inproj_call(x2, mod3, g_pre, w_all, layer, cos_t, sin_t, fd, *, tm, tiles_per_group):
    rows, d = x2.shape
    nt = rows // tm
    tiles_per_seq = cos_t.shape[0] // tm
    widths = (2 * A_WIDTH, 2 * A_KV_WIDTH, 4 * A_KV_WIDTH, B_WIDTH, B_WIDTH, R_WIDTH, R_WIDTH, R_WIDTH, 2 * F_WIDTH)
    dtypes = (BF16, BF16, BF16, F32, F32, F32, F32, F32, F32)
    row_spec = lambda w: pl.BlockSpec((tm, w), lambda i: (i, 0))
    return pl.pallas_call(
        _inproj_kernel,
        grid=(nt,),
        in_specs=[row_spec(d),
                  pl.BlockSpec((1, 1, 3 * d), lambda i: (i // tiles_per_group, 0, 0)),
                  _const_spec((1, d)),
                  *[_cols_spec(w_all, layer, first, last) for first, last in MIX_GROUPS],
                  pl.BlockSpec((tm, LANES), lambda i: (i % tiles_per_seq, 0)),
                  pl.BlockSpec((tm, LANES), lambda i: (i % tiles_per_seq, 0)),
                  _const_spec(fd.shape)],
        out_specs=[row_spec(w) for w in widths],
        out_shape=[jax.ShapeDtypeStruct((rows, w), dt) for w, dt in zip(widths, dtypes)],
        compiler_params=_cparams(("arbitrary",)),
        name="in_proj",
    )(x2, mod3, g_pre, w_all, w_all, w_all, w_all, cos_t, sin_t, fd)


def _attn_kernel(*refs, local):
    if local:
        (sink_ref, q_ref, kp_ref, kc_ref, kn_ref, vp_ref, vc_ref, vn_ref, kx_ref, vx_ref, o_ref,
         kwin, vwin) = refs
    else:
        sink_ref, q_ref, kx_ref, vx_ref, o_ref = refs
    blk = A_BLOCK
    nsub = q_ref.shape[1] // blk
    n = pl.program_id(1)
    last = pl.num_programs(1) * nsub - 1
    contract_last = (((1,), (1,)), ((), ()))

    lane_q = lax.broadcasted_iota(jnp.int32, (blk, LANES), 1)
    nloc = 3 * blk
    if local:
        step = nsub * blk
        kwin[0:blk] = kp_ref[0]
        kwin[blk:blk + step] = kc_ref[0]
        kwin[blk + step:2 * blk + step] = kn_ref[0]
        vwin[0:blk] = vp_ref[0]
        vwin[blk:blk + step] = vc_ref[0]
        vwin[blk + step:2 * blk + step] = vn_ref[0]
        row = lax.broadcasted_iota(jnp.int32, (blk, nloc), 0)
        col = lax.broadcasted_iota(jnp.int32, (blk, nloc), 1)
        band = jnp.minimum(col - row, row + 2 * blk - col)

    for sub in range(nsub):
        q0 = sub * blk
        if local:
            g_blk = n * nsub + sub
            seq_lo = blk - g_blk * blk
            seq_hi = blk + (last - g_blk + 1) * blk
            valid = (jnp.minimum(band, jnp.minimum(col - seq_lo, seq_hi - 1 - col)) >= 0)[None]
        for j in range(A_KV_HEADS):
            kcol = slice(j * LANES, (j + 1) * LANES)
            ecol = slice(2 * j * LANES, (2 * j + 1) * LANES)
            ocol = slice((2 * j + 1) * LANES, (2 * j + 2) * LANES)
            qs = jnp.concatenate([q_ref[0, q0:q0 + blk, (4 * j + g) * LANES:(4 * j + g + 1) * LANES]
                                  for g in range(4)], axis=0)
            s = lax.dot_general(qs, kx_ref[0, :, kcol], contract_last, preferred_element_type=F32)
            if local:
                s_loc = lax.dot_general(qs, kwin[q0:q0 + nloc, kcol], contract_last, preferred_element_type=F32)
                s_loc = jnp.where(valid, s_loc.reshape(4, blk, nloc), NEG_INF).reshape(4 * blk, nloc)
                s = jnp.concatenate([s_loc, s], axis=1)
            sink = jnp.concatenate(
                [jnp.full((blk, 1), sink_ref[4 * j + g] * LOG2E, F32) for g in range(4)], axis=0)
            m = jnp.maximum(jnp.max(s, axis=1, keepdims=True), sink)
            pb = jnp.exp2((s - m).astype(BF16))
            e_sink = jnp.exp2(sink - m)

            vcol = slice(2 * j * LANES, (2 * j + 2) * LANES)
            if local:
                o_all = (jnp.dot(pb[:, :nloc], vwin[q0:q0 + nloc, vcol], preferred_element_type=F32)
                         + jnp.dot(pb[:, nloc:], vx_ref[0, :, vcol], preferred_element_type=F32))
            else:
                o_all = jnp.dot(pb, vx_ref[0, :, vcol], preferred_element_type=F32)

            for pair in range(2):
                r0 = 2 * pair * blk
                o_even = o_all[r0:r0 + blk, :LANES]
                o_odd = o_all[r0 + blk:r0 + 2 * blk, LANES:]
                num = jnp.where(lane_q < 64, o_even, o_odd)
                den = (pltpu.roll(jnp.where(lane_q < 64, o_odd, o_even), 64, 1)
                       + jnp.where(lane_q < 64, e_sink[r0:r0 + blk], e_sink[r0 + blk:r0 + 2 * blk]))
                c0 = (2 * j + pair) * LANES
                o_ref[0, q0:q0 + blk, c0:c0 + LANES] = num / den


def _attn_call(sink, q, k, v, kx, vx, *, local):
    b, sq, _ = q.shape
    blk = A_BLOCK
    nq = sq // blk
    nsub = min(ATTN_SUB, nq)
    step = nsub * blk
    lx = kx.shape[1]
    smem = pl.BlockSpec(memory_space=pltpu.SMEM)
    kw, vw = kx.shape[2], vx.shape[2]
    q_spec = pl.BlockSpec((1, step, q.shape[2]), lambda bi, n: (bi, n, 0))
    if local:
        def halo(w):
            return (pl.BlockSpec((1, blk, w), lambda bi, n: (bi, jnp.maximum(n * nsub - 1, 0), 0)),
                    pl.BlockSpec((1, step, w), lambda bi, n: (bi, n, 0)),
                    pl.BlockSpec((1, blk, w), lambda bi, n: (bi, jnp.minimum((n + 1) * nsub, nq - 1), 0)))
        in_specs = [smem, q_spec, *halo(kw), *halo(vw),
                    pl.BlockSpec((1, lx, kw), lambda bi, n: (bi, 0, 0)),
                    pl.BlockSpec((1, lx, vw), lambda bi, n: (bi, 0, 0))]
        args = (sink, q, k, k, k, v, v, v, kx, vx)
        scratch = [pltpu.VMEM((step + 2 * blk, kw), k.dtype), pltpu.VMEM((step + 2 * blk, vw), v.dtype)]
    else:
        in_specs = [smem, q_spec, pl.BlockSpec((1, lx, kw), lambda bi, n: (bi, 0, 0)),
                    pl.BlockSpec((1, lx, vw), lambda bi, n: (bi, 0, 0))]
        args = (sink, q, kx, vx)
        scratch = []
    return pl.pallas_call(
        functools.partial(_attn_kernel, local=local),
        grid=(b, nq // nsub),
        in_specs=in_specs,
        out_specs=pl.BlockSpec((1, step, A_WIDTH), lambda bi, n: (bi, n, 0)),
        out_shape=jax.ShapeDtypeStruct((b, sq, A_WIDTH), F32),
        scratch_shapes=scratch,
        compiler_params=_cparams(("arbitrary", "arbitrary")),
        name="win_attn" if local else "ctx_attn",
    )(*args)


def _head_of(shape, dim):
    return lax.broadcasted_iota(jnp.int32, shape, dim) // HEAD_DIM


def _group_mean(t, avg):
    hi = t.astype(BF16)
    lo = (t - hi.astype(F32)).astype(BF16)
    return (jnp.dot(hi, avg, preferred_element_type=F32) + jnp.dot(lo, avg, preferred_element_type=F32))


def _ret_kernel(*refs, reverse, finalize):
    if finalize:
        lg_ref, lgl_ref, q_ref, k_ref, v_ref, r0_ref, yin_ref, y_ref, rfin_ref, r_scr = refs
    else:
        lg_ref, lgl_ref, q_ref, k_ref, v_ref, r0_ref, y_ref, rfin_ref, r_scr = refs
    c = pl.program_id(0)
    nb, ch, w = q_ref.shape

    @pl.when(c == 0)
    def _():
        r_scr[...] = r0_ref[...]

    lgl = lgl_ref[...]
    pos = lax.broadcasted_iota(jnp.int32, (ch, 1), 0).astype(F32)
    if reverse:
        xi = jnp.exp(lgl * (ch - pos))
        zeta = jnp.exp(lgl * pos)
    else:
        xi = jnp.exp(lgl * (pos + 1.0))
        zeta = jnp.exp(lgl * (ch - 1.0 - pos))
    g_chunk = jnp.exp(lgl * float(ch))
    ri = lax.broadcasted_iota(jnp.int32, (ch, ch), 0)
    ci = lax.broadcasted_iota(jnp.int32, (ch, ch), 1)
    diff = (ci - ri) if reverse else (ri - ci)
    dist = jnp.maximum(diff, 0).astype(F32)
    decays = [jnp.where(diff >= 0, jnp.exp(lg_ref[h] * dist), 0.0) for h in range(R_HEADS)]
    lane_head = _head_of((ch, w), 1)
    same_head = _head_of((w, w), 0) == _head_of((w, w), 1)
    avg = jnp.where(same_head, 1.0 / HEAD_DIM, 0.0).astype(BF16)

    for bi in range(nb):
        q = q_ref[bi]
        k = k_ref[bi]
        v = v_ref[bi]
        r = r_scr[bi]
        cross = jnp.dot((q * xi).astype(BF16), r.astype(BF16), preferred_element_type=F32)
        q4 = jnp.concatenate([jnp.where(lane_head == h, q, 0.0) for h in range(R_HEADS)], axis=0).astype(BF16)
        sc = lax.dot_general(q4, k.astype(BF16), (((1,), (1,)), ((), ())), preferred_element_type=F32)
        s4 = jnp.concatenate([(sc[h * ch:(h + 1) * ch] * decays[h]).astype(BF16) for h in range(R_HEADS)], axis=1)
        v4 = jnp.concatenate([jnp.where(lane_head == h, v, 0.0) for h in range(R_HEADS)], axis=0).astype(BF16)
        y = jnp.dot(s4, v4, preferred_element_type=F32) + cross

        kz = (k * zeta).astype(BF16)
        ktv = lax.dot_general(kz, v.astype(BF16), (((0,), (0,)), ((), ())), preferred_element_type=F32)
        r_new = g_chunk * r + jnp.where(same_head, ktv, 0.0)
        r_scr[bi] = r_new
        rfin_ref[bi] = r_new

        if finalize:
            y = y + yin_ref[bi]
            mu = _group_mean(y, avg)
            d = y - mu
            var = _group_mean(d * d, avg)
            y = d * lax.rsqrt(var + EPS)
        y_ref[bi] = y


def _ret_call(lg, lgl, q, k, v, r0, y_in, *, reverse):
    b, s, w = q.shape
    ch = min(R_STEP, s)
    nc = s // ch
    finalize = y_in is not None
    cidx = (lambda c: (0, nc - 1 - c, 0)) if reverse else (lambda c: (0, c, 0))
    chunk = pl.BlockSpec((b, ch, w), cidx)
    state = pl.BlockSpec((b, w, w), lambda c: (0, 0, 0))
    in_specs = [pl.BlockSpec(memory_space=pltpu.SMEM), pl.BlockSpec((1, w), lambda c: (0, 0)),
                chunk, chunk, chunk, state]
    args = [lg, lgl, q, k, v, r0]
    if finalize:
        in_specs.append(chunk)
        args.append(y_in)
    return pl.pallas_call(
        functools.partial(_ret_kernel, reverse=reverse, finalize=finalize),
        grid=(nc,),
        in_specs=in_specs,
        out_specs=[chunk, state],
        out_shape=[jax.ShapeDtypeStruct((b, s, w), F32), jax.ShapeDtypeStruct((b, w, w), F32)],
        scratch_shapes=[pltpu.VMEM((b, w, w), F32)],
        compiler_params=_cparams(("arbitrary",)),
        name="retention_bwd" if reverse else "retention_fwd",
    )(*args)


def _dft_cs(n, scale):
    a = 2.0 * np.pi * np.outer(np.arange(n), np.arange(n)) / n
    return np.cos(a) * scale, np.sin(a) * scale


def _fourier_rows_kernel(m_ref, tc_ref, ts_ref, z_ref, o_ref):
    rows, cb = z_ref.shape[1], z_ref.shape[2]
    m = m_ref[...].astype(BF16)
    for ci in range(cb):
        z = z_ref[0, :, ci, :]
        zz = jnp.concatenate([z[:, :F_WIDTH], z[:, F_WIDTH:]], axis=0).astype(BF16)
        a = jnp.dot(m, zz, preferred_element_type=F32)
        a_re, a_im = a[:rows], a[rows:]
        tc = jnp.concatenate([tc_ref[ci]] * (F_WIDTH // LANES), axis=1)
        ts = jnp.concatenate([ts_ref[ci]] * (F_WIDTH // LANES), axis=1)
        o_ref[0, :, 0, ci, :] = (a_re * tc + a_im * ts).astype(o_ref.dtype)
        o_ref[0, :, 1, ci, :] = (a_im * tc - a_re * ts).astype(o_ref.dtype)


def _fourier_cols_kernel(g_ref, b_ref, o_ref):
    g = g_ref[...].astype(BF16)
    kt, _, cw, f = b_ref.shape[1:]
    for i in range(kt):
        bm = b_ref[0, i].reshape(2 * cw, f).astype(BF16)
        o_ref[0, :, i, :] = jnp.dot(g, bm, preferred_element_type=F32)


def _fourier_latent(zf, rows):
    b, s, _ = zf.shape
    cw = GRID_W
    c_r, s_r = _dft_cs(rows, rows ** -0.5)
    m1 = jnp.asarray(np.block([[c_r, s_r], [-s_r, c_r]]), F32)
    ang = 2.0 * np.pi * np.outer(np.arange(cw), np.arange(rows)) / s
    tc = jnp.asarray(np.repeat(np.cos(ang)[:, :, None], LANES, axis=2), F32)
    ts = jnp.asarray(np.repeat(np.sin(ang)[:, :, None], LANES, axis=2), F32)
    cb = 32
    bk = pl.pallas_call(
        _fourier_rows_kernel,
        grid=(cw // cb, b),
        in_specs=[_const_spec(m1.shape),
                  pl.BlockSpec((cb, rows, LANES), lambda j, bi: (j, 0, 0)),
                  pl.BlockSpec((cb, rows, LANES), lambda j, bi: (j, 0, 0)),
                  pl.BlockSpec((1, rows, cb, 2 * F_WIDTH), lambda j, bi: (bi, 0, j, 0))],
        out_specs=pl.BlockSpec((1, rows, 2, cb, F_WIDTH), lambda j, bi: (bi, 0, 0, j, 0)),
        out_shape=jax.ShapeDtypeStruct((b, rows, 2, cw, F_WIDTH), F32),
        compiler_params=_cparams(("arbitrary", "arbitrary")),
        name="fourier_rows",
    )(m1, tc, ts, zf.reshape(b, rows, cw, 2 * F_WIDTH))
    c_c, s_c = _dft_cs(cw, cw ** -0.5)
    g = jnp.asarray(np.concatenate([c_c, s_c], axis=1), F32)
    kt = 64
    out = pl.pallas_call(
        _fourier_cols_kernel,
        grid=(b, rows // kt),
        in_specs=[_const_spec(g.shape),
                  pl.BlockSpec((1, kt, 2, cw, F_WIDTH), lambda bi, i: (bi, i, 0, 0, 0))],
        out_specs=pl.BlockSpec((1, cw, kt, F_WIDTH), lambda bi, i: (bi, 0, i, 0)),
        out_shape=jax.ShapeDtypeStruct((b, cw, rows, F_WIDTH), F32),
        compiler_params=_cparams(("arbitrary", "arbitrary")),
        name="fourier_cols",
    )(g, bk)
    return out.reshape(b, s, F_WIDTH)


def _fourier_dense_kernel(m_ref, z_ref, o_ref):
    z = z_ref[0].astype(F32)
    zz = jnp.concatenate([z[:, :F_WIDTH], z[:, F_WIDTH:]], axis=0)
    o_ref[0] = jnp.dot(m_ref[...], zz, preferred_element_type=F32, precision=HI)


def _fourier_dense(zf):
    b, n, _ = zf.shape
    c_n, s_n = _dft_cs(n, n ** -0.5)
    m = jnp.asarray(np.concatenate([c_n, s_n], axis=1), F32)
    return pl.pallas_call(
        _fourier_dense_kernel,
        grid=(b,),
        in_specs=[_const_spec(m.shape), pl.BlockSpec((1, n, 2 * F_WIDTH), lambda bi: (bi, 0, 0))],
        out_specs=pl.BlockSpec((1, n, F_WIDTH), lambda bi: (bi, 0, 0)),
        out_shape=jax.ShapeDtypeStruct((b, n, F_WIDTH), F32),
        compiler_params=_cparams(("arbitrary",)),
        name="fourier_dense",
    )(m, zf)


def _merge_kernel(x_ref, mod_ref, gpre_ref, gpost_ref, ya_ref, wc_ref, wprev_ref, wnext_ref, bb_ref,
                  yr_ref, yf_ref, cw_ref, cb_ref, wza_ref, wzb_ref, wzr_ref, wfm_ref,
                  woa_ref, wob_ref, wor_ref, wof_ref, wout_ref, o_ref, *, tiles_per_seq):
    i = pl.program_id(0)
    x = x_ref[...]
    mod = mod_ref[0]
    h = _modulated_norm(x, gpre_ref[...], mod).astype(BF16)
    tm = x.shape[0]

    wc = wc_ref[...]
    t = i % tiles_per_seq
    prev_row = jnp.where(t > 0, wprev_ref[7:8, :], 0.0)
    next_row = jnp.where(t < tiles_per_seq - 1, wnext_ref[0:1, :], 0.0)
    row = lax.broadcasted_iota(jnp.int32, wc.shape, 0)
    up = jnp.where(row == 0, prev_row, pltpu.roll(wc, 1, 0))
    dn = jnp.where(row == tm - 1, next_row, pltpu.roll(wc, tm - 1, 0))
    conv = up * cw_ref[0:1, :] + wc * cw_ref[1:2, :] + dn * cw_ref[2:3, :] + cb_ref[...]
    yb = bb_ref[...] * conv

    ys = (ya_ref[...], yb, yr_ref[...], yf_ref[...])
    wo_refs = (woa_ref, wob_ref, wor_ref, wof_ref)
    fz = _SIZE["f_z"]
    wz = (wza_ref[0], wzb_ref[0], wzr_ref[0], wfm_ref[0, :, 0:fz])
    acts = []
    for br in range(N_BRANCH):
        z = jnp.dot(h, wz[br], preferred_element_type=F32)
        acts.append((ys[br] * (z * _sigmoid(z))).astype(BF16))
    y = jnp.zeros((tm, D_MODEL), F32)
    for c0 in range(0, D_MODEL, MERGE_CHUNK):
        total = jnp.zeros((tm, MERGE_CHUNK), F32)
        for br in range(N_BRANCH):
            proj = jnp.dot(acts[br], wo_refs[br][0, :, c0:c0 + MERGE_CHUNK], preferred_element_type=F32)
            g0 = fz + br * D_MODEL + c0
            gate = _sigmoid(jnp.dot(h, wfm_ref[0, :, g0:g0 + MERGE_CHUNK], preferred_element_type=F32))
            total = total + gate * proj
        y = y + jnp.dot(total.astype(BF16), wout_ref[0, c0:c0 + MERGE_CHUNK, :], preferred_element_type=F32)
    ms = jnp.mean(y * y, axis=-1, keepdims=True)
    yn = y * lax.rsqrt(ms + EPS) * gpost_ref[...]
    o_ref[...] = x + mod[:, 2 * D_MODEL:3 * D_MODEL] * yn


def _merge_call(x2, mod3, g_pre, g_post, ya, wc, bb, yr, yf, conv_w, conv_b, w_all, wos, wout, layer,
                *, tm, tiles_per_group, tiles_per_seq):
    rows, d = x2.shape
    nt = rows // tm
    hb = tm // 8
    nhb = rows // 8
    row_spec = lambda w: pl.BlockSpec((tm, w), lambda i: (i, 0))
    cols = functools.partial(_cols_spec, w_all, layer)
    return pl.pallas_call(
        functools.partial(_merge_kernel, tiles_per_seq=tiles_per_seq),
        grid=(nt,),
        in_specs=[row_spec(d),
                  pl.BlockSpec((1, 1, 3 * d), lambda i: (i // tiles_per_group, 0, 0)),
                  _const_spec((1, d)), _const_spec((1, d)),
                  row_spec(A_WIDTH),
                  row_spec(B_WIDTH),
                  pl.BlockSpec((8, B_WIDTH), lambda i: (jnp.maximum(i * hb - 1, 0), 0)),
                  pl.BlockSpec((8, B_WIDTH), lambda i: (jnp.minimum((i + 1) * hb, nhb - 1), 0)),
                  row_spec(B_WIDTH), row_spec(R_WIDTH), row_spec(F_WIDTH),
                  _const_spec(conv_w.shape), _const_spec(conv_b.shape),
                  cols("a_z", "a_z"), cols("b_z", "b_z"), cols("r_z", "r_z"), cols("f_z", "merge"),
                  *[_layer_spec(w, layer) for w in wos], _layer_spec(wout, layer)],
        out_specs=row_spec(d),
        out_shape=jax.ShapeDtypeStruct((rows, d), F32),
        compiler_params=_cparams(("arbitrary",)),
        name="merge",
    )(x2, mod3, g_pre, g_post, ya, wc, wc, wc, bb, yr, yf, conv_w, conv_b, w_all, w_all, w_all, w_all, *wos, wout)


def _rope_tables(n):
    rows = n // GRID_W
    row = jnp.broadcast_to(jnp.arange(rows)[:, None], (rows, GRID_W)).reshape(-1).astype(F32)
    col = jnp.broadcast_to(jnp.arange(GRID_W)[None, :], (rows, GRID_W)).reshape(-1).astype(F32)
    half = HEAD_DIM // 2
    inv = ROPE_BASE ** (-jnp.arange(0, half, 2, dtype=F32) / half)
    ang_r = row[:, None] * inv
    ang_c = col[:, None] * inv
    cos = jnp.concatenate([jnp.cos(ang_r), jnp.cos(ang_r), jnp.cos(ang_c), jnp.cos(ang_c)], axis=1)
    sin = jnp.concatenate([-jnp.sin(ang_r), jnp.sin(ang_r), -jnp.sin(ang_c), jnp.sin(ang_c)], axis=1)
    return jnp.tile(cos, (1, LANES // HEAD_DIM)), jnp.tile(sin, (1, LANES // HEAD_DIM))


def kernel(x, c, ctx, c_ctx, w_ada, b_ada, norm_pre, norm_post, w_in, attn_sink, conv_w, conv_b, ret_decay,
           w_o_attn, w_o_conv, w_o_ret, w_o_fourier, w_out):
    b, s, d = x.shape
    lc = ctx.shape[1]
    depth = w_in.shape[0]
    rows_grid = s // GRID_W

    cv = jnp.zeros((8, d), F32).at[:b].set(c).at[b].set(c_ctx)
    mods = _ada_call(cv, w_ada, b_ada)

    cos_x, sin_x = _rope_tables(s)
    cos_c = jnp.ones((b * lc, LANES), F32)
    sin_c = jnp.zeros((b * lc, LANES), F32)

    c64, s64 = _dft_cs(HEAD_DIM, HEAD_DIM ** -0.5)
    eye = np.eye(F_WIDTH // HEAD_DIM)
    fd = jnp.asarray(np.concatenate([np.kron(eye, c64), -np.kron(eye, s64)], axis=1), F32)

    lg_all = jax.nn.log_sigmoid(ret_decay.astype(F32))
    zero_state = jnp.zeros((b, R_WIDTH, R_WIDTH), F32)

    w_all = w_in.astype(BF16)
    wos = tuple(w.astype(BF16) for w in (w_o_attn, w_o_conv, w_o_ret, w_o_fourier))
    wout = w_out.astype(BF16)

    tm_x = 1024
    tm_m = 1024
    x2 = x.reshape(b * s, d)
    xc2 = ctx.reshape(b * lc, d)
    for l in range(depth):
        update_ctx = l < depth - 1
        g_pre = norm_pre[l].reshape(1, d)
        g_post = norm_post[l].reshape(1, d)
        mod_x = mods[l, :b].reshape(b, 1, 3 * d)
        mod_c = mods[l, b:b + 1].reshape(1, 1, 3 * d)
        cb = conv_b[l].reshape(1, B_WIDTH)
        lg = lg_all[l]
        lgl = jnp.repeat(lg, HEAD_DIM, axis=1)

        px = _inproj_call(x2, mod_x, g_pre, w_all, l, cos_x, sin_x, fd, tm=tm_x, tiles_per_group=s // tm_x)
        pc = _inproj_call(xc2, mod_c, g_pre, w_all, l, cos_c, sin_c, fd, tm=b * lc, tiles_per_group=1)
        qa, ka, va, wcx, bbx, qr, kr, vr, zf = [t.reshape(b, s, -1) for t in px]
        qac, kac, vac, wcc, bbc, qrc, krc, vrc, zfc = [t.reshape(b, lc, -1) for t in pc]

        ya = _attn_call(attn_sink[l], qa, ka, va, kac, vac, local=True)

        ycf, st_f = _ret_call(lg[0], lgl[0:1], qrc, krc, vrc, zero_state, None, reverse=False)
        ycr, st_b = _ret_call(lg[1], lgl[1:2], qrc, krc, vrc, zero_state, ycf, reverse=True)
        yf_, _ = _ret_call(lg[0], lgl[0:1], qr, kr, vr, st_f, None, reverse=False)
        yr, _ = _ret_call(lg[1], lgl[1:2], qr, kr, vr, st_b, yf_, reverse=True)

        yfo = _fourier_latent(zf, rows_grid)

        x2_new = _merge_call(x2, mod_x, g_pre, g_post, ya.reshape(b * s, -1), wcx.reshape(b * s, -1),
                             bbx.reshape(b * s, -1), yr.reshape(b * s, -1), yfo.reshape(b * s, -1),
                             conv_w[l], cb, w_all, wos, wout, l,
                             tm=tm_m, tiles_per_group=s // tm_m, tiles_per_seq=s // tm_m)
        if update_ctx:
            yac = _attn_call(attn_sink[l], qac, None, None, kac, vac, local=False)
            yfc = _fourier_dense(zfc)
            xc2 = _merge_call(xc2, mod_c, g_pre, g_post, yac.reshape(b * lc, -1), wcc.reshape(b * lc, -1),
                              bbc.reshape(b * lc, -1), ycr.reshape(b * lc, -1), yfc.reshape(b * lc, -1),
                              conv_w[l], cb, w_all, wos, wout, l,
                              tm=lc, tiles_per_group=b, tiles_per_seq=1)
        x2 = x2_new
    return x2.reshape(b, s, d)
```

```python
import functools
import math

import numpy as np
import jax
import jax.numpy as jnp
from jax import lax
from jax.experimental import pallas as pl
from jax.experimental.pallas import tpu as pltpu

D_MODEL = 1024
GRID_W = 64
HEAD_DIM = 64
EPS = 1e-6
NEG_INF = -1e30
A_HEADS = 8
A_KV_HEADS = 2
A_BLOCK = 128
A_WIDTH = A_HEADS * HEAD_DIM
A_KV_WIDTH = A_KV_HEADS * HEAD_DIM
ROPE_BASE = 10000.0
B_WIDTH = 256
R_HEADS = 4
R_WIDTH = R_HEADS * HEAD_DIM
R_STEP = 256
F_WIDTH = 256
N_BRANCH = 4

IN_NAMES = ("a_q", "a_k", "a_v", "a_z", "b_u", "b_b", "b_c", "b_z", "r_q", "r_k", "r_v", "r_z", "f_u", "f_z", "merge")
IN_SIZES = (A_WIDTH, A_KV_WIDTH, A_KV_WIDTH, A_WIDTH, B_WIDTH, B_WIDTH, B_WIDTH, B_WIDTH,
            R_WIDTH, R_WIDTH, R_WIDTH, R_WIDTH, F_WIDTH, F_WIDTH, N_BRANCH * D_MODEL)
_OFFS = dict(zip(IN_NAMES, np.cumsum((0,) + IN_SIZES)[:-1].tolist()))
_SIZE = dict(zip(IN_NAMES, IN_SIZES))

Z_NAMES = ("a_z", "b_z", "r_z", "f_z")
Z_SIZES = tuple(_SIZE[n] for n in Z_NAMES)

MERGE_CHUNK = 256
ATTN_SUB = 8
LOG2E = math.log2(math.e)
LANES = 128
VMEM_LIMIT = 56 * 1024 * 1024

BF16 = jnp.bfloat16
F32 = jnp.float32
HI = lax.Precision.HIGHEST


def _cparams(sem):
    return pltpu.CompilerParams(dimension_semantics=sem, vmem_limit_bytes=VMEM_LIMIT)


def _const_spec(shape):
    nd = len(shape)
    return pl.BlockSpec(shape, lambda *_: (0,) * nd)


def _sigmoid(v):
    return 1.0 / (1.0 + jnp.exp(-v))


def _ada_kernel(cv_ref, w_ref, b_ref, o_ref):
    cv = cv_ref[...]
    s = cv * _sigmoid(cv)
    o_ref[0] = jnp.dot(s, w_ref[0], preferred_element_type=F32, precision=HI) + b_ref[0]


def _ada_call(cv, w_ada, b_ada):
    depth, d, d3 = w_ada.shape
    tn = 1024
    return pl.pallas_call(
        _ada_kernel,
        grid=(depth, d3 // tn),
        in_specs=[pl.BlockSpec((8, d), lambda l, j: (0, 0)),
                  pl.BlockSpec((1, d, tn), lambda l, j: (l, 0, j)),
                  pl.BlockSpec((1, 1, tn), lambda l, j: (l, 0, j))],
        out_specs=pl.BlockSpec((1, 8, tn), lambda l, j: (l, 0, j)),
        out_shape=jax.ShapeDtypeStruct((depth, 8, d3), F32),
        compiler_params=_cparams(("arbitrary", "arbitrary")),
        name="ada_mod",
    )(cv, w_ada, b_ada.reshape(depth, 1, d3))


def _modulated_norm(x, g, mod):
    ms = jnp.mean(x * x, axis=-1, keepdims=True)
    y = x * lax.rsqrt(ms + EPS) * g
    return y * (1.0 + mod[:, D_MODEL:2 * D_MODEL]) + mod[:, 0:D_MODEL]


def _rope(t, cos, sin_signed, first_half):
    outs = []
    for j in range(t.shape[1] // LANES):
        tj = t[:, j * LANES:(j + 1) * LANES]
        partner = jnp.where(first_half, pltpu.roll(tj, LANES - 16, 1), pltpu.roll(tj, 16, 1))
        outs.append(tj * cos + partner * sin_signed)
    return outs[0] if len(outs) == 1 else jnp.concatenate(outs, axis=1)


MIX_GROUPS = (("a_q", "a_v"), ("b_u", "b_c"), ("r_q", "r_v"), ("f_u", "f_u"))


def _inproj_kernel(x_ref, mod_ref, g_ref, wa_ref, wb_ref, wr_ref, wf_ref, cos_ref, sin_ref, fd_ref,
                   qa_ref, ka_ref, va_ref, wc_ref, bb_ref, qr_ref, kr_ref, vr_ref, zf_ref):
    h = _modulated_norm(x_ref[...], g_ref[...], mod_ref[0]).astype(BF16)
    w_refs = {"a": wa_ref, "b": wb_ref, "r": wr_ref, "f": wf_ref}
    starts = {first[0]: _OFFS[first] for first, _ in MIX_GROUPS}

    def proj(name):
        o = _OFFS[name] - starts[name[0]]
        return jnp.dot(h, w_refs[name[0]][0, :, o:o + _SIZE[name]], preferred_element_type=F32)

    cos = cos_ref[...]
    sin = sin_ref[...]
    lane = lax.broadcasted_iota(jnp.int32, cos.shape, 1)
    first_half = (lane % 32) < 16
    low_half = lane < HEAD_DIM
    k_scale = HEAD_DIM ** -0.5
    q = _rope(proj("a_q"), cos, sin, first_half) * (k_scale * LOG2E)
    for p in range(A_WIDTH // LANES):
        qp = q[:, p * LANES:(p + 1) * LANES]
        qa_ref[:, (2 * p) * LANES:(2 * p + 1) * LANES] = jnp.where(low_half, qp, 0.0).astype(BF16)
        qa_ref[:, (2 * p + 1) * LANES:(2 * p + 2) * LANES] = jnp.where(low_half, 0.0, qp).astype(BF16)
    k = _rope(proj("a_k"), cos, sin, first_half)
    k_sw = pltpu.roll(k, HEAD_DIM, 1)
    ka_ref[:, 0:LANES] = jnp.where(low_half, k, k_sw).astype(BF16)
    ka_ref[:, LANES:2 * LANES] = jnp.where(low_half, k_sw, k).astype(BF16)
    v = proj("a_v")
    v_sw = pltpu.roll(v, HEAD_DIM, 1)
    va_ref[:, 0:LANES] = jnp.where(low_half, v, 1.0).astype(BF16)
    va_ref[:, LANES:2 * LANES] = jnp.where(low_half, 1.0, v_sw).astype(BF16)
    va_ref[:, 2 * LANES:3 * LANES] = jnp.where(low_half, v_sw, 1.0).astype(BF16)
    va_ref[:, 3 * LANES:4 * LANES] = jnp.where(low_half, 1.0, v).astype(BF16)
    wc_ref[...] = proj("b_c") * proj("b_u")
    bb_ref[...] = proj("b_b")
    qr_ref[...] = _rope(proj("r_q"), cos, sin, first_half)
    kr_ref[...] = _rope(proj("r_k"), cos, sin, first_half) * k_scale
    vr_ref[...] = proj("r_v")
    zf_ref[...] = jnp.dot(proj("f_u").astype(BF16), fd_ref[...].astype(BF16), preferred_element_type=F32)


def _layer_spec(arr, layer):
    return pl.BlockSpec((1,) + arr.shape[1:], lambda *_: (layer,) + (0,) * (arr.ndim - 1),
                        pipeline_mode=pl.Buffered(1))


def _cols_spec(w_all, layer, first, last):
    lo = _OFFS[first]
    width = _OFFS[last] + _SIZE[last] - lo
    return pl.BlockSpec((pl.Element(1), pl.Element(w_all.shape[1]), pl.Element(width)),
                        lambda *_: (layer, 0, lo), pipeline_mode=pl.Buffered(1))


def _inproj_call(x2, mod3, g_pre, w_all, layer, cos_t, sin_t, fd, *, tm, tiles_per_group):
    rows, d = x2.shape
    nt = rows // tm
    tiles_per_seq = cos_t.shape[0] // tm
    widths = (2 * A_WIDTH, 2 * A_KV_WIDTH, 4 * A_KV_WIDTH, B_WIDTH, B_WIDTH, R_WIDTH, R_WIDTH, R_WIDTH, 2 * F_WIDTH)
    dtypes = (BF16, BF16, BF16, F32, F32, F32, F32, F32, F32)
    row_spec = lambda w: pl.BlockSpec((tm, w), lambda i: (i, 0))
    return pl.pallas_call(
        _inproj_kernel,
        grid=(nt,),
        in_specs=[row_spec(d),
                  pl.BlockSpec((1, 1, 3 * d), lambda i: (i // tiles_per_group, 0, 0)),
                  _const_spec((1, d)),
                  *[_cols_spec(w_all, layer, first, last) for first, last in MIX_GROUPS],
                  pl.BlockSpec((tm, LANES), lambda i: (i % tiles_per_seq, 0)),
                  pl.BlockSpec((tm, LANES), lambda i: (i % tiles_per_seq, 0)),
                  _const_spec(fd.shape)],
        out_specs=[row_spec(w) for w in widths],
        out_shape=[jax.ShapeDtypeStruct((rows, w), dt) for w, dt in zip(widths, dtypes)],
        compiler_params=_cparams(("arbitrary",)),
        name="in_proj",
    )(x2, mod3, g_pre, w_all, w_all, w_all, w_all, cos_t, sin_t, fd)


def _attn_kernel(*refs, local):
    if local:
        (sink_ref, q_ref, kp_ref, kc_ref, kn_ref, vp_ref, vc_ref, vn_ref, kx_ref, vx_ref, o_ref,
         kwin, vwin) = refs
    else:
        sink_ref, q_ref, kx_ref, vx_ref, o_ref = refs
    blk = A_BLOCK
    nsub = q_ref.shape[1] // blk
    n = pl.program_id(1)
    last = pl.num_programs(1) * nsub - 1
    contract_last = (((1,), (1,)), ((), ()))

    lane_q = lax.broadcasted_iota(jnp.int32, (blk, LANES), 1)
    nloc = 3 * blk
    if local:
        step = nsub * blk
        kwin[0:blk] = kp_ref[0]
        kwin[blk:blk + step] = kc_ref[0]
        kwin[blk + step:2 * blk + step] = kn_ref[0]
        vwin[0:blk] = vp_ref[0]
        vwin[blk:blk + step] = vc_ref[0]
        vwin[blk + step:2 * blk + step] = vn_ref[0]
        row = lax.broadcasted_iota(jnp.int32, (blk, nloc), 0)
        col = lax.broadcasted_iota(jnp.int32, (blk, nloc), 1)
        band = jnp.minimum(col - row, row + 2 * blk - col)

    for sub in range(nsub):
        q0 = sub * blk
        if local:
            g_blk = n * nsub + sub
            seq_lo = blk - g_blk * blk
            seq_hi = blk + (last - g_blk + 1) * blk
            valid = (jnp.minimum(band, jnp.minimum(col - seq_lo, seq_hi - 1 - col)) >= 0)[None]
        for j in range(A_KV_HEADS):
            kcol = slice(j * LANES, (j + 1) * LANES)
            ecol = slice(2 * j * LANES, (2 * j + 1) * LANES)
            ocol = slice((2 * j + 1) * LANES, (2 * j + 2) * LANES)
            qs = jnp.concatenate([q_ref[0, q0:q0 + blk, (4 * j + g) * LANES:(4 * j + g + 1) * LANES]
                                  for g in range(4)], axis=0)
            s = lax.dot_general(qs, kx_ref[0, :, kcol], contract_last, preferred_element_type=F32)
            if local:
                s_loc = lax.dot_general(qs, kwin[q0:q0 + nloc, kcol], contract_last, preferred_element_type=F32)
                s_loc = jnp.where(valid, s_loc.reshape(4, blk, nloc), NEG_INF).reshape(4 * blk, nloc)
                s = jnp.concatenate([s_loc, s], axis=1)
            sink = jnp.concatenate(
                [jnp.full((blk, 1), sink_ref[4 * j + g] * LOG2E, F32) for g in range(4)], axis=0)
            m = jnp.maximum(jnp.max(s, axis=1, keepdims=True), sink)
            pb = jnp.exp2((s - m).astype(BF16))
            e_sink = jnp.exp2(sink - m)

            vcol = slice(2 * j * LANES, (2 * j + 2) * LANES)
            if local:
                o_all = (jnp.dot(pb[:, :nloc], vwin[q0:q0 + nloc, vcol], preferred_element_type=F32)
                         + jnp.dot(pb[:, nloc:], vx_ref[0, :, vcol], preferred_element_type=F32))
            else:
                o_all = jnp.dot(pb, vx_ref[0, :, vcol], preferred_element_type=F32)

            for pair in range(2):
                r0 = 2 * pair * blk
                o_even = o_all[r0:r0 + blk, :LANES]
                o_odd = o_all[r0 + blk:r0 + 2 * blk, LANES:]
                num = jnp.where(lane_q < 64, o_even, o_odd)
                den = (pltpu.roll(jnp.where(lane_q < 64, o_odd, o_even), 64, 1)
                       + jnp.where(lane_q < 64, e_sink[r0:r0 + blk], e_sink[r0 + blk:r0 + 2 * blk]))
                c0 = (2 * j + pair) * LANES
                o_ref[0, q0:q0 + blk, c0:c0 + LANES] = num / den


def _attn_call(sink, q, k, v, kx, vx, *, local):
    b, sq, _ = q.shape
    blk = A_BLOCK
    nq = sq // blk
    nsub = min(ATTN_SUB, nq)
    step = nsub * blk
    lx = kx.shape[1]
    smem = pl.BlockSpec(memory_space=pltpu.SMEM)
    kw, vw = kx.shape[2], vx.shape[2]
    q_spec = pl.BlockSpec((1, step, q.shape[2]), lambda bi, n: (bi, n, 0))
    if local:
        def halo(w):
            return (pl.BlockSpec((1, blk, w), lambda bi, n: (bi, jnp.maximum(n * nsub - 1, 0), 0)),
                    pl.BlockSpec((1, step, w), lambda bi, n: (bi, n, 0)),
                    pl.BlockSpec((1, blk, w), lambda bi, n: (bi, jnp.minimum((n + 1) * nsub, nq - 1), 0)))
        in_specs = [smem, q_spec, *halo(kw), *halo(vw),
                    pl.BlockSpec((1, lx, kw), lambda bi, n: (bi, 0, 0)),
                    pl.BlockSpec((1, lx, vw), lambda bi, n: (bi, 0, 0))]
        args = (sink, q, k, k, k, v, v, v, kx, vx)
        scratch = [pltpu.VMEM((step + 2 * blk, kw), k.dtype), pltpu.VMEM((step + 2 * blk, vw), v.dtype)]
    else:
        in_specs = [smem, q_spec, pl.BlockSpec((1, lx, kw), lambda bi, n: (bi, 0, 0)),
                    pl.BlockSpec((1, lx, vw), lambda bi, n: (bi, 0, 0))]
        args = (sink, q, kx, vx)
        scratch = []
    return pl.pallas_call(
        functools.partial(_attn_kernel, local=local),
        grid=(b, nq // nsub),
        in_specs=in_specs,
        out_specs=pl.BlockSpec((1, step, A_WIDTH), lambda bi, n: (bi, n, 0)),
        out_shape=jax.ShapeDtypeStruct((b, sq, A_WIDTH), F32),
        scratch_shapes=scratch,
        compiler_params=_cparams(("arbitrary", "arbitrary")),
        name="win_attn" if local else "ctx_attn",
    )(*args)


def _head_of(shape, dim):
    return lax.broadcasted_iota(jnp.int32, shape, dim) // HEAD_DIM


def _group_mean(t, avg):
    hi = t.astype(BF16)
    lo = (t - hi.astype(F32)).astype(BF16)
    return (jnp.dot(hi, avg, preferred_element_type=F32) + jnp.dot(lo, avg, preferred_element_type=F32))


def _ret_kernel(*refs, reverse, finalize):
    if finalize:
        lg_ref, lgl_ref, q_ref, k_ref, v_ref, r0_ref, yin_ref, y_ref, rfin_ref, r_scr = refs
    else:
        lg_ref, lgl_ref, q_ref, k_ref, v_ref, r0_ref, y_ref, rfin_ref, r_scr = refs
    c = pl.program_id(0)
    nb, ch, w = q_ref.shape

    @pl.when(c == 0)
    def _():
        r_scr[...] = r0_ref[...]

    lgl = lgl_ref[...]
    pos = lax.broadcasted_iota(jnp.int32, (ch, 1), 0).astype(F32)
    if reverse:
        xi = jnp.exp(lgl * (ch - pos))
        zeta = jnp.exp(lgl * pos)
    else:
        xi = jnp.exp(lgl * (pos + 1.0))
        zeta = jnp.exp(lgl * (ch - 1.0 - pos))
    g_chunk = jnp.exp(lgl * float(ch))
    ri = lax.broadcasted_iota(jnp.int32, (ch, ch), 0)
    ci = lax.broadcasted_iota(jnp.int32, (ch, ch), 1)
    diff = (ci - ri) if reverse else (ri - ci)
    dist = jnp.maximum(diff, 0).astype(F32)
    decays = [jnp.where(diff >= 0, jnp.exp(lg_ref[h] * dist), 0.0) for h in range(R_HEADS)]
    lane_head = _head_of((ch, w), 1)
    same_head = _head_of((w, w), 0) == _head_of((w, w), 1)
    avg = jnp.where(same_head, 1.0 / HEAD_DIM, 0.0).astype(BF16)

    for bi in range(nb):
        q = q_ref[bi]
        k = k_ref[bi]
        v = v_ref[bi]
        r = r_scr[bi]
        cross = jnp.dot((q * xi).astype(BF16), r.astype(BF16), preferred_element_type=F32)
        q4 = jnp.concatenate([jnp.where(lane_head == h, q, 0.0) for h in range(R_HEADS)], axis=0).astype(BF16)
        sc = lax.dot_general(q4, k.astype(BF16), (((1,), (1,)), ((), ())), preferred_element_type=F32)
        s4 = jnp.concatenate([(sc[h * ch:(h + 1) * ch] * decays[h]).astype(BF16) for h in range(R_HEADS)], axis=1)
        v4 = jnp.concatenate([jnp.where(lane_head == h, v, 0.0) for h in range(R_HEADS)], axis=0).astype(BF16)
        y = jnp.dot(s4, v4, preferred_element_type=F32) + cross

        kz = (k * zeta).astype(BF16)
        ktv = lax.dot_general(kz, v.astype(BF16), (((0,), (0,)), ((), ())), preferred_element_type=F32)
        r_new = g_chunk * r + jnp.where(same_head, ktv, 0.0)
        r_scr[bi] = r_new
        rfin_ref[bi] = r_new

        if finalize:
            y = y + yin_ref[bi]
            mu = _group_mean(y, avg)
            d = y - mu
            var = _group_mean(d * d, avg)
            y = d * lax.rsqrt(var + EPS)
        y_ref[bi] = y


def _ret_call(lg, lgl, q, k, v, r0, y_in, *, reverse):
    b, s, w = q.shape
    ch = min(R_STEP, s)
    nc = s // ch
    finalize = y_in is not None
    cidx = (lambda c: (0, nc - 1 - c, 0)) if reverse else (lambda c: (0, c, 0))
    chunk = pl.BlockSpec((b, ch, w), cidx)
    state = pl.BlockSpec((b, w, w), lambda c: (0, 0, 0))
    in_specs = [pl.BlockSpec(memory_space=pltpu.SMEM), pl.BlockSpec((1, w), lambda c: (0, 0)),
                chunk, chunk, chunk, state]
    args = [lg, lgl, q, k, v, r0]
    if finalize:
        in_specs.append(chunk)
        args.append(y_in)
    return pl.pallas_call(
        functools.partial(_ret_kernel, reverse=reverse, finalize=finalize),
        grid=(nc,),
        in_specs=in_specs,
        out_specs=[chunk, state],
        out_shape=[jax.ShapeDtypeStruct((b, s, w), F32), jax.ShapeDtypeStruct((b, w, w), F32)],
        scratch_shapes=[pltpu.VMEM((b, w, w), F32)],
        compiler_params=_cparams(("arbitrary",)),
        name="retention_bwd" if reverse else "retention_fwd",
    )(*args)


def _dft_cs(n, scale):
    a = 2.0 * np.pi * np.outer(np.arange(n), np.arange(n)) / n
    return np.cos(a) * scale, np.sin(a) * scale


def _fourier_rows_kernel(m_ref, tc_ref, ts_ref, z_ref, o_ref):
    rows, cb = z_ref.shape[1], z_ref.shape[2]
    m = m_ref[...].astype(BF16)
    for ci in range(cb):
        z = z_ref[0, :, ci, :]
        zz = jnp.concatenate([z[:, :F_WIDTH], z[:, F_WIDTH:]], axis=0).astype(BF16)
        a = jnp.dot(m, zz, preferred_element_type=F32)
        a_re, a_im = a[:rows], a[rows:]
        tc = jnp.concatenate([tc_ref[ci]] * (F_WIDTH // LANES), axis=1)
        ts = jnp.concatenate([ts_ref[ci]] * (F_WIDTH // LANES), axis=1)
        o_ref[0, :, 0, ci, :] = (a_re * tc + a_im * ts).astype(o_ref.dtype)
        o_ref[0, :, 1, ci, :] = (a_im * tc - a_re * ts).astype(o_ref.dtype)


def _fourier_cols_kernel(g_ref, b_ref, o_ref):
    g = g_ref[...].astype(BF16)
    kt, _, cw, f = b_ref.shape[1:]
    for i in range(kt):
        bm = b_ref[0, i].reshape(2 * cw, f).astype(BF16)
        o_ref[0, :, i, :] = jnp.dot(g, bm, preferred_element_type=F32)


def _fourier_latent(zf, rows):
    b, s, _ = zf.shape
    cw = GRID_W
    c_r, s_r = _dft_cs(rows, rows ** -0.5)
    m1 = jnp.asarray(np.block([[c_r, s_r], [-s_r, c_r]]), F32)
    ang = 2.0 * np.pi * np.outer(np.arange(cw), np.arange(rows)) / s
    tc = jnp.asarray(np.repeat(np.cos(ang)[:, :, None], LANES, axis=2), F32)
    ts = jnp.asarray(np.repeat(np.sin(ang)[:, :, None], LANES, axis=2), F32)
    cb = 32
    bk = pl.pallas_call(
        _fourier_rows_kernel,
        grid=(cw // cb, b),
        in_specs=[_const_spec(m1.shape),
                  pl.BlockSpec((cb, rows, LANES), lambda j, bi: (j, 0, 0)),
                  pl.BlockSpec((cb, rows, LANES), lambda j, bi: (j, 0, 0)),
                  pl.BlockSpec((1, rows, cb, 2 * F_WIDTH), lambda j, bi: (bi, 0, j, 0))],
        out_specs=pl.BlockSpec((1, rows, 2, cb, F_WIDTH), lambda j, bi: (bi, 0, 0, j, 0)),
        out_shape=jax.ShapeDtypeStruct((b, rows, 2, cw, F_WIDTH), F32),
        compiler_params=_cparams(("arbitrary", "arbitrary")),
        name="fourier_rows",
    )(m1, tc, ts, zf.reshape(b, rows, cw, 2 * F_WIDTH))
    c_c, s_c = _dft_cs(cw, cw ** -0.5)
    g = jnp.asarray(np.concatenate([c_c, s_c], axis=1), F32)
    kt = 64
    out = pl.pallas_call(
        _fourier_cols_kernel,
        grid=(b, rows // kt),
        in_specs=[_const_spec(g.shape),
                  pl.BlockSpec((1, kt, 2, cw, F_WIDTH), lambda bi, i: (bi, i, 0, 0, 0))],
        out_specs=pl.BlockSpec((1, cw, kt, F_WIDTH), lambda bi, i: (bi, 0, i, 0)),
        out_shape=jax.ShapeDtypeStruct((b, cw, rows, F_WIDTH), F32),
        compiler_params=_cparams(("arbitrary", "arbitrary")),
        name="fourier_cols",
    )(g, bk)
    return out.reshape(b, s, F_WIDTH)


def _fourier_dense_kernel(m_ref, z_ref, o_ref):
    z = z_ref[0].astype(F32)
    zz = jnp.concatenate([z[:, :F_WIDTH], z[:, F_WIDTH:]], axis=0)
    o_ref[0] = jnp.dot(m_ref[...], zz, preferred_element_type=F32, precision=HI)


def _fourier_dense(zf):
    b, n, _ = zf.shape
    c_n, s_n = _dft_cs(n, n ** -0.5)
    m = jnp.asarray(np.concatenate([c_n, s_n], axis=1), F32)
    return pl.pallas_call(
        _fourier_dense_kernel,
        grid=(b,),
        in_specs=[_const_spec(m.shape), pl.BlockSpec((1, n, 2 * F_WIDTH), lambda bi: (bi, 0, 0))],
        out_specs=pl.BlockSpec((1, n, F_WIDTH), lambda bi: (bi, 0, 0)),
        out_shape=jax.ShapeDtypeStruct((b, n, F_WIDTH), F32),
        compiler_params=_cparams(("arbitrary",)),
        name="fourier_dense",
    )(m, zf)


def _merge_kernel(x_ref, mod_ref, gpre_ref, gpost_ref, ya_ref, wc_ref, wprev_ref, wnext_ref, bb_ref,
                  yr_ref, yf_ref, cw_ref, cb_ref, wza_ref, wzb_ref, wzr_ref, wfm_ref,
                  woa_ref, wob_ref, wor_ref, wof_ref, wout_ref, o_ref, *, seq_rows):
    i = pl.program_id(0)
    x = x_ref[...]
    mod = mod_ref[0]
    h = _modulated_norm(x, gpre_ref[...], mod).astype(BF16)
    tm = x.shape[0]

    wc = wc_ref[...]
    row = lax.broadcasted_iota(jnp.int32, wc.shape, 0)
    pos = (i * tm + row) % seq_rows
    up = jnp.where(row == 0, wprev_ref[7:8, :], pltpu.roll(wc, 1, 0))
    dn = jnp.where(row == tm - 1, wnext_ref[0:1, :], pltpu.roll(wc, tm - 1, 0))
    up = jnp.where(pos == 0, 0.0, up)
    dn = jnp.where(pos == seq_rows - 1, 0.0, dn)
    conv = up * cw_ref[0:1, :] + wc * cw_ref[1:2, :] + dn * cw_ref[2:3, :] + cb_ref[...]
    yb = bb_ref[...] * conv

    ys = (ya_ref[...], yb, yr_ref[...], yf_ref[...])
    wo_refs = (woa_ref, wob_ref, wor_ref, wof_ref)
    fz = _SIZE["f_z"]
    wz = (wza_ref[0], wzb_ref[0], wzr_ref[0], wfm_ref[0, :, 0:fz])
    acts = []
    for br in range(N_BRANCH):
        z = jnp.dot(h, wz[br], preferred_element_type=F32)
        acts.append((ys[br] * (z * _sigmoid(z))).astype(BF16))
    y = jnp.zeros((tm, D_MODEL), F32)
    for c0 in range(0, D_MODEL, MERGE_CHUNK):
        total = jnp.zeros((tm, MERGE_CHUNK), F32)
        for br in range(N_BRANCH):
            proj = jnp.dot(acts[br], wo_refs[br][0, :, c0:c0 + MERGE_CHUNK], preferred_element_type=F32)
            g0 = fz + br * D_MODEL + c0
            gate = _sigmoid(jnp.dot(h, wfm_ref[0, :, g0:g0 + MERGE_CHUNK], preferred_element_type=F32))
            total = total + gate * proj
        y = y + jnp.dot(total.astype(BF16), wout_ref[0, c0:c0 + MERGE_CHUNK, :], preferred_element_type=F32)
    ms = jnp.mean(y * y, axis=-1, keepdims=True)
    yn = y * lax.rsqrt(ms + EPS) * gpost_ref[...]
    o_ref[...] = x + mod[:, 2 * D_MODEL:3 * D_MODEL] * yn


def _merge_call(x2, mod3, g_pre, g_post, ya, wc, bb, yr, yf, conv_w, conv_b, w_all, wos, wout, layer,
                *, tm, tiles_per_group, seq_rows):
    rows, d = x2.shape
    nt = rows // tm
    hb = tm // 8
    nhb = rows // 8
    row_spec = lambda w: pl.BlockSpec((tm, w), lambda i: (i, 0))
    cols = functools.partial(_cols_spec, w_all, layer)
    return pl.pallas_call(
        functools.partial(_merge_kernel, seq_rows=seq_rows),
        grid=(nt,),
        in_specs=[row_spec(d),
                  pl.BlockSpec((1, 1, 3 * d), lambda i: (i // tiles_per_group, 0, 0)),
                  _const_spec((1, d)), _const_spec((1, d)),
                  row_spec(A_WIDTH),
                  row_spec(B_WIDTH),
                  pl.BlockSpec((8, B_WIDTH), lambda i: (jnp.maximum(i * hb - 1, 0), 0)),
                  pl.BlockSpec((8, B_WIDTH), lambda i: (jnp.minimum((i + 1) * hb, nhb - 1), 0)),
                  row_spec(B_WIDTH), row_spec(R_WIDTH), row_spec(F_WIDTH),
                  _const_spec(conv_w.shape), _const_spec(conv_b.shape),
                  cols("a_z", "a_z"), cols("b_z", "b_z"), cols("r_z", "r_z"), cols("f_z", "merge"),
                  *[_layer_spec(w, layer) for w in wos], _layer_spec(wout, layer)],
        out_specs=row_spec(d),
        out_shape=jax.ShapeDtypeStruct((rows, d), F32),
        compiler_params=_cparams(("arbitrary",)),
        name="merge",
    )(x2, mod3, g_pre, g_post, ya, wc, wc, wc, bb, yr, yf, conv_w, conv_b, w_all, w_all, w_all, w_all, *wos, wout)


def _rope_tables(n):
    rows = n // GRID_W
    row = jnp.broadcast_to(jnp.arange(rows)[:, None], (rows, GRID_W)).reshape(-1).astype(F32)
    col = jnp.broadcast_to(jnp.arange(GRID_W)[None, :], (rows, GRID_W)).reshape(-1).astype(F32)
    half = HEAD_DIM // 2
    inv = ROPE_BASE ** (-jnp.arange(0, half, 2, dtype=F32) / half)
    ang_r = row[:, None] * inv
    ang_c = col[:, None] * inv
    cos = jnp.concatenate([jnp.cos(ang_r), jnp.cos(ang_r), jnp.cos(ang_c), jnp.cos(ang_c)], axis=1)
    sin = jnp.concatenate([-jnp.sin(ang_r), jnp.sin(ang_r), -jnp.sin(ang_c), jnp.sin(ang_c)], axis=1)
    return jnp.tile(cos, (1, LANES // HEAD_DIM)), jnp.tile(sin, (1, LANES // HEAD_DIM))


def kernel(x, c, ctx, c_ctx, w_ada, b_ada, norm_pre, norm_post, w_in, attn_sink, conv_w, conv_b, ret_decay,
           w_o_attn, w_o_conv, w_o_ret, w_o_fourier, w_out):
    b, s, d = x.shape
    lc = ctx.shape[1]
    depth = w_in.shape[0]
    rows_grid = s // GRID_W

    cv = jnp.zeros((8, d), F32).at[:b].set(c).at[b].set(c_ctx)
    mods = _ada_call(cv, w_ada, b_ada)

    cos_x, sin_x = _rope_tables(s)
    cos_c = jnp.ones((b * lc, LANES), F32)
    sin_c = jnp.zeros((b * lc, LANES), F32)

    c64, s64 = _dft_cs(HEAD_DIM, HEAD_DIM ** -0.5)
    eye = np.eye(F_WIDTH // HEAD_DIM)
    fd = jnp.asarray(np.concatenate([np.kron(eye, c64), -np.kron(eye, s64)], axis=1), F32)

    lg_all = jax.nn.log_sigmoid(ret_decay.astype(F32))
    zero_state = jnp.zeros((b, R_WIDTH, R_WIDTH), F32)

    w_all = w_in.astype(BF16)
    wos = tuple(w.astype(BF16) for w in (w_o_attn, w_o_conv, w_o_ret, w_o_fourier))
    wout = w_out.astype(BF16)

    tm_x = 1024
    tm_m = 1024
    x2 = x.reshape(b * s, d)
    xc2 = ctx.reshape(b * lc, d)
    for l in range(depth):
        update_ctx = l < depth - 1
        g_pre = norm_pre[l].reshape(1, d)
        g_post = norm_post[l].reshape(1, d)
        mod_x = mods[l, :b].reshape(b, 1, 3 * d)
        mod_c = mods[l, b:b + 1].reshape(1, 1, 3 * d)
        cb = conv_b[l].reshape(1, B_WIDTH)
        lg = lg_all[l]
        lgl = jnp.repeat(lg, HEAD_DIM, axis=1)

        px = _inproj_call(x2, mod_x, g_pre, w_all, l, cos_x, sin_x, fd, tm=tm_x, tiles_per_group=s // tm_x)
        pc = _inproj_call(xc2, mod_c, g_pre, w_all, l, cos_c, sin_c, fd, tm=b * lc, tiles_per_group=1)
        qa, ka, va, wcx, bbx, qr, kr, vr, zf = [t.reshape(b, s, -1) for t in px]
        qac, kac, vac, wcc, bbc, qrc, krc, vrc, zfc = [t.reshape(b, lc, -1) for t in pc]

        ya = _attn_call(attn_sink[l], qa, ka, va, kac, vac, local=True)

        ycf, st_f = _ret_call(lg[0], lgl[0:1], qrc, krc, vrc, zero_state, None, reverse=False)
        ycr, st_b = _ret_call(lg[1], lgl[1:2], qrc, krc, vrc, zero_state, ycf, reverse=True)
        yf_, _ = _ret_call(lg[0], lgl[0:1], qr, kr, vr, st_f, None, reverse=False)
        yr, _ = _ret_call(lg[1], lgl[1:2], qr, kr, vr, st_b, yf_, reverse=True)

        yfo = _fourier_latent(zf, rows_grid)

        x2_new = _merge_call(x2, mod_x, g_pre, g_post, ya.reshape(b * s, -1), wcx.reshape(b * s, -1),
                             bbx.reshape(b * s, -1), yr.reshape(b * s, -1), yfo.reshape(b * s, -1),
                             conv_w[l], cb, w_all, wos, wout, l,
                             tm=tm_m, tiles_per_group=s // tm_m, seq_rows=s)
        if update_ctx:
            yac = _attn_call(attn_sink[l], qac, None, None, kac, vac, local=False)
            yfc = _fourier_dense(zfc)
            xc2 = _merge_call(xc2, mod_c, g_pre, g_post, yac.reshape(b * lc, -1), wcc.reshape(b * lc, -1),
                              bbc.reshape(b * lc, -1), ycr.reshape(b * lc, -1), yfc.reshape(b * lc, -1),
                              conv_w[l], cb, w_all, wos, wout, l,
                              tm=b * lc, tiles_per_group=1, seq_rows=lc)
        x2 = x2_new
    return x2.reshape(b, s, d)
```
